```python
import math
import jax
import jax.numpy as jnp
from jax import lax
import numpy as np

D_MODEL = 1024
BATCH = 8
SEQ = 2048
DEPTH = 4

N_MIXERS = 2
N_SUBLAYERS = 3
ATTN_HEADS = 8
ATTN_HEAD_DIM = D_MODEL // (2 * ATTN_HEADS)
ATTN_V_DIM = 2 * ATTN_HEAD_DIM
Q_BLOCK = 128
NUM_BUCKETS = 32
MAX_DISTANCE = 128
HGRN_EXPAND = 128
HGRN_HEADS = D_MODEL // HGRN_EXPAND
HGRN_FORGET_DIM = HGRN_HEADS * HGRN_EXPAND
HGRN_HEAD_V = D_MODEL // HGRN_HEADS
HGRN_CHUNK = 64
D_FF = 2816
N_ATTN_LAYERS = (DEPTH + 1) // 2
N_HGRN_LAYERS = DEPTH // 2
EPS = 1e-6

kernel_name = 'hybrid_diffattn_hgrn2_macaron_adaln'


def rms_norm(h, gain):
    h32 = h.astype(jnp.float32)
    h32 = h32 * lax.rsqrt(jnp.mean(h32 * h32, axis=-1, keepdims=True) + EPS)
    return (h32 * gain.astype(jnp.float32)).astype(h.dtype)


def t5_causal_bucket(dist):
    max_exact = NUM_BUCKETS // 2
    d = jnp.maximum(dist, 0)
    d_f = jnp.maximum(d, 1).astype(jnp.float32)
    large = max_exact + (jnp.log(d_f / max_exact) / math.log(MAX_DISTANCE / max_exact)
                         * (NUM_BUCKETS - max_exact)).astype(jnp.int32)
    large = jnp.minimum(large, NUM_BUCKETS - 1)
    return jnp.where(d < max_exact, d, large)


def swiglu(h, w_in, w_down):
    a, b = jnp.split(h @ w_in, 2, axis=-1)
    return (jax.nn.silu(a) * b) @ w_down


def diff_attention(h, w_qkv, w_o, q_gain, k_gain, lam, subln_gain, rel_bias, layer_idx):
    B, S, _ = h.shape
    H, hd, dv = ATTN_HEADS, ATTN_HEAD_DIM, ATTN_V_DIM
    q, k, v = jnp.split(h @ w_qkv, 3, axis=-1)
    q = rms_norm(q.reshape(B, S, H, 2, hd), q_gain) * (hd ** -0.5)
    k = rms_norm(k.reshape(B, S, H, 2, hd), k_gain)
    v = v.reshape(B, S, H, dv)
    lambda_init = 0.8 - 0.6 * math.exp(-0.3 * layer_idx)
    lam32 = lam.astype(jnp.float32)
    lam_full = (jnp.exp(jnp.sum(lam32[0] * lam32[1])) - jnp.exp(jnp.sum(lam32[2] * lam32[3]))
                + lambda_init)
    kt = k.transpose(0, 2, 3, 1, 4)
    vt = v.transpose(0, 2, 1, 3)
    nb = S // Q_BLOCK
    qb = q.reshape(B, nb, Q_BLOCK, H, 2, hd).transpose(1, 0, 3, 4, 2, 5)
    k_pos = jnp.arange(S)

    def block(args):
        q_blk, blk = args
        q_pos = blk * Q_BLOCK + jnp.arange(Q_BLOCK)
        dist = q_pos[:, None] - k_pos[None, :]
        bias = rel_bias[t5_causal_bucket(dist)].astype(jnp.float32).transpose(2, 0, 1)
        s = jnp.einsum('bhcqd,bhckd->bhcqk', q_blk, kt).astype(jnp.float32) + bias[None, :, None]
        s = jnp.where(dist >= 0, s, -jnp.inf)
        p = jax.nn.softmax(s, axis=-1)
        a = p[:, :, 0] - lam_full * p[:, :, 1]
        return jnp.einsum('bhqk,bhkv->bhqv', a, vt)

    o = lax.map(block, (qb, jnp.arange(nb)))
    o = o.transpose(1, 0, 3, 2, 4).reshape(B, S, H, dv)
    o = rms_norm(o, subln_gain) * (1.0 - lambda_init)
    return o.reshape(B, S, H * dv).astype(h.dtype) @ w_o


def hgrn2(h, w_in, w_o, out_gain, lb):
    B, S, _ = h.shape
    H, dk, dv, C = HGRN_HEADS, HGRN_EXPAND, HGRN_HEAD_V, HGRN_CHUNK
    F = HGRN_FORGET_DIM
    q, f, i, g = jnp.split(h @ w_in, [F, 2 * F, 2 * F + D_MODEL], axis=-1)
    f32 = f.astype(jnp.float32)
    lb32 = lb.astype(jnp.float32)
    log_forget = jnp.logaddexp(jnp.log(lb32), jnp.log1p(-lb32) + jax.nn.log_sigmoid(f32))
    k = (1.0 - lb32) * jax.nn.sigmoid(-f32)
    nc = S // C

    def to_chunks(t, d):
        return t.astype(jnp.float32).reshape(B, nc, C, H, d).transpose(1, 0, 3, 2, 4)

    qs, ks, gs = to_chunks(q, dk), to_chunks(k, dk), to_chunks(log_forget, dk)
    vs = to_chunks(i, dv)
    causal = jnp.tril(jnp.ones((C, C), dtype=bool))[None, None, :, :, None]

    def step(state, inp):
        qc, kc, vc, gc = inp
        G = jnp.cumsum(gc, axis=2)
        o_inter = jnp.einsum('bhtk,bhkv->bhtv', qc * jnp.exp(G), state)
        diff = G[:, :, :, None, :] - G[:, :, None, :, :]
        decay = jnp.exp(jnp.where(causal, diff, -jnp.inf))
        scores = jnp.einsum('bhtk,bhsk,bhtsk->bhts', qc, kc, decay)
        o_intra = jnp.einsum('bhts,bhsv->bhtv', scores, vc)
        G_last = G[:, :, -1:, :]
        new_state = (jnp.exp(G_last[:, :, 0, :])[..., None] * state
                     + jnp.einsum('bhsk,bhsv->bhkv', kc * jnp.exp(G_last - G), vc))
        return new_state, o_inter + o_intra

    state0 = jnp.zeros((B, H, dk, dv), jnp.float32)
    _, o = lax.scan(step, state0, (qs, ks, vs, gs))
    o = o.transpose(1, 0, 3, 2, 4).reshape(B, S, H, dv)
    o = rms_norm(o, out_gain.reshape(H, dv)) * jax.nn.silu(g.astype(jnp.float32)).reshape(B, S, H, dv)
    return o.reshape(B, S, D_MODEL).astype(h.dtype) @ w_o


def _normal(key, shape, scale):
    return jax.random.normal(key, shape, jnp.float32) * scale


def setup_inputs(seed: int = 0) -> dict:
    key = jax.random.key(seed)
    ks = jax.random.split(key, 20)
    D, F2 = D_MODEL, 2 * HGRN_FORGET_DIM + 2 * D_MODEL
    return {
        'x': _normal(ks[0], (BATCH, SEQ, D), 1.0),
        'c': _normal(ks[1], (BATCH, D), 1.0),
        'ada_w': _normal(ks[2], (DEPTH, D, N_SUBLAYERS * 3 * D), 0.5 * D ** -0.5),
        'ada_b': _normal(ks[3], (DEPTH, N_SUBLAYERS * 3 * D), 0.02),
        'norm_g': 1.0 + _normal(ks[4], (DEPTH, N_SUBLAYERS, D), 0.02),
        'ffn_w_in': _normal(ks[5], (DEPTH, 2, D, 2 * D_FF), D ** -0.5),
        'ffn_w_down': _normal(ks[6], (DEPTH, 2, D_FF, D), D_FF ** -0.5),
        'attn_w_qkv': _normal(ks[7], (N_ATTN_LAYERS, D, 3 * D), D ** -0.5),
        'attn_w_o': _normal(ks[8], (N_ATTN_LAYERS, D, D), D ** -0.5),
        'attn_q_gain': 1.0 + _normal(ks[9], (N_ATTN_LAYERS, ATTN_HEAD_DIM), 0.02),
        'attn_k_gain': 1.0 + _normal(ks[10], (N_ATTN_LAYERS, ATTN_HEAD_DIM), 0.02),
        'attn_lambda': _normal(ks[11], (N_ATTN_LAYERS, 4, ATTN_HEAD_DIM), 0.1),
        'attn_subln_gain': 1.0 + _normal(ks[12], (N_ATTN_LAYERS, ATTN_V_DIM), 0.02),
        'rel_bias': _normal(ks[13], (NUM_BUCKETS, ATTN_HEADS), 0.5),
        'hgrn_w_in': _normal(ks[14], (N_HGRN_LAYERS, D, F2), D ** -0.5),
        'hgrn_w_o': _normal(ks[15], (N_HGRN_LAYERS, D, D), D ** -0.5),
        'hgrn_out_gain': 1.0 + _normal(ks[16], (N_HGRN_LAYERS, D), 0.02),
        'hgrn_lb_logits': _normal(ks[17], (N_HGRN_LAYERS, HGRN_FORGET_DIM), 0.1),
    }


def reference(x, c, ada_w, ada_b, norm_g, ffn_w_in, ffn_w_down, attn_w_qkv, attn_w_o,
              attn_q_gain, attn_k_gain, attn_lambda, attn_subln_gain, rel_bias,
              hgrn_w_in, hgrn_w_o, hgrn_out_gain, hgrn_lb_logits):
    B = x.shape[0]
    lb_all = jnp.cumsum(jax.nn.softmax(hgrn_lb_logits.astype(jnp.float32), axis=0), axis=0)
    lb_all = lb_all - lb_all[0:1]
    c_act = jax.nn.silu(c)
    for layer in range(DEPTH):
        mod = (c_act @ ada_w[layer] + ada_b[layer]).reshape(B, N_SUBLAYERS, 3, D_MODEL)
        shift = mod[:, :, 0][:, None]
        scale = mod[:, :, 1][:, None]
        gate = mod[:, :, 2][:, None]
        h = rms_norm(x, norm_g[layer, 0]) * (1.0 + scale[:, :, 0]) + shift[:, :, 0]
        x = x + 0.5 * gate[:, :, 0] * swiglu(h, ffn_w_in[layer, 0], ffn_w_down[layer, 0])
        h = rms_norm(x, norm_g[layer, 1]) * (1.0 + scale[:, :, 1]) + shift[:, :, 1]
        j = layer // N_MIXERS
        if layer % N_MIXERS == 0:
            y = diff_attention(h, attn_w_qkv[j], attn_w_o[j], attn_q_gain[j], attn_k_gain[j],
                               attn_lambda[j], attn_subln_gain[j], rel_bias, layer)
        else:
            y = hgrn2(h, hgrn_w_in[j], hgrn_w_o[j], hgrn_out_gain[j], lb_all[j])
        x = x + gate[:, :, 1] * y
        h = rms_norm(x, norm_g[layer, 2]) * (1.0 + scale[:, :, 2]) + shift[:, :, 2]
        x = x + 0.5 * gate[:, :, 2] * swiglu(h, ffn_w_in[layer, 1], ffn_w_down[layer, 1])
    return x
```

```python
import functools
import math

import numpy as np
import jax
import jax.numpy as jnp
from jax import lax
from jax.experimental import pallas as pl
from jax.experimental.pallas import tpu as pltpu

F32 = jnp.float32
BF16 = jnp.bfloat16

EPS = 1e-6
N_SUBLAYERS = 3
N_MIXERS = 2
ATTN_HEADS = 8
ATTN_HEAD_DIM = 64
ATTN_V_DIM = 128
NUM_BUCKETS = 32
MAX_DISTANCE = 128
HGRN_HEADS = 8
HGRN_DK = 128
HGRN_DV = 128

LANES = 128
VMEM_LIMIT = 56 * 1024 * 1024
NEG = -1e30

ATTN_TILE = 256
HGRN_CHUNK = 128


def _cparams(n_axes):
    return pltpu.CompilerParams(
        dimension_semantics=("arbitrary",) * n_axes, vmem_limit_bytes=VMEM_LIMIT)


def _resident(shape):
    return pl.BlockSpec(shape, lambda *_: (0,) * len(shape), pipeline_mode=pl.Buffered(1))


def _sigmoid(z):
    return 1.0 / (1.0 + jnp.exp(-z))


def _dot(a, b):
    return jnp.dot(a, b, preferred_element_type=F32)


def _dot_nt(a, b):
    return lax.dot_general(a, b, (((1,), (1,)), ((), ())), preferred_element_type=F32)


def _dot_tn(a, b):
    return lax.dot_general(a, b, (((0,), (0,)), ((), ())), preferred_element_type=F32)


def _norm_mod(x, gain, mod_ref):
    ms = jnp.mean(x * x, axis=-1, keepdims=True)
    h = x * lax.rsqrt(ms + EPS) * gain
    return h * (1.0 + mod_ref[0, 1:2, :]) + mod_ref[0, 0:1, :]


def _ada_kernel(c_ref, w_ref, b_ref, o_ref):
    c = c_ref[...]
    ca = (c * _sigmoid(c)).astype(BF16)
    o_ref[0] = _dot(ca, w_ref[0].astype(BF16)) + b_ref[0]


def _ada_call(c, ada_w, ada_b):
    depth, d, n = ada_w.shape
    b = c.shape[0]
    tn = 1152
    assert n % tn == 0
    return pl.pallas_call(
        _ada_kernel,
        grid=(depth, n // tn),
        in_specs=[
            pl.BlockSpec((b, d), lambda l, j: (0, 0)),
            pl.BlockSpec((1, d, tn), lambda l, j: (l, 0, j)),
            pl.BlockSpec((1, 1, tn), lambda l, j: (l, 0, j)),
        ],
        out_specs=pl.BlockSpec((1, b, tn), lambda l, j: (l, 0, j)),
        out_shape=jax.ShapeDtypeStruct((depth, b, n), F32),
        compiler_params=_cparams(2),
        name="ada_mod",
    )(c, ada_w, ada_b.reshape(depth, 1, n))


def _ffn_kernel(x_ref, mod_ref, g_ref, win_ref, wdn_ref, o_ref, act_ref, *, d_ff, tf):
    x = x_ref[...]
    h = _norm_mod(x, g_ref[...], mod_ref).astype(BF16)
    for j in range(d_ff // tf):
        a = _dot(h, win_ref[:, j * tf:(j + 1) * tf])
        b = _dot(h, win_ref[:, d_ff + j * tf:d_ff + (j + 1) * tf])
        act_ref[:, j * tf:(j + 1) * tf] = (a * _sigmoid(a) * b).astype(BF16)
    y = _dot(act_ref[...], wdn_ref[...])
    o_ref[...] = x + 0.5 * mod_ref[0, 2:3, :] * y


def _ffn_call(x2, mod3, gain, w_in, w_down, seq):
    t, d = x2.shape
    d_ff = w_down.shape[0]
    tm = min(512, seq)
    tf = 1408
    assert t % tm == 0 and seq % tm == 0 and d_ff % tf == 0
    per_seq = seq // tm
    return pl.pallas_call(
        functools.partial(_ffn_kernel, d_ff=d_ff, tf=tf),
        grid=(t // tm,),
        in_specs=[
            pl.BlockSpec((tm, d), lambda i: (i, 0)),
            pl.BlockSpec((1, 3, d), lambda i: (i // per_seq, 0, 0)),
            _resident((1, d)),
            _resident((d, 2 * d_ff)),
            _resident((d_ff, d)),
        ],
        out_specs=pl.BlockSpec((tm, d), lambda i: (i, 0)),
        out_shape=jax.ShapeDtypeStruct((t, d), F32),
        scratch_shapes=[pltpu.VMEM((tm, d_ff), BF16)],
        compiler_params=_cparams(1),
        name="ffn_half_step",
    )(x2, mod3, gain, w_in, w_down)


def _out_kernel(x_ref, o_ref, mod_ref, w_ref, y_ref):
    y = _dot(o_ref[...], w_ref[...])
    y_ref[...] = x_ref[...] + mod_ref[0, 2:3, :] * y


def _out_call(x2, o2, mod3, w_o, seq):
    t, d = x2.shape
    tm = min(1024, seq)
    assert t % tm == 0 and seq % tm == 0
    per_seq = seq // tm
    return pl.pallas_call(
        _out_kernel,
        grid=(t // tm,),
        in_specs=[
            pl.BlockSpec((tm, d), lambda i: (i, 0)),
            pl.BlockSpec((tm, d), lambda i: (i, 0)),
            pl.BlockSpec((1, 3, d), lambda i: (i // per_seq, 0, 0)),
            _resident((d, d)),
        ],
        out_specs=pl.BlockSpec((tm, d), lambda i: (i, 0)),
        out_shape=jax.ShapeDtypeStruct((t, d), F32),
        compiler_params=_cparams(1),
        name="mixer_out_proj",
    )(x2, o2, mod3, w_o)


def _group_mean_sq(z, ones_blk):
    sq = z * z
    hi = sq.astype(BF16)
    lo = (sq - hi.astype(F32)).astype(BF16)
    return _dot(hi, ones_blk) + _dot(lo, ones_blk)


def _qkv_kernel(x_ref, mod_ref, g_ref, w_ref, qg_ref, kg_ref, q_ref, k_ref, v_ref, *, d):
    h = _norm_mod(x_ref[...], g_ref[...], mod_ref).astype(BF16)
    r = lax.broadcasted_iota(jnp.int32, (LANES, LANES), 0) // ATTN_HEAD_DIM
    c = lax.broadcasted_iota(jnp.int32, (LANES, LANES), 1) // ATTN_HEAD_DIM
    ones_blk = jnp.where(r == c, 1.0 / ATTN_HEAD_DIM, 0.0).astype(BF16)
    q_scale = ATTN_HEAD_DIM ** -0.5
    for j in range(d // LANES):
        sl = slice(j * LANES, (j + 1) * LANES)
        q = _dot(h, w_ref[:, j * LANES:(j + 1) * LANES])
        q = q * lax.rsqrt(_group_mean_sq(q, ones_blk) + EPS) * qg_ref[...] * q_scale
        q_ref[:, sl] = q.astype(BF16)
        k = _dot(h, w_ref[:, d + j * LANES:d + (j + 1) * LANES])
        k = k * lax.rsqrt(_group_mean_sq(k, ones_blk) + EPS) * kg_ref[...]
        k_ref[:, sl] = k.astype(BF16)
    v_ref[...] = _dot(h, w_ref[:, 2 * d:3 * d]).astype(BF16)


def _qkv_call(x2, mod3, gain, w_qkv, q_gain2, k_gain2, seq):
    t, d = x2.shape
    tm = min(512, seq)
    assert t % tm == 0 and seq % tm == 0
    per_seq = seq // tm
    out = jax.ShapeDtypeStruct((t, d), BF16)
    row = pl.BlockSpec((tm, d), lambda i: (i, 0))
    return pl.pallas_call(
        functools.partial(_qkv_kernel, d=d),
        grid=(t // tm,),
        in_specs=[
            row,
            pl.BlockSpec((1, 3, d), lambda i: (i // per_seq, 0, 0)),
            _resident((1, d)),
            _resident((d, 3 * d)),
            _resident((1, LANES)),
            _resident((1, LANES)),
        ],
        out_specs=[row, row, row],
        out_shape=[out, out, out],
        compiler_params=_cparams(1),
        name="attn_qkv_proj",
    )(x2, mod3, gain, w_qkv, q_gain2, k_gain2)


def _bucket_thresholds():
    max_exact = NUM_BUCKETS // 2
    dist = np.arange(0, 4 * MAX_DISTANCE)
    d_f = np.maximum(dist, 1).astype(np.float32)
    large = max_exact + (np.log(d_f / np.float32(max_exact)) / np.float32(math.log(MAX_DISTANCE / max_exact))
                         * np.float32(NUM_BUCKETS - max_exact)).astype(np.int32)
    bucket = np.where(dist < max_exact, dist, np.minimum(large, NUM_BUCKETS - 1))
    assert np.all(np.diff(bucket) >= 0) and bucket[-1] == NUM_BUCKETS - 1
    return [int(np.argmax(bucket >= b)) for b in range(NUM_BUCKETS)]


def _bias_kernel(rb_ref, o_ref, *, thresholds, tile):
    h = pl.program_id(0)
    i = lax.broadcasted_iota(jnp.int32, (tile, tile), 0)
    j = lax.broadcasted_iota(jnp.int32, (tile, tile), 1)
    for off in range(3):
        dist = off * tile + i - j
        val = jnp.full((tile, tile), rb_ref[0, h], F32)
        for b in range(1, NUM_BUCKETS):
            val = jnp.where(dist >= thresholds[b], rb_ref[b, h], val)
        o_ref[0, off] = jnp.where(dist >= 0, val, NEG)


def _bias_call(rel_bias, tile):
    thresholds = _bucket_thresholds()
    assert thresholds[NUM_BUCKETS - 1] <= tile + 1
    heads = rel_bias.shape[1]
    return pl.pallas_call(
        functools.partial(_bias_kernel, thresholds=thresholds, tile=tile),
        grid=(heads,),
        in_specs=[pl.BlockSpec(memory_space=pltpu.SMEM)],
        out_specs=pl.BlockSpec((1, 3, tile, tile), lambda h: (h, 0, 0, 0)),
        out_shape=jax.ShapeDtypeStruct((heads, 3, tile, tile), F32),
        compiler_params=_cparams(1),
        name="attn_rel_bias_tiles",
    )(rel_bias)


def _attn_kernel(q_ref, k_ref, v_ref, bias_ref, lam_ref, sg_ref, o_ref, m_ref, l_ref, acc_ref,
                 *, tile, lambda_init):
    qi = pl.program_id(2)
    q = q_ref[...]
    lane = lax.broadcasted_iota(jnp.int32, q.shape, 1)
    zero = jnp.zeros_like(q)
    qs = jnp.concatenate([jnp.where(lane < ATTN_HEAD_DIM, q, zero),
                          jnp.where(lane >= ATTN_HEAD_DIM, q, zero)], axis=0)
    m_ref[...] = jnp.full(m_ref.shape, NEG, F32)
    l_ref[...] = jnp.zeros(l_ref.shape, F32)
    acc_ref[...] = jnp.zeros(acc_ref.shape, F32)

    def body(kj, carry):
        start = pl.multiple_of(kj * tile, tile)
        kb = k_ref[pl.ds(start, tile), :]
        vb = v_ref[pl.ds(start, tile), :]
        bt = bias_ref[0, jnp.minimum(qi - kj, 2)]
        s = _dot_nt(qs, kb) + jnp.concatenate([bt, bt], axis=0)
        m_old = m_ref[...]
        m_new = jnp.maximum(m_old, jnp.max(s, axis=-1, keepdims=True))
        alpha = jnp.exp(m_old - m_new)
        p = jnp.exp(s - m_new[:, 0:1])
        l_ref[...] = alpha * l_ref[...] + jnp.sum(p, axis=-1, keepdims=True)
        acc_ref[...] = alpha * acc_ref[...] + _dot(p.astype(BF16), vb)
        m_ref[...] = m_new
        return carry

    lax.fori_loop(0, qi + 1, body, 0)

    lam = lam_ref[...]
    lam_full = (jnp.exp(jnp.sum(lam[0:1] * lam[1:2], axis=-1, keepdims=True))
                - jnp.exp(jnp.sum(lam[2:3] * lam[3:4], axis=-1, keepdims=True)) + lambda_init)
    o_all = acc_ref[...] / l_ref[...]
    o = o_all[:tile] - lam_full * o_all[tile:]
    ms = jnp.mean(o * o, axis=-1, keepdims=True)
    o = o * lax.rsqrt(ms + EPS) * sg_ref[...] * (1.0 - lambda_init)
    o_ref[...] = o.astype(BF16)


def _attn_call(q2, k2, v2, bias_tiles, lam, subln_gain2, batch, seq, lambda_init):
    t, d = q2.shape
    tile = ATTN_TILE
    heads = d // ATTN_V_DIM
    nq = seq // tile
    assert seq % tile == 0
    kv_spec = pl.BlockSpec((seq, ATTN_V_DIM), lambda b, h, i: (b, h))
    q_spec = pl.BlockSpec((tile, ATTN_V_DIM), lambda b, h, i: (b * nq + i, h))
    return pl.pallas_call(
        functools.partial(_attn_kernel, tile=tile, lambda_init=lambda_init),
        grid=(batch, heads, nq),
        in_specs=[
            q_spec, kv_spec, kv_spec,
            pl.BlockSpec((1, 3, tile, tile), lambda b, h, i: (h, 0, 0, 0)),
            pl.BlockSpec(lam.shape, lambda b, h, i: (0, 0)),
            pl.BlockSpec((1, ATTN_V_DIM), lambda b, h, i: (0, 0)),
        ],
        out_specs=q_spec,
        out_shape=jax.ShapeDtypeStruct((t, d), BF16),
        scratch_shapes=[
            pltpu.VMEM((2 * tile, LANES), F32),
            pltpu.VMEM((2 * tile, LANES), F32),
            pltpu.VMEM((2 * tile, ATTN_V_DIM), F32),
        ],
        compiler_params=_cparams(3),
        name="diff_attention",
    )(q2, k2, v2, bias_tiles, lam, subln_gain2)


def _hgrn_in_kernel(x_ref, mod_ref, g_ref, w_ref, lbl_ref, q_ref, k_ref, lf_ref, v_ref, og_ref,
                    *, d, layer_j):
    h = _norm_mod(x_ref[...], g_ref[...], mod_ref).astype(BF16)
    logits = lbl_ref[...]
    e = jnp.exp(logits - jnp.max(logits, axis=0, keepdims=True))
    p = e / jnp.sum(e, axis=0, keepdims=True)
    lb = jnp.zeros((1, d), F32)
    for i in range(1, layer_j + 1):
        lb = lb + p[i:i + 1, :]
    log_lb = jnp.log(lb)
    log_1m_lb = jnp.log1p(-lb)
    tn = 4 * LANES
    for j in range(d // tn):
        sl = slice(j * tn, (j + 1) * tn)
        q_ref[:, sl] = _dot(h, w_ref[:, j * tn:(j + 1) * tn])
        f = _dot(h, w_ref[:, d + j * tn:d + (j + 1) * tn])
        ef = jnp.exp(-jnp.abs(f))
        log_sig = jnp.minimum(f, 0.0) - jnp.log1p(ef)
        a = log_lb[:, sl]
        b = log_1m_lb[:, sl] + log_sig
        lf_ref[:, sl] = jnp.maximum(a, b) + jnp.log1p(jnp.exp(-jnp.abs(a - b)))
        k_ref[:, sl] = (1.0 - lb[:, sl]) * (jnp.where(f >= 0.0, ef, 1.0) / (1.0 + ef))
        v_ref[:, sl] = _dot(h, w_ref[:, 2 * d + j * tn:2 * d + (j + 1) * tn]).astype(BF16)
        g = _dot(h, w_ref[:, 3 * d + j * tn:3 * d + (j + 1) * tn])
        og_ref[:, sl] = g * _sigmoid(g)


def _hgrn_in_call(x2, mod3, gain, w_in, lb_logits, layer_j, seq):
    t, d = x2.shape
    tm = min(512, seq)
    assert t % tm == 0 and seq % tm == 0
    per_seq = seq // tm
    row = pl.BlockSpec((tm, d), lambda i: (i, 0))
    f32_out = jax.ShapeDtypeStruct((t, d), F32)
    return pl.pallas_call(
        functools.partial(_hgrn_in_kernel, d=d, layer_j=layer_j),
        grid=(t // tm,),
        in_specs=[
            row,
            pl.BlockSpec((1, 3, d), lambda i: (i // per_seq, 0, 0)),
            _resident((1, d)),
            _resident((d, 4 * d)),
            _resident(lb_logits.shape),
        ],
        out_specs=[row, row, row, row, row],
        out_shape=[f32_out, f32_out, f32_out, jax.ShapeDtypeStruct((t, d), BF16), f32_out],
        compiler_params=_cparams(1),
        name="hgrn_in_proj",
    )(x2, mod3, gain, w_in, lb_logits)


def _rows_at(g_cum, half):
    c = g_cum.shape[0]
    if half >= 4:
        blk = 2 * half
        return jnp.concatenate(
            [jnp.broadcast_to(g_cum[p * blk + half - 1:p * blk + half, :], (blk, g_cum.shape[1]))
             for p in range(c // blk)], axis=0)
    row = lax.broadcasted_iota(jnp.int32, g_cum.shape, 0)
    if half == 2:
        r4 = row % 4
        up1 = pltpu.roll(g_cum, c - 1, 0)
        dn1 = pltpu.roll(g_cum, 1, 0)
        dn2 = pltpu.roll(g_cum, 2, 0)
        return jnp.where(r4 == 0, up1, jnp.where(r4 == 1, g_cum, jnp.where(r4 == 2, dn1, dn2)))
    assert half == 1
    return jnp.where(row % 2 == 0, g_cum, pltpu.roll(g_cum, 1, 0))


def _hgrn_rec_kernel(q_ref, k_ref, lf_ref, v_ref, og_ref, gain_ref, o_ref, st_ref, *, chunk):
    seq = q_ref.shape[0]
    c = chunk
    st_ref[...] = jnp.zeros(st_ref.shape, F32)
    ri = lax.broadcasted_iota(jnp.int32, (c, c), 0)
    ci = lax.broadcasted_iota(jnp.int32, (c, c), 1)
    tri = jnp.where(ri >= ci, 1.0, 0.0).astype(BF16)
    rc_xor = ri ^ ci
    row = lax.broadcasted_iota(jnp.int32, (c, HGRN_DK), 0)

    def body(n, carry):
        sl = pl.ds(pl.multiple_of(n * c, c), c)
        g = lf_ref[sl, :]
        q = q_ref[sl, :]
        k = k_ref[sl, :]
        v = v_ref[sl, :]
        g_hi = g.astype(BF16)
        r1 = g - g_hi.astype(F32)
        g_mid = r1.astype(BF16)
        g_lo = (r1 - g_mid.astype(F32)).astype(BF16)
        g_cum = _dot(tri, g_hi) + (_dot(tri, g_mid) + _dot(tri, g_lo))
        g_last = g_cum[c - 1:c, :]

        state = st_ref[...]
        o = _dot_nt((q * jnp.exp(g_cum)).astype(BF16), state.astype(BF16))

        scores = _dot_nt(q.astype(BF16), k.astype(BF16))
        half = 1
        while half < c:
            e = jnp.exp(-jnp.abs(g_cum - _rows_at(g_cum, half)))
            right = (row % (2 * half)) >= half
            ql = jnp.where(right, q * e, 0.0).astype(BF16)
            kl = jnp.where(right, 0.0, k * e).astype(BF16)
            scores = jnp.where(rc_xor >= half, _dot_nt(ql, kl), scores)
            half *= 2
        o = o + _dot(scores.astype(BF16), v)

        k_dec = (k * jnp.exp(g_last - g_cum)).astype(BF16)
        st_ref[...] = jnp.exp(g_last) * state + _dot_tn(v, k_dec)

        ms = jnp.mean(o * o, axis=-1, keepdims=True)
        o_ref[sl, :] = (o * lax.rsqrt(ms + EPS) * gain_ref[...] * og_ref[sl, :]).astype(BF16)
        return carry

    lax.fori_loop(0, seq // c, body, 0)


def _hgrn_rec_call(q2, k2, lf2, v2, og2, out_gain2, batch, seq):
    t, d = q2.shape
    heads = d // HGRN_DV
    assert seq % HGRN_CHUNK == 0
    blk = pl.BlockSpec((seq, HGRN_DV), lambda b, h: (b, h))
    return pl.pallas_call(
        functools.partial(_hgrn_rec_kernel, chunk=HGRN_CHUNK),
        grid=(batch, heads),
        in_specs=[blk, blk, blk, blk, blk, pl.BlockSpec((1, HGRN_DV), lambda b, h: (0, h))],
        out_specs=blk,
        out_shape=jax.ShapeDtypeStruct((t, d), BF16),
        scratch_shapes=[pltpu.VMEM((HGRN_DV, HGRN_DK), F32)],
        compiler_params=_cparams(2),
        name="hgrn_recurrence",
    )(q2, k2, lf2, v2, og2, out_gain2)


def kernel(x, c, ada_w, ada_b, norm_g, ffn_w_in, ffn_w_down, attn_w_qkv, attn_w_o, attn_q_gain,
           attn_k_gain, attn_lambda, attn_subln_gain, rel_bias, hgrn_w_in, hgrn_w_o,
           hgrn_out_gain, hgrn_lb_logits):
    batch, seq, d = x.shape
    depth = ada_w.shape[0]
    mod = _ada_call(c, ada_w, ada_b).reshape(depth, batch, N_SUBLAYERS, 3, d)
    bias_tiles = _bias_call(rel_bias, ATTN_TILE)
    x2 = x.reshape(batch * seq, d)
    for layer in range(depth):
        gains = norm_g[layer].reshape(N_SUBLAYERS, 1, d)
        x2 = _ffn_call(x2, mod[layer, :, 0], gains[0], ffn_w_in[layer, 0].astype(BF16),
                       ffn_w_down[layer, 0].astype(BF16), seq)
        j = layer // N_MIXERS
        if layer % N_MIXERS == 0:
            reps = LANES // ATTN_HEAD_DIM
            q2, k2, v2 = _qkv_call(
                x2, mod[layer, :, 1], gains[1], attn_w_qkv[j].astype(BF16),
                jnp.tile(attn_q_gain[j], reps).reshape(1, LANES),
                jnp.tile(attn_k_gain[j], reps).reshape(1, LANES), seq)
            lambda_init = 0.8 - 0.6 * math.exp(-0.3 * layer)
            o2 = _attn_call(q2, k2, v2, bias_tiles, attn_lambda[j],
                            attn_subln_gain[j].reshape(1, ATTN_V_DIM), batch, seq, lambda_init)
            x2 = _out_call(x2, o2, mod[layer, :, 1], attn_w_o[j].astype(BF16), seq)
        else:
            q2, k2, lf2, v2, og2 = _hgrn_in_call(
                x2, mod[layer, :, 1], gains[1], hgrn_w_in[j].astype(BF16), hgrn_lb_logits, j, seq)
            o2 = _hgrn_rec_call(q2, k2, lf2, v2, og2, hgrn_out_gain[j].reshape(1, d), batch, seq)
            x2 = _out_call(x2, o2, mod[layer, :, 1], hgrn_w_o[j].astype(BF16), seq)
        x2 = _ffn_call(x2, mod[layer, :, 2], gains[2], ffn_w_in[layer, 1].astype(BF16),
                       ffn_w_down[layer, 1].astype(BF16), seq)
    return x2.reshape(batch, seq, d)
```

```python
import functools
import math

import numpy as np
import jax
import jax.numpy as jnp
from jax import lax
from jax.experimental import pallas as pl
from jax.experimental.pallas import tpu as pltpu

F32 = jnp.float32
BF16 = jnp.bfloat16

EPS = 1e-6
N_SUBLAYERS = 3
N_MIXERS = 2
ATTN_HEADS = 8
ATTN_HEAD_DIM = 64
ATTN_V_DIM = 128
NUM_BUCKETS = 32
MAX_DISTANCE = 128
HGRN_HEADS = 8
HGRN_DK = 128
HGRN_DV = 128

LANES = 128
VMEM_LIMIT = 56 * 1024 * 1024
NEG = -1e30

ATTN_TILE = 256
QKV_COLS = 256
HGRN_CHUNK = 128


def _cparams(n_axes):
    return pltpu.CompilerParams(
        dimension_semantics=("arbitrary",) * n_axes, vmem_limit_bytes=VMEM_LIMIT)


def _resident(shape):
    return pl.BlockSpec(shape, lambda *_: (0,) * len(shape), pipeline_mode=pl.Buffered(1))


def _sigmoid(z):
    return 1.0 / (1.0 + jnp.exp(-z))


def _dot(a, b):
    return jnp.dot(a, b, preferred_element_type=F32)


def _dot_nt(a, b):
    return lax.dot_general(a, b, (((1,), (1,)), ((), ())), preferred_element_type=F32)


def _dot_tn(a, b):
    return lax.dot_general(a, b, (((0,), (0,)), ((), ())), preferred_element_type=F32)


def _norm_mod(x, gain, mod_ref):
    ms = jnp.mean(x * x, axis=-1, keepdims=True)
    h = x * lax.rsqrt(ms + EPS) * gain
    return h * (1.0 + mod_ref[0, 1:2, :]) + mod_ref[0, 0:1, :]


def _ada_kernel(c_ref, w_ref, b_ref, o_ref):
    c = c_ref[...]
    ca = (c * _sigmoid(c)).astype(BF16)
    o_ref[0] = _dot(ca, w_ref[0].astype(BF16)) + b_ref[0]


def _ada_call(c, ada_w, ada_b):
    depth, d, n = ada_w.shape
    b = c.shape[0]
    tn = 1152
    assert n % tn == 0
    return pl.pallas_call(
        _ada_kernel,
        grid=(depth, n // tn),
        in_specs=[
            pl.BlockSpec((b, d), lambda l, j: (0, 0)),
            pl.BlockSpec((1, d, tn), lambda l, j: (l, 0, j)),
            pl.BlockSpec((1, 1, tn), lambda l, j: (l, 0, j)),
        ],
        out_specs=pl.BlockSpec((1, b, tn), lambda l, j: (l, 0, j)),
        out_shape=jax.ShapeDtypeStruct((depth, b, n), F32),
        compiler_params=_cparams(2),
        name="ada_mod",
    )(c, ada_w, ada_b.reshape(depth, 1, n))


def _ffn_kernel(x_ref, mod_ref, g_ref, win_ref, wdn_ref, o_ref, act_ref, *, d_ff, tf):
    x = x_ref[...]
    h = _norm_mod(x, g_ref[...], mod_ref).astype(BF16)
    for j in range(d_ff // tf):
        a = _dot(h, win_ref[:, j * tf:(j + 1) * tf])
        b = _dot(h, win_ref[:, d_ff + j * tf:d_ff + (j + 1) * tf])
        act_ref[:, j * tf:(j + 1) * tf] = (a * _sigmoid(a) * b).astype(BF16)
    y = _dot(act_ref[...], wdn_ref[...])
    o_ref[...] = x + 0.5 * mod_ref[0, 2:3, :] * y


def _ffn_call(x2, mod3, gain, w_in, w_down, seq):
    t, d = x2.shape
    d_ff = w_down.shape[0]
    tm = min(512, seq)
    tf = 1408
    assert t % tm == 0 and seq % tm == 0 and d_ff % tf == 0
    per_seq = seq // tm
    return pl.pallas_call(
        functools.partial(_ffn_kernel, d_ff=d_ff, tf=tf),
        grid=(t // tm,),
        in_specs=[
            pl.BlockSpec((tm, d), lambda i: (i, 0)),
            pl.BlockSpec((1, 3, d), lambda i: (i // per_seq, 0, 0)),
            _resident((1, d)),
            _resident((d, 2 * d_ff)),
            _resident((d_ff, d)),
        ],
        out_specs=pl.BlockSpec((tm, d), lambda i: (i, 0)),
        out_shape=jax.ShapeDtypeStruct((t, d), F32),
        scratch_shapes=[pltpu.VMEM((tm, d_ff), BF16)],
        compiler_params=_cparams(1),
        name="ffn_half_step",
    )(x2, mod3, gain, w_in, w_down)


def _out_kernel(x_ref, o_ref, mod_ref, w_ref, y_ref):
    y = _dot(o_ref[...], w_ref[...])
    y_ref[...] = x_ref[...] + mod_ref[0, 2:3, :] * y


def _out_call(x2, o2, mod3, w_o, seq):
    t, d = x2.shape
    tm = min(1024, seq)
    assert t % tm == 0 and seq % tm == 0
    per_seq = seq // tm
    return pl.pallas_call(
        _out_kernel,
        grid=(t // tm,),
        in_specs=[
            pl.BlockSpec((tm, d), lambda i: (i, 0)),
            pl.BlockSpec((tm, d), lambda i: (i, 0)),
            pl.BlockSpec((1, 3, d), lambda i: (i // per_seq, 0, 0)),
            _resident((d, d)),
        ],
        out_specs=pl.BlockSpec((tm, d), lambda i: (i, 0)),
        out_shape=jax.ShapeDtypeStruct((t, d), F32),
        compiler_params=_cparams(1),
        name="mixer_out_proj",
    )(x2, o2, mod3, w_o)


def _group_mean_sq(z, ones_blk):
    return _dot((z * z).astype(BF16), ones_blk)


def _qkv_kernel(x_ref, mod_ref, g_ref, w_ref, qg_ref, kg_ref, q_ref, k_ref, v_ref, *, d):
    h = _norm_mod(x_ref[...], g_ref[...], mod_ref).astype(BF16)
    tn = QKV_COLS
    r = lax.broadcasted_iota(jnp.int32, (tn, tn), 0) // ATTN_HEAD_DIM
    c = lax.broadcasted_iota(jnp.int32, (tn, tn), 1) // ATTN_HEAD_DIM
    ones_blk = jnp.where(r == c, 1.0 / ATTN_HEAD_DIM, 0.0).astype(BF16)
    q_scale = ATTN_HEAD_DIM ** -0.5
    for j in range(d // tn):
        sl = slice(j * tn, (j + 1) * tn)
        q = _dot(h, w_ref[:, j * tn:(j + 1) * tn])
        q = q * lax.rsqrt(_group_mean_sq(q, ones_blk) + EPS) * (qg_ref[...] * q_scale)
        q_ref[:, sl] = q.astype(BF16)
        k = _dot(h, w_ref[:, d + j * tn:d + (j + 1) * tn])
        k = k * lax.rsqrt(_group_mean_sq(k, ones_blk) + EPS) * kg_ref[...]
        k_ref[:, sl] = k.astype(BF16)
    v_ref[...] = _dot(h, w_ref[:, 2 * d:3 * d]).astype(BF16)


def _qkv_call(x2, mod3, gain, w_qkv, q_gain2, k_gain2, seq):
    t, d = x2.shape
    tm = min(512, seq)
    assert t % tm == 0 and seq % tm == 0
    per_seq = seq // tm
    out = jax.ShapeDtypeStruct((t, d), BF16)
    row = pl.BlockSpec((tm, d), lambda i: (i, 0))
    return pl.pallas_call(
        functools.partial(_qkv_kernel, d=d),
        grid=(t // tm,),
        in_specs=[
            row,
            pl.BlockSpec((1, 3, d), lambda i: (i // per_seq, 0, 0)),
            _resident((1, d)),
            _resident((d, 3 * d)),
            _resident((1, QKV_COLS)),
            _resident((1, QKV_COLS)),
        ],
        out_specs=[row, row, row],
        out_shape=[out, out, out],
        compiler_params=_cparams(1),
        name="attn_qkv_proj",
    )(x2, mod3, gain, w_qkv, q_gain2, k_gain2)


def _bucket_thresholds():
    max_exact = NUM_BUCKETS // 2
    dist = np.arange(0, 4 * MAX_DISTANCE)
    d_f = np.maximum(dist, 1).astype(np.float32)
    large = max_exact + (np.log(d_f / np.float32(max_exact)) / np.float32(math.log(MAX_DISTANCE / max_exact))
                         * np.float32(NUM_BUCKETS - max_exact)).astype(np.int32)
    bucket = np.where(dist < max_exact, dist, np.minimum(large, NUM_BUCKETS - 1))
    assert np.all(np.diff(bucket) >= 0) and bucket[-1] == NUM_BUCKETS - 1
    return [int(np.argmax(bucket >= b)) for b in range(NUM_BUCKETS)]


def _bias_kernel(rb_ref, o_ref, *, thresholds, tile):
    h = pl.program_id(0)
    i = lax.broadcasted_iota(jnp.int32, (tile, tile), 0)
    j = lax.broadcasted_iota(jnp.int32, (tile, tile), 1)
    for off in range(3):
        dist = off * tile + i - j
        val = jnp.full((tile, tile), rb_ref[0, h], F32)
        for b in range(1, NUM_BUCKETS):
            val = jnp.where(dist >= thresholds[b], rb_ref[b, h], val)
        o_ref[0, off] = jnp.where(dist >= 0, val, NEG)


def _bias_call(rel_bias, tile):
    thresholds = _bucket_thresholds()
    assert thresholds[NUM_BUCKETS - 1] <= tile + 1
    heads = rel_bias.shape[1]
    return pl.pallas_call(
        functools.partial(_bias_kernel, thresholds=thresholds, tile=tile),
        grid=(heads,),
        in_specs=[pl.BlockSpec(memory_space=pltpu.SMEM)],
        out_specs=pl.BlockSpec((1, 3, tile, tile), lambda h: (h, 0, 0, 0)),
        out_shape=jax.ShapeDtypeStruct((heads, 3, tile, tile), F32),
        compiler_params=_cparams(1),
        name="attn_rel_bias_tiles",
    )(rel_bias)


def _attn_kernel(q_ref, k_ref, v_ref, bias_ref, lam_ref, sg_ref, o_ref, v1_ref, s_ref, p_ref,
                 *, tile, lambda_init):
    seq = q_ref.shape[0]
    dv = ATTN_V_DIM
    v1_ref[:, :dv] = v_ref[...]
    v1_ref[:, dv:] = jnp.ones((seq, dv), BF16)
    lam = lam_ref[...]
    lam_full = (jnp.exp(jnp.sum(lam[0:1] * lam[1:2], axis=-1, keepdims=True))
                - jnp.exp(jnp.sum(lam[2:3] * lam[3:4], axis=-1, keepdims=True)) + lambda_init)
    lane = lax.broadcasted_iota(jnp.int32, (tile, dv), 1)
    for i in range(seq // tile):
        q = q_ref[i * tile:(i + 1) * tile, :]
        zero = jnp.zeros_like(q)
        qs = jnp.concatenate([jnp.where(lane < ATTN_HEAD_DIM, q, zero),
                              jnp.where(lane >= ATTN_HEAD_DIM, q, zero)], axis=0)
        slot = i % 2
        n_keys = (i + 1) * tile
        for j in range(i + 1):
            bt = bias_ref[0, min(i - j, 2)]
            s_ref[slot, :, j * tile:(j + 1) * tile] = (
                _dot_nt(qs, k_ref[j * tile:(j + 1) * tile, :]) + jnp.concatenate([bt, bt], axis=0))
        m = jnp.max(s_ref[slot, :, :n_keys], axis=-1, keepdims=True)
        p_ref[slot, :, :n_keys] = jnp.exp(s_ref[slot, :, :n_keys] - m).astype(BF16)
        acc = _dot(p_ref[slot, :, :n_keys], v1_ref[:n_keys, :])
        o_all = acc[:, :dv] / acc[:, dv:]
        o = o_all[:tile] - lam_full * o_all[tile:]
        ms = jnp.mean(o * o, axis=-1, keepdims=True)
        o = o * lax.rsqrt(ms + EPS) * sg_ref[...] * (1.0 - lambda_init)
        o_ref[i * tile:(i + 1) * tile, :] = o.astype(BF16)


def _attn_call(q2, k2, v2, bias_tiles, lam, subln_gain2, batch, seq, lambda_init):
    t, d = q2.shape
    tile = ATTN_TILE
    heads = d // ATTN_V_DIM
    assert seq % tile == 0
    blk = pl.BlockSpec((seq, ATTN_V_DIM), lambda b, h: (b, h))
    return pl.pallas_call(
        functools.partial(_attn_kernel, tile=tile, lambda_init=lambda_init),
        grid=(batch, heads),
        in_specs=[
            blk, blk, blk,
            pl.BlockSpec((1, 3, tile, tile), lambda b, h: (h, 0, 0, 0)),
            pl.BlockSpec(lam.shape, lambda b, h: (0, 0)),
            pl.BlockSpec((1, ATTN_V_DIM), lambda b, h: (0, 0)),
        ],
        out_specs=blk,
        out_shape=jax.ShapeDtypeStruct((t, d), BF16),
        scratch_shapes=[
            pltpu.VMEM((seq, 2 * ATTN_V_DIM), BF16),
            pltpu.VMEM((2, 2 * tile, seq), F32),
            pltpu.VMEM((2, 2 * tile, seq), BF16),
        ],
        compiler_params=_cparams(2),
        name="diff_attention",
    )(q2, k2, v2, bias_tiles, lam, subln_gain2)


def _hgrn_in_kernel(x_ref, mod_ref, g_ref, w_ref, lbl_ref, q_ref, k_ref, lf_ref, v_ref, og_ref,
                    *, d, layer_j):
    h = _norm_mod(x_ref[...], g_ref[...], mod_ref).astype(BF16)
    logits = lbl_ref[...]
    e = jnp.exp(logits - jnp.max(logits, axis=0, keepdims=True))
    p = e / jnp.sum(e, axis=0, keepdims=True)
    lb = jnp.zeros((1, d), F32)
    for i in range(1, layer_j + 1):
        lb = lb + p[i:i + 1, :]
    log_lb = jnp.log(lb)
    log_1m_lb = jnp.log1p(-lb)
    tn = 4 * LANES
    for j in range(d // tn):
        sl = slice(j * tn, (j + 1) * tn)
        q_ref[:, sl] = _dot(h, w_ref[:, j * tn:(j + 1) * tn])
        f = _dot(h, w_ref[:, d + j * tn:d + (j + 1) * tn])
        ef = jnp.exp(-jnp.abs(f))
        log_sig = jnp.minimum(f, 0.0) - jnp.log1p(ef)
        a = log_lb[:, sl]
        b = log_1m_lb[:, sl] + log_sig
        lf_ref[:, sl] = jnp.maximum(a, b) + jnp.log1p(jnp.exp(-jnp.abs(a - b)))
        k_ref[:, sl] = (1.0 - lb[:, sl]) * (jnp.where(f >= 0.0, ef, 1.0) / (1.0 + ef))
        v_ref[:, sl] = _dot(h, w_ref[:, 2 * d + j * tn:2 * d + (j + 1) * tn]).astype(BF16)
        g = _dot(h, w_ref[:, 3 * d + j * tn:3 * d + (j + 1) * tn])
        og_ref[:, sl] = g * _sigmoid(g)


def _hgrn_in_call(x2, mod3, gain, w_in, lb_logits, layer_j, seq):
    t, d = x2.shape
    tm = min(512, seq)
    assert t % tm == 0 and seq % tm == 0
    per_seq = seq // tm
    row = pl.BlockSpec((tm, d), lambda i: (i, 0))
    f32_out = jax.ShapeDtypeStruct((t, d), F32)
    return pl.pallas_call(
        functools.partial(_hgrn_in_kernel, d=d, layer_j=layer_j),
        grid=(t // tm,),
        in_specs=[
            row,
            pl.BlockSpec((1, 3, d), lambda i: (i // per_seq, 0, 0)),
            _resident((1, d)),
            _resident((d, 4 * d)),
            _resident(lb_logits.shape),
        ],
        out_specs=[row, row, row, row, row],
        out_shape=[f32_out, f32_out, f32_out, jax.ShapeDtypeStruct((t, d), BF16), f32_out],
        compiler_params=_cparams(1),
        name="hgrn_in_proj",
    )(x2, mod3, gain, w_in, lb_logits)


def _rows_at(g_cum, half):
    c = g_cum.shape[0]
    if half >= 4:
        blk = 2 * half
        return jnp.concatenate(
            [jnp.broadcast_to(g_cum[p * blk + half - 1:p * blk + half, :], (blk, g_cum.shape[1]))
             for p in range(c // blk)], axis=0)
    row = lax.broadcasted_iota(jnp.int32, g_cum.shape, 0)
    if half == 2:
        r4 = row % 4
        up1 = pltpu.roll(g_cum, c - 1, 0)
        dn1 = pltpu.roll(g_cum, 1, 0)
        dn2 = pltpu.roll(g_cum, 2, 0)
        return jnp.where(r4 == 0, up1, jnp.where(r4 == 1, g_cum, jnp.where(r4 == 2, dn1, dn2)))
    assert half == 1
    return jnp.where(row % 2 == 0, g_cum, pltpu.roll(g_cum, 1, 0))


def _hgrn_rec_kernel(q_ref, k_ref, lf_ref, v_ref, og_ref, gain_ref, o_ref, *, chunk):
    seq = q_ref.shape[0]
    c = chunk
    ri = lax.broadcasted_iota(jnp.int32, (c, c), 0)
    ci = lax.broadcasted_iota(jnp.int32, (c, c), 1)
    tri = jnp.where(ri >= ci, 1.0, 0.0).astype(BF16)
    rc_xor = ri ^ ci
    row = lax.broadcasted_iota(jnp.int32, (c, HGRN_DK), 0)

    state = jnp.zeros((HGRN_DV, HGRN_DK), F32)
    for n in range(seq // c):
        sl = slice(n * c, (n + 1) * c)
        g = lf_ref[sl, :]
        q = q_ref[sl, :]
        k = k_ref[sl, :]
        v = v_ref[sl, :]
        g_hi = g.astype(BF16)
        r1 = g - g_hi.astype(F32)
        g_mid = r1.astype(BF16)
        g_lo = (r1 - g_mid.astype(F32)).astype(BF16)
        g_cum = _dot(tri, g_hi) + (_dot(tri, g_mid) + _dot(tri, g_lo))
        g_last = g_cum[c - 1:c, :]

        o = _dot_nt((q * jnp.exp(g_cum)).astype(BF16), state.astype(BF16))

        scores = _dot_nt(q.astype(BF16), k.astype(BF16))
        half = 1
        while half < c:
            e = jnp.exp(-jnp.abs(g_cum - _rows_at(g_cum, half)))
            right = (row % (2 * half)) >= half
            ql = jnp.where(right, q * e, 0.0).astype(BF16)
            kl = jnp.where(right, 0.0, k * e).astype(BF16)
            scores = jnp.where(rc_xor >= half, _dot_nt(ql, kl), scores)
            half *= 2
        o = o + _dot(scores.astype(BF16), v)

        k_dec = (k * jnp.exp(g_last - g_cum)).astype(BF16)
        state = jnp.exp(g_last) * state + _dot_tn(v, k_dec)

        ms = jnp.mean(o * o, axis=-1, keepdims=True)
        o_ref[sl, :] = (o * lax.rsqrt(ms + EPS) * gain_ref[...] * og_ref[sl, :]).astype(BF16)


def _hgrn_rec_call(q2, k2, lf2, v2, og2, out_gain2, batch, seq):
    t, d = q2.shape
    heads = d // HGRN_DV
    assert seq % HGRN_CHUNK == 0
    blk = pl.BlockSpec((seq, HGRN_DV), lambda b, h: (b, h))
    return pl.pallas_call(
        functools.partial(_hgrn_rec_kernel, chunk=HGRN_CHUNK),
        grid=(batch, heads),
        in_specs=[blk, blk, blk, blk, blk, pl.BlockSpec((1, HGRN_DV), lambda b, h: (0, h))],
        out_specs=blk,
        out_shape=jax.ShapeDtypeStruct((t, d), BF16),
        compiler_params=_cparams(2),
        name="hgrn_recurrence",
    )(q2, k2, lf2, v2, og2, out_gain2)


def kernel(x, c, ada_w, ada_b, norm_g, ffn_w_in, ffn_w_down, attn_w_qkv, attn_w_o, attn_q_gain,
           attn_k_gain, attn_lambda, attn_subln_gain, rel_bias, hgrn_w_in, hgrn_w_o,
           hgrn_out_gain, hgrn_lb_logits):
    batch, seq, d = x.shape
    depth = ada_w.shape[0]
    mod = _ada_call(c, ada_w, ada_b).reshape(depth, batch, N_SUBLAYERS, 3, d)
    bias_tiles = _bias_call(rel_bias, ATTN_TILE)
    x2 = x.reshape(batch * seq, d)
    for layer in range(depth):
        gains = norm_g[layer].reshape(N_SUBLAYERS, 1, d)
        x2 = _ffn_call(x2, mod[layer, :, 0], gains[0], ffn_w_in[layer, 0].astype(BF16),
                       ffn_w_down[layer, 0].astype(BF16), seq)
        j = layer // N_MIXERS
        if layer % N_MIXERS == 0:
            reps = QKV_COLS // ATTN_HEAD_DIM
            q2, k2, v2 = _qkv_call(
                x2, mod[layer, :, 1], gains[1], attn_w_qkv[j].astype(BF16),
                jnp.tile(attn_q_gain[j], reps).reshape(1, QKV_COLS),
                jnp.tile(attn_k_gain[j], reps).reshape(1, QKV_COLS), seq)
            lambda_init = 0.8 - 0.6 * math.exp(-0.3 * layer)
            o2 = _attn_call(q2, k2, v2, bias_tiles, attn_lambda[j],
                            attn_subln_gain[j].reshape(1, ATTN_V_DIM), batch, seq, lambda_init)
            x2 = _out_call(x2, o2, mod[layer, :, 1], attn_w_o[j].astype(BF16), seq)
        else:
            q2, k2, lf2, v2, og2 = _hgrn_in_call(
                x2, mod[layer, :, 1], gains[1], hgrn_w_in[j].astype(BF16), hgrn_lb_logits, j, seq)
            o2 = _hgrn_rec_call(q2, k2, lf2, v2, og2, hgrn_out_gain[j].reshape(1, d), batch, seq)
            x2 = _out_call(x2, o2, mod[layer, :, 1], hgrn_w_o[j].astype(BF16), seq)
        x2 = _ffn_call(x2, mod[layer, :, 2], gains[2], ffn_w_in[layer, 1].astype(BF16),
                       ffn_w_down[layer, 1].astype(BF16), seq)
    return x2.reshape(batch, seq, d)
```

```python
import functools
import math

import numpy as np
import jax
import jax.numpy as jnp
from jax import lax
from jax.experimental import pallas as pl
from jax.experimental.pallas import tpu as pltpu

F32 = jnp.float32
BF16 = jnp.bfloat16

EPS = 1e-6
N_SUBLAYERS = 3
N_MIXERS = 2
ATTN_HEADS = 8
ATTN_HEAD_DIM = 64
ATTN_V_DIM = 128
NUM_BUCKETS = 32
MAX_DISTANCE = 128
HGRN_HEADS = 8
HGRN_DK = 128
HGRN_DV = 128

LANES = 128
VMEM_LIMIT = 56 * 1024 * 1024
NEG = -1e30
LOG2E = math.log2(math.e)

ATTN_TILE = 256
QKV_COLS = 256
HGRN_CHUNK = 128
HGRN_INTERLEAVE = 8


def _cparams(n_axes):
    return pltpu.CompilerParams(
        dimension_semantics=("arbitrary",) * n_axes, vmem_limit_bytes=VMEM_LIMIT)


def _resident(shape):
    return pl.BlockSpec(shape, lambda *_: (0,) * len(shape), pipeline_mode=pl.Buffered(1))


def _sigmoid(z):
    return 1.0 / (1.0 + jnp.exp(-z))


def _dot(a, b):
    return jnp.dot(a, b, preferred_element_type=F32)


def _dot_nt(a, b):
    return lax.dot_general(a, b, (((1,), (1,)), ((), ())), preferred_element_type=F32)


def _dot_tn(a, b):
    return lax.dot_general(a, b, (((0,), (0,)), ((), ())), preferred_element_type=F32)


def _norm_mod(x, gain, mod_ref):
    ms = jnp.mean(x * x, axis=-1, keepdims=True)
    h = x * lax.rsqrt(ms + EPS) * gain
    return h * (1.0 + mod_ref[0, 1:2, :]) + mod_ref[0, 0:1, :]


def _ada_kernel(c_ref, w_ref, b_ref, o_ref):
    c = c_ref[...]
    ca = (c * _sigmoid(c)).astype(BF16)
    o_ref[0] = _dot(ca, w_ref[0].astype(BF16)) + b_ref[0]


def _ada_call(c, ada_w, ada_b):
    depth, d, n = ada_w.shape
    b = c.shape[0]
    tn = 1152
    assert n % tn == 0
    return pl.pallas_call(
        _ada_kernel,
        grid=(depth, n // tn),
        in_specs=[
            pl.BlockSpec((b, d), lambda l, j: (0, 0)),
            pl.BlockSpec((1, d, tn), lambda l, j: (l, 0, j)),
            pl.BlockSpec((1, 1, tn), lambda l, j: (l, 0, j)),
        ],
        out_specs=pl.BlockSpec((1, b, tn), lambda l, j: (l, 0, j)),
        out_shape=jax.ShapeDtypeStruct((depth, b, n), F32),
        compiler_params=_cparams(2),
        name="ada_mod",
    )(c, ada_w, ada_b.reshape(depth, 1, n))


def _ffn_kernel(x_ref, mod_ref, g_ref, win_ref, wdn_ref, o_ref, act_ref, *, d_ff, tf):
    x = x_ref[...]
    h = _norm_mod(x, g_ref[...], mod_ref).astype(BF16)
    for j in range(d_ff // tf):
        a = _dot(h, win_ref[:, j * tf:(j + 1) * tf])
        b = _dot(h, win_ref[:, d_ff + j * tf:d_ff + (j + 1) * tf])
        act_ref[:, j * tf:(j + 1) * tf] = (a * _sigmoid(a) * b).astype(BF16)
    y = _dot(act_ref[...], wdn_ref[...])
    o_ref[...] = x + 0.5 * mod_ref[0, 2:3, :] * y


def _ffn_call(x2, mod3, gain, w_in, w_down, seq):
    t, d = x2.shape
    d_ff = w_down.shape[0]
    tm = min(512, seq)
    tf = 1408
    assert t % tm == 0 and seq % tm == 0 and d_ff % tf == 0
    per_seq = seq // tm
    return pl.pallas_call(
        functools.partial(_ffn_kernel, d_ff=d_ff, tf=tf),
        grid=(t // tm,),
        in_specs=[
            pl.BlockSpec((tm, d), lambda i: (i, 0)),
            pl.BlockSpec((1, 3, d), lambda i: (i // per_seq, 0, 0)),
            _resident((1, d)),
            _resident((d, 2 * d_ff)),
            _resident((d_ff, d)),
        ],
        out_specs=pl.BlockSpec((tm, d), lambda i: (i, 0)),
        out_shape=jax.ShapeDtypeStruct((t, d), F32),
        scratch_shapes=[pltpu.VMEM((tm, d_ff), BF16)],
        compiler_params=_cparams(1),
        name="ffn_half_step",
    )(x2, mod3, gain, w_in, w_down)


def _out_kernel(x_ref, o_ref, mod_ref, w_ref, y_ref):
    y = _dot(o_ref[...], w_ref[...])
    y_ref[...] = x_ref[...] + mod_ref[0, 2:3, :] * y


def _out_call(x2, o2, mod3, w_o, seq):
    t, d = x2.shape
    tm = min(1024, seq)
    assert t % tm == 0 and seq % tm == 0
    per_seq = seq // tm
    return pl.pallas_call(
        _out_kernel,
        grid=(t // tm,),
        in_specs=[
            pl.BlockSpec((tm, d), lambda i: (i, 0)),
            pl.BlockSpec((tm, d), lambda i: (i, 0)),
            pl.BlockSpec((1, 3, d), lambda i: (i // per_seq, 0, 0)),
            _resident((d, d)),
        ],
        out_specs=pl.BlockSpec((tm, d), lambda i: (i, 0)),
        out_shape=jax.ShapeDtypeStruct((t, d), F32),
        compiler_params=_cparams(1),
        name="mixer_out_proj",
    )(x2, o2, mod3, w_o)


def _group_mean_sq(z, ones_blk):
    return _dot((z * z).astype(BF16), ones_blk)


def _qkv_kernel(x_ref, mod_ref, g_ref, w_ref, qg_ref, kg_ref, q_ref, k_ref, v_ref, *, d):
    h = _norm_mod(x_ref[...], g_ref[...], mod_ref).astype(BF16)
    tn = QKV_COLS
    r = lax.broadcasted_iota(jnp.int32, (tn, tn), 0) // ATTN_HEAD_DIM
    c = lax.broadcasted_iota(jnp.int32, (tn, tn), 1) // ATTN_HEAD_DIM
    ones_blk = jnp.where(r == c, 1.0 / ATTN_HEAD_DIM, 0.0).astype(BF16)
    q_scale = ATTN_HEAD_DIM ** -0.5 * LOG2E
    for j in range(d // tn):
        sl = slice(j * tn, (j + 1) * tn)
        q = _dot(h, w_ref[:, j * tn:(j + 1) * tn])
        q = q * lax.rsqrt(_group_mean_sq(q, ones_blk) + EPS) * (qg_ref[...] * q_scale)
        q_ref[:, sl] = q.astype(BF16)
        k = _dot(h, w_ref[:, d + j * tn:d + (j + 1) * tn])
        k = k * lax.rsqrt(_group_mean_sq(k, ones_blk) + EPS) * kg_ref[...]
        k_ref[:, sl] = k.astype(BF16)
    v_ref[...] = _dot(h, w_ref[:, 2 * d:3 * d]).astype(BF16)


def _qkv_call(x2, mod3, gain, w_qkv, q_gain2, k_gain2, seq):
    t, d = x2.shape
    tm = min(512, seq)
    assert t % tm == 0 and seq % tm == 0
    per_seq = seq // tm
    out = jax.ShapeDtypeStruct((t, d), BF16)
    row = pl.BlockSpec((tm, d), lambda i: (i, 0))
    return pl.pallas_call(
        functools.partial(_qkv_kernel, d=d),
        grid=(t // tm,),
        in_specs=[
            row,
            pl.BlockSpec((1, 3, d), lambda i: (i // per_seq, 0, 0)),
            _resident((1, d)),
            _resident((d, 3 * d)),
            _resident((1, QKV_COLS)),
            _resident((1, QKV_COLS)),
        ],
        out_specs=[row, row, row],
        out_shape=[out, out, out],
        compiler_params=_cparams(1),
        name="attn_qkv_proj",
    )(x2, mod3, gain, w_qkv, q_gain2, k_gain2)


def _bucket_thresholds():
    max_exact = NUM_BUCKETS // 2
    dist = np.arange(0, 4 * MAX_DISTANCE)
    d_f = np.maximum(dist, 1).astype(np.float32)
    large = max_exact + (np.log(d_f / np.float32(max_exact)) / np.float32(math.log(MAX_DISTANCE / max_exact))
                         * np.float32(NUM_BUCKETS - max_exact)).astype(np.int32)
    bucket = np.where(dist < max_exact, dist, np.minimum(large, NUM_BUCKETS - 1))
    assert np.all(np.diff(bucket) >= 0) and bucket[-1] == NUM_BUCKETS - 1
    return [int(np.argmax(bucket >= b)) for b in range(NUM_BUCKETS)]


def _bias_kernel(rb_ref, o_ref, *, thresholds, tile):
    h = pl.program_id(0)
    i = lax.broadcasted_iota(jnp.int32, (tile, tile), 0)
    j = lax.broadcasted_iota(jnp.int32, (tile, tile), 1)
    for off in range(3):
        dist = off * tile + i - j
        val = jnp.full((tile, tile), rb_ref[0, h], F32)
        for b in range(1, NUM_BUCKETS):
            val = jnp.where(dist >= thresholds[b], rb_ref[b, h], val)
        o_ref[0, off] = jnp.where(dist >= 0, val * LOG2E, NEG)


def _bias_call(rel_bias, tile):
    thresholds = _bucket_thresholds()
    assert thresholds[NUM_BUCKETS - 1] <= tile + 1
    heads = rel_bias.shape[1]
    return pl.pallas_call(
        functools.partial(_bias_kernel, thresholds=thresholds, tile=tile),
        grid=(heads,),
        in_specs=[pl.BlockSpec(memory_space=pltpu.SMEM)],
        out_specs=pl.BlockSpec((1, 3, tile, tile), lambda h: (h, 0, 0, 0)),
        out_shape=jax.ShapeDtypeStruct((heads, 3, tile, tile), F32),
        compiler_params=_cparams(1),
        name="attn_rel_bias_tiles",
    )(rel_bias)


def _attn_kernel(q_ref, k_ref, v_ref, bias_ref, lam_ref, sg_ref, o_ref, v1_ref,
                 s0_ref, s1_ref, p0_ref, p1_ref, m0_ref, m1_ref, *, tile, lambda_init):
    seq = q_ref.shape[0]
    dv = ATTN_V_DIM
    n_tiles = seq // tile
    s_refs, p_refs, m_refs = (s0_ref, s1_ref), (p0_ref, p1_ref), (m0_ref, m1_ref)
    v1_ref[:, :dv] = v_ref[...]
    v1_ref[:, dv:] = jnp.ones((seq, dv), BF16)
    lam = lam_ref[...]
    lam_full = (jnp.exp(jnp.sum(lam[0:1] * lam[1:2], axis=-1, keepdims=True))
                - jnp.exp(jnp.sum(lam[2:3] * lam[3:4], axis=-1, keepdims=True)) + lambda_init)
    lane = lax.broadcasted_iota(jnp.int32, (tile, dv), 1)

    def stacked_q(i):
        q = q_ref[i * tile:(i + 1) * tile, :]
        zero = jnp.zeros_like(q)
        return jnp.concatenate([jnp.where(lane < ATTN_HEAD_DIM, q, zero),
                                jnp.where(lane >= ATTN_HEAD_DIM, q, zero)], axis=0)

    def scores(i, j, qs):
        bt = bias_ref[0, min(i - j, 2)]
        s_refs[i % 2][:, j * tile:(j + 1) * tile] = (
            _dot_nt(qs, k_ref[j * tile:(j + 1) * tile, :]) + jnp.concatenate([bt, bt], axis=0))

    def row_max(i):
        m = jnp.max(s_refs[i % 2][:, :(i + 1) * tile], axis=-1, keepdims=True)
        m_refs[i % 2][...] = jnp.broadcast_to(m, m_refs[i % 2].shape)

    def probs(i, j):
        m = m_refs[i % 2][...]
        for c in range(j * tile, (j + 1) * tile, LANES):
            cols = slice(c, c + LANES)
            p_refs[i % 2][:, cols] = jnp.exp2(s_refs[i % 2][:, cols] - m).astype(BF16)

    def finish(i):
        n_keys = (i + 1) * tile
        acc = _dot(p_refs[i % 2][:, :n_keys], v1_ref[:n_keys, :])
        o_all = acc[:, :dv] / acc[:, dv:]
        o = o_all[:tile] - lam_full * o_all[tile:]
        ms = jnp.mean(o * o, axis=-1, keepdims=True)
        o = o * lax.rsqrt(ms + EPS) * sg_ref[...] * (1.0 - lambda_init)
        o_ref[i * tile:(i + 1) * tile, :] = o.astype(BF16)

    def score_and_prob_steps(i_scores, i_probs):
        qs = stacked_q(i_scores) if i_scores >= 0 else None
        for j in range(max(i_scores, i_probs) + 1):
            if j <= i_scores:
                scores(i_scores, j, qs)
            if j <= i_probs:
                probs(i_probs, j)
        if i_scores >= 0:
            row_max(i_scores)

    last = n_tiles - 1
    score_and_prob_steps(last, -1)
    score_and_prob_steps(last - 1, last)
    for i in range(last, -1, -1):
        finish(i)
        score_and_prob_steps(i - 2, i - 1)


def _attn_call(q2, k2, v2, bias_tiles, lam, subln_gain2, batch, seq, lambda_init):
    t, d = q2.shape
    tile = ATTN_TILE
    heads = d // ATTN_V_DIM
    assert seq % tile == 0
    blk = pl.BlockSpec((seq, ATTN_V_DIM), lambda b, h: (b, h))
    return pl.pallas_call(
        functools.partial(_attn_kernel, tile=tile, lambda_init=lambda_init),
        grid=(batch, heads),
        in_specs=[
            blk, blk, blk,
            pl.BlockSpec((1, 3, tile, tile), lambda b, h: (h, 0, 0, 0)),
            pl.BlockSpec(lam.shape, lambda b, h: (0, 0)),
            pl.BlockSpec((1, ATTN_V_DIM), lambda b, h: (0, 0)),
        ],
        out_specs=blk,
        out_shape=jax.ShapeDtypeStruct((t, d), BF16),
        scratch_shapes=[
            pltpu.VMEM((seq, 2 * ATTN_V_DIM), BF16),
            pltpu.VMEM((2 * tile, seq), F32),
            pltpu.VMEM((2 * tile, seq), F32),
            pltpu.VMEM((2 * tile, seq), BF16),
            pltpu.VMEM((2 * tile, seq), BF16),
            pltpu.VMEM((2 * tile, LANES), F32),
            pltpu.VMEM((2 * tile, LANES), F32),
        ],
        compiler_params=_cparams(2),
        name="diff_attention",
    )(q2, k2, v2, bias_tiles, lam, subln_gain2)


def _hgrn_in_kernel(x_ref, mod_ref, g_ref, w_ref, lbl_ref, q_ref, k_ref, lf_ref, v_ref, og_ref,
                    *, d, layer_j):
    h = _norm_mod(x_ref[...], g_ref[...], mod_ref).astype(BF16)
    logits = lbl_ref[...]
    e = jnp.exp(logits - jnp.max(logits, axis=0, keepdims=True))
    p = e / jnp.sum(e, axis=0, keepdims=True)
    lb = jnp.zeros((1, d), F32)
    for i in range(1, layer_j + 1):
        lb = lb + p[i:i + 1, :]
    log_lb = jnp.log(lb)
    log_1m_lb = jnp.log1p(-lb)
    tn = 4 * LANES
    for j in range(d // tn):
        sl = slice(j * tn, (j + 1) * tn)
        q_ref[:, sl] = _dot(h, w_ref[:, j * tn:(j + 1) * tn])
        f = _dot(h, w_ref[:, d + j * tn:d + (j + 1) * tn])
        ef = jnp.exp(-jnp.abs(f))
        log_sig = jnp.minimum(f, 0.0) - jnp.log1p(ef)
        a = log_lb[:, sl]
        b = log_1m_lb[:, sl] + log_sig
        lf_ref[:, sl] = jnp.maximum(a, b) + jnp.log1p(jnp.exp(-jnp.abs(a - b)))
        k_ref[:, sl] = (1.0 - lb[:, sl]) * (jnp.where(f >= 0.0, ef, 1.0) / (1.0 + ef))
        v_ref[:, sl] = _dot(h, w_ref[:, 2 * d + j * tn:2 * d + (j + 1) * tn]).astype(BF16)
        g = _dot(h, w_ref[:, 3 * d + j * tn:3 * d + (j + 1) * tn])
        og_ref[:, sl] = g * _sigmoid(g)


def _hgrn_in_call(x2, mod3, gain, w_in, lb_logits, layer_j, seq):
    t, d = x2.shape
    tm = min(512, seq)
    assert t % tm == 0 and seq % tm == 0
    per_seq = seq // tm
    row = pl.BlockSpec((tm, d), lambda i: (i, 0))
    f32_out = jax.ShapeDtypeStruct((t, d), F32)
    return pl.pallas_call(
        functools.partial(_hgrn_in_kernel, d=d, layer_j=layer_j),
        grid=(t // tm,),
        in_specs=[
            row,
            pl.BlockSpec((1, 3, d), lambda i: (i // per_seq, 0, 0)),
            _resident((1, d)),
            _resident((d, 4 * d)),
            _resident(lb_logits.shape),
        ],
        out_specs=[row, row, row, row, row],
        out_shape=[f32_out, f32_out, f32_out, jax.ShapeDtypeStruct((t, d), BF16), f32_out],
        compiler_params=_cparams(1),
        name="hgrn_in_proj",
    )(x2, mod3, gain, w_in, lb_logits)


def _rows_at(g_cum, half):
    c = g_cum.shape[0]
    if half >= 4:
        blk = 2 * half
        return jnp.concatenate(
            [jnp.broadcast_to(g_cum[p * blk + half - 1:p * blk + half, :], (blk, g_cum.shape[1]))
             for p in range(c // blk)], axis=0)
    row = lax.broadcasted_iota(jnp.int32, g_cum.shape, 0)
    if half == 2:
        r4 = row % 4
        up1 = pltpu.roll(g_cum, c - 1, 0)
        dn1 = pltpu.roll(g_cum, 1, 0)
        dn2 = pltpu.roll(g_cum, 2, 0)
        return jnp.where(r4 == 0, up1, jnp.where(r4 == 1, g_cum, jnp.where(r4 == 2, dn1, dn2)))
    assert half == 1
    return jnp.where(row % 2 == 0, g_cum, pltpu.roll(g_cum, 1, 0))


def _round_robin(generators, width):
    pending = list(generators)
    active = []
    while pending or active:
        while pending and len(active) < width:
            active.append(pending.pop(0))
        for gen in list(active):
            try:
                next(gen)
            except StopIteration:
                active.remove(gen)


def _hgrn_rec_kernel(q_ref, k_ref, lf_ref, v_ref, og_ref, gain_ref, o_ref,
                     qe_ref, a_ref, u_ref, dl_ref, st_ref, *, chunk):
    seq = q_ref.shape[0]
    c = chunk
    n_chunks = seq // c
    ri = lax.broadcasted_iota(jnp.int32, (c, c), 0)
    ci = lax.broadcasted_iota(jnp.int32, (c, c), 1)
    tri = jnp.where(ri >= ci, 1.0, 0.0).astype(BF16)
    pair_key = jnp.where(ri >= ci, ri ^ ci, -1)
    row = lax.broadcasted_iota(jnp.int32, (c, HGRN_DK), 0)

    def neg_abs(z):
        return lax.bitcast_convert_type(
            lax.bitcast_convert_type(z, jnp.int32) | jnp.int32(-2 ** 31), F32)

    def intra(n):
        sl = slice(n * c, (n + 1) * c)
        g = lf_ref[sl, :]
        q = q_ref[sl, :]
        k = k_ref[sl, :]
        g_hi = g.astype(BF16)
        r1 = g - g_hi.astype(F32)
        g_mid = r1.astype(BF16)
        g_lo = (r1 - g_mid.astype(F32)).astype(BF16)
        g2 = (_dot(tri, g_hi) + (_dot(tri, g_mid) + _dot(tri, g_lo))) * LOG2E
        g2_last = g2[c - 1:c, :]
        yield
        qe_ref[sl, :] = (q * jnp.exp2(g2)).astype(BF16)
        k_dec = (k * jnp.exp2(g2_last - g2)).astype(BF16)
        u_ref[n] = _dot_tn(v_ref[sl, :], k_dec)
        dl_ref[n] = jnp.exp2(g2_last)
        scores = jnp.where(pair_key == 0, _dot_nt(q.astype(BF16), k.astype(BF16)), 0.0)
        yield
        half = 1
        while half < c:
            e = jnp.exp2(neg_abs(g2 - _rows_at(g2, half)))
            if half >= 8:
                qk = jnp.concatenate(
                    [(q if (b % 2) else k)[b * half:(b + 1) * half] for b in range(c // half)], axis=0)
            else:
                qk = jnp.where((row & (2 * half - 1)) >= half, q, k)
            zf = qk * e
            z = zf.astype(BF16)
            if half >= 8:
                z_right = jnp.concatenate(
                    [zf[b * half:(b + 1) * half] for b in range(1, c // half, 2)], axis=0).astype(BF16)
                level = _dot_nt(z_right, z)
                pieces = []
                for b in range(c // half):
                    rows = slice(b * half, (b + 1) * half)
                    piece = scores[rows]
                    if b % 2:
                        lv = level[(b // 2) * half:(b // 2 + 1) * half]
                        piece = jnp.where(pair_key[rows] >= half, lv, piece)
                    pieces.append(piece)
                scores = jnp.concatenate(pieces, axis=0)
            else:
                scores = jnp.where(pair_key >= half, _dot_nt(z, z), scores)
            half *= 2
            yield
        a_ref[n] = scores.astype(BF16)

    def outputs(n):
        sl = slice(n * c, (n + 1) * c)
        o = _dot_nt(qe_ref[sl, :], st_ref[n]) + _dot(a_ref[n], v_ref[sl, :])
        yield
        ms = jnp.mean(o * o, axis=-1, keepdims=True)
        o_ref[sl, :] = (o * lax.rsqrt(ms + EPS) * gain_ref[...] * og_ref[sl, :]).astype(BF16)

    _round_robin([intra(n) for n in range(n_chunks)], HGRN_INTERLEAVE)

    state = jnp.zeros((HGRN_DV, HGRN_DK), F32)
    for n in range(n_chunks):
        st_ref[n] = state.astype(BF16)
        state = dl_ref[n] * state + u_ref[n]

    _round_robin([outputs(n) for n in range(n_chunks)], HGRN_INTERLEAVE)


def _hgrn_rec_call(q2, k2, lf2, v2, og2, out_gain2, batch, seq):
    t, d = q2.shape
    heads = d // HGRN_DV
    c = HGRN_CHUNK
    assert seq % c == 0
    n_chunks = seq // c
    blk = pl.BlockSpec((seq, HGRN_DV), lambda b, h: (b, h))
    return pl.pallas_call(
        functools.partial(_hgrn_rec_kernel, chunk=c),
        grid=(batch, heads),
        in_specs=[blk, blk, blk, blk, blk, pl.BlockSpec((1, HGRN_DV), lambda b, h: (0, h))],
        out_specs=blk,
        out_shape=jax.ShapeDtypeStruct((t, d), BF16),
        scratch_shapes=[
            pltpu.VMEM((seq, HGRN_DK), BF16),
            pltpu.VMEM((n_chunks, c, c), BF16),
            pltpu.VMEM((n_chunks, HGRN_DV, HGRN_DK), F32),
            pltpu.VMEM((n_chunks, 1, HGRN_DK), F32),
            pltpu.VMEM((n_chunks, HGRN_DV, HGRN_DK), BF16),
        ],
        compiler_params=_cparams(2),
        name="hgrn_recurrence",
    )(q2, k2, lf2, v2, og2, out_gain2)


def kernel(x, c, ada_w, ada_b, norm_g, ffn_w_in, ffn_w_down, attn_w_qkv, attn_w_o, attn_q_gain,
           attn_k_gain, attn_lambda, attn_subln_gain, rel_bias, hgrn_w_in, hgrn_w_o,
           hgrn_out_gain, hgrn_lb_logits):
    batch, seq, d = x.shape
    depth = ada_w.shape[0]
    mod = _ada_call(c, ada_w, ada_b).reshape(depth, batch, N_SUBLAYERS, 3, d)
    bias_tiles = _bias_call(rel_bias, ATTN_TILE)
    x2 = x.reshape(batch * seq, d)
    for layer in range(depth):
        gains = norm_g[layer].reshape(N_SUBLAYERS, 1, d)
        x2 = _ffn_call(x2, mod[layer, :, 0], gains[0], ffn_w_in[layer, 0].astype(BF16),
                       ffn_w_down[layer, 0].astype(BF16), seq)
        j = layer // N_MIXERS
        if layer % N_MIXERS == 0:
            reps = QKV_COLS // ATTN_HEAD_DIM
            q2, k2, v2 = _qkv_call(
                x2, mod[layer, :, 1], gains[1], attn_w_qkv[j].astype(BF16),
                jnp.tile(attn_q_gain[j], reps).reshape(1, QKV_COLS),
                jnp.tile(attn_k_gain[j], reps).reshape(1, QKV_COLS), seq)
            lambda_init = 0.8 - 0.6 * math.exp(-0.3 * layer)
            o2 = _attn_call(q2, k2, v2, bias_tiles, attn_lambda[j],
                            attn_subln_gain[j].reshape(1, ATTN_V_DIM), batch, seq, lambda_init)
            x2 = _out_call(x2, o2, mod[layer, :, 1], attn_w_o[j].astype(BF16), seq)
        else:
            q2, k2, lf2, v2, og2 = _hgrn_in_call(
                x2, mod[layer, :, 1], gains[1], hgrn_w_in[j].astype(BF16), hgrn_lb_logits, j, seq)
            o2 = _hgrn_rec_call(q2, k2, lf2, v2, og2, hgrn_out_gain[j].reshape(1, d), batch, seq)
            x2 = _out_call(x2, o2, mod[layer, :, 1], hgrn_w_o[j].astype(BF16), seq)
        x2 = _ffn_call(x2, mod[layer, :, 2], gains[2], ffn_w_in[layer, 1].astype(BF16),
                       ffn_w_down[layer, 1].astype(BF16), seq)
    return x2.reshape(batch, seq, d)
```

```python
import functools
import math

import numpy as np
import jax
import jax.numpy as jnp
from jax import lax
from jax.experimental import pallas as pl
from jax.experimental.pallas import tpu as pltpu

F32 = jnp.float32
BF16 = jnp.bfloat16

EPS = 1e-6
N_SUBLAYERS = 3
N_MIXERS = 2
ATTN_HEADS = 8
ATTN_HEAD_DIM = 64
ATTN_V_DIM = 128
NUM_BUCKETS = 32
MAX_DISTANCE = 128
HGRN_HEADS = 8
HGRN_DK = 128
HGRN_DV = 128

LANES = 128
VMEM_LIMIT = 56 * 1024 * 1024
NEG = -1e30
LOG2E = math.log2(math.e)

MXU_COLS = 256
ATTN_TILE = 256
QKV_COLS = MXU_COLS
FFN_COLS = 2 * MXU_COLS
HGRN_COLS = MXU_COLS
HGRN_CHUNK = 128
HGRN_INTERLEAVE = 8


def _cparams(n_axes):
    return pltpu.CompilerParams(
        dimension_semantics=("arbitrary",) * n_axes, vmem_limit_bytes=VMEM_LIMIT)


def _resident(shape):
    return pl.BlockSpec(shape, lambda *_: (0,) * len(shape), pipeline_mode=pl.Buffered(1))


def _sigmoid(z):
    return 1.0 / (1.0 + jnp.exp(-z))


def _dot(a, b):
    return jnp.dot(a, b, preferred_element_type=F32)


def _dot_nt(a, b):
    return lax.dot_general(a, b, (((1,), (1,)), ((), ())), preferred_element_type=F32)


def _dot_tn(a, b):
    return lax.dot_general(a, b, (((0,), (0,)), ((), ())), preferred_element_type=F32)


def _pipelined(stages):
    pending = None
    for produce, consume in stages:
        value = produce()
        if pending is not None:
            pending[1](pending[0])
        pending = (value, consume)
    if pending is not None:
        pending[1](pending[0])


def _col_pieces(total, width):
    return [(lo, min(lo + width, total)) for lo in range(0, total, width)]


def _norm_mod(x, gain, mod_ref):
    ms = jnp.mean(x * x, axis=-1, keepdims=True)
    h = x * lax.rsqrt(ms + EPS) * gain
    return h * (1.0 + mod_ref[0, 1:2, :]) + mod_ref[0, 0:1, :]


def _ada_kernel(c_ref, w_ref, b_ref, o_ref):
    c = c_ref[...]
    ca = (c * _sigmoid(c)).astype(BF16)
    o_ref[0] = _dot(ca, w_ref[0].astype(BF16)) + b_ref[0]


def _ada_call(c, ada_w, ada_b):
    depth, d, n = ada_w.shape
    b = c.shape[0]
    tn = 1152
    assert n % tn == 0
    return pl.pallas_call(
        _ada_kernel,
        grid=(depth, n // tn),
        in_specs=[
            pl.BlockSpec((b, d), lambda l, j: (0, 0)),
            pl.BlockSpec((1, d, tn), lambda l, j: (l, 0, j)),
            pl.BlockSpec((1, 1, tn), lambda l, j: (l, 0, j)),
        ],
        out_specs=pl.BlockSpec((1, b, tn), lambda l, j: (l, 0, j)),
        out_shape=jax.ShapeDtypeStruct((depth, b, n), F32),
        compiler_params=_cparams(2),
        name="ada_mod",
    )(c, ada_w, ada_b.reshape(depth, 1, n))


def _ffn_kernel(*refs, d_ff, with_mixer):
    if with_mixer:
        (x_ref, mix_ref, mixmod_ref, wo_ref, mod_ref, g_ref, win_ref, wdn_ref, o_ref, act_ref) = refs
        x = x_ref[...] + mixmod_ref[0, 2:3, :] * _dot(mix_ref[...], wo_ref[...])
    else:
        (x_ref, mod_ref, g_ref, win_ref, wdn_ref, o_ref, act_ref) = refs
        x = x_ref[...]
    h = _norm_mod(x, g_ref[...], mod_ref).astype(BF16)

    def stage(lo, hi):
        def produce():
            return _dot(h, win_ref[:, lo:hi]), _dot(h, win_ref[:, d_ff + lo:d_ff + hi])

        def consume(ab):
            a, b = ab
            act_ref[:, lo:hi] = (a * _sigmoid(a) * b).astype(BF16)

        return produce, consume

    _pipelined([stage(lo, hi) for lo, hi in _col_pieces(d_ff, FFN_COLS)])
    y = _dot(act_ref[...], wdn_ref[...])
    o_ref[...] = x + 0.5 * mod_ref[0, 2:3, :] * y


def _ffn_call(x2, mod3, gain, w_in, w_down, seq, mixer=None):
    t, d = x2.shape
    d_ff = w_down.shape[0]
    tm = min(512, seq)
    assert t % tm == 0 and seq % tm == 0 and d_ff % MXU_COLS == 0
    per_seq = seq // tm
    row = pl.BlockSpec((tm, d), lambda i: (i, 0))
    mod_spec = pl.BlockSpec((1, 3, d), lambda i: (i // per_seq, 0, 0))
    operands, in_specs = [x2], [row]
    if mixer is not None:
        operands += list(mixer)
        in_specs += [row, mod_spec, _resident((d, d))]
    operands += [mod3, gain, w_in, w_down]
    in_specs += [mod_spec, _resident((1, d)), _resident((d, 2 * d_ff)), _resident((d_ff, d))]
    return pl.pallas_call(
        functools.partial(_ffn_kernel, d_ff=d_ff, with_mixer=mixer is not None),
        grid=(t // tm,),
        in_specs=in_specs,
        out_specs=row,
        out_shape=jax.ShapeDtypeStruct((t, d), F32),
        scratch_shapes=[pltpu.VMEM((tm, d_ff), BF16)],
        compiler_params=_cparams(1),
        name="ffn_half_step",
    )(*operands)


def _group_mean_sq(z, ones_blk):
    return _dot((z * z).astype(BF16), ones_blk)


def _qkv_kernel(x_ref, mod_ref, g_ref, w_ref, qg_ref, kg_ref, q_ref, k_ref, v_ref, *, d):
    h = _norm_mod(x_ref[...], g_ref[...], mod_ref).astype(BF16)
    tn = QKV_COLS
    r = lax.broadcasted_iota(jnp.int32, (tn, tn), 0) // ATTN_HEAD_DIM
    c = lax.broadcasted_iota(jnp.int32, (tn, tn), 1) // ATTN_HEAD_DIM
    ones_blk = jnp.where(r == c, 1.0 / ATTN_HEAD_DIM, 0.0).astype(BF16)
    q_gain = qg_ref[...] * (ATTN_HEAD_DIM ** -0.5 * LOG2E)
    k_gain = kg_ref[...]

    def normed(w_lo, out_ref, lo, hi, gain):
        def consume(z):
            out_ref[:, lo:hi] = (z * lax.rsqrt(_group_mean_sq(z, ones_blk) + EPS) * gain).astype(BF16)

        return (lambda: _dot(h, w_ref[:, w_lo + lo:w_lo + hi])), consume

    def plain(w_lo, out_ref, lo, hi):
        def consume(z):
            out_ref[:, lo:hi] = z.astype(BF16)

        return (lambda: _dot(h, w_ref[:, w_lo + lo:w_lo + hi])), consume

    stages = []
    for lo, hi in _col_pieces(d, tn):
        stages += [normed(0, q_ref, lo, hi, q_gain), normed(d, k_ref, lo, hi, k_gain),
                   plain(2 * d, v_ref, lo, hi)]
    _pipelined(stages)


def _qkv_call(x2, mod3, gain, w_qkv, q_gain2, k_gain2, seq):
    t, d = x2.shape
    tm = min(512, seq)
    assert t % tm == 0 and seq % tm == 0
    per_seq = seq // tm
    out = jax.ShapeDtypeStruct((t, d), BF16)
    row = pl.BlockSpec((tm, d), lambda i: (i, 0))
    return pl.pallas_call(
        functools.partial(_qkv_kernel, d=d),
        grid=(t // tm,),
        in_specs=[
            row,
            pl.BlockSpec((1, 3, d), lambda i: (i // per_seq, 0, 0)),
            _resident((1, d)),
            _resident((d, 3 * d)),
            _resident((1, QKV_COLS)),
            _resident((1, QKV_COLS)),
        ],
        out_specs=[row, row, row],
        out_shape=[out, out, out],
        compiler_params=_cparams(1),
        name="attn_qkv_proj",
    )(x2, mod3, gain, w_qkv, q_gain2, k_gain2)


def _bucket_thresholds():
    max_exact = NUM_BUCKETS // 2
    dist = np.arange(0, 4 * MAX_DISTANCE)
    d_f = np.maximum(dist, 1).astype(np.float32)
    large = max_exact + (np.log(d_f / np.float32(max_exact)) / np.float32(math.log(MAX_DISTANCE / max_exact))
                         * np.float32(NUM_BUCKETS - max_exact)).astype(np.int32)
    bucket = np.where(dist < max_exact, dist, np.minimum(large, NUM_BUCKETS - 1))
    assert np.all(np.diff(bucket) >= 0) and bucket[-1] == NUM_BUCKETS - 1
    return [int(np.argmax(bucket >= b)) for b in range(NUM_BUCKETS)]


def _bias_kernel(rb_ref, o_ref, *, thresholds, tile):
    h = pl.program_id(0)
    i = lax.broadcasted_iota(jnp.int32, (tile, tile), 0)
    j = lax.broadcasted_iota(jnp.int32, (tile, tile), 1)
    for off in range(3):
        dist = off * tile + i - j
        val = jnp.full((tile, tile), rb_ref[0, h], F32)
        for b in range(1, NUM_BUCKETS):
            val = jnp.where(dist >= thresholds[b], rb_ref[b, h], val)
        o_ref[0, off] = jnp.where(dist >= 0, val * LOG2E, NEG)


def _bias_call(rel_bias, tile):
    thresholds = _bucket_thresholds()
    assert thresholds[NUM_BUCKETS - 1] <= tile + 1
    heads = rel_bias.shape[1]
    return pl.pallas_call(
        functools.partial(_bias_kernel, thresholds=thresholds, tile=tile),
        grid=(heads,),
        in_specs=[pl.BlockSpec(memory_space=pltpu.SMEM)],
        out_specs=pl.BlockSpec((1, 3, tile, tile), lambda h: (h, 0, 0, 0)),
        out_shape=jax.ShapeDtypeStruct((heads, 3, tile, tile), F32),
        compiler_params=_cparams(1),
        name="attn_rel_bias_tiles",
    )(rel_bias)


def _attn_kernel(q_ref, k_ref, v_ref, bias_ref, lam_ref, sg_ref, o_ref, v1_ref,
                 s0_ref, s1_ref, p0_ref, p1_ref, m0_ref, m1_ref, *, tile, lambda_init):
    seq = q_ref.shape[0]
    dv = ATTN_V_DIM
    n_tiles = seq // tile
    s_refs, p_refs, m_refs = (s0_ref, s1_ref), (p0_ref, p1_ref), (m0_ref, m1_ref)
    v1_ref[:, :dv] = v_ref[...]
    v1_ref[:, dv:] = jnp.ones((seq, dv), BF16)
    lam = lam_ref[...]
    lam_full = (jnp.exp(jnp.sum(lam[0:1] * lam[1:2], axis=-1, keepdims=True))
                - jnp.exp(jnp.sum(lam[2:3] * lam[3:4], axis=-1, keepdims=True)) + lambda_init)
    lane = lax.broadcasted_iota(jnp.int32, (tile, dv), 1)

    def stacked_q(i):
        q = q_ref[i * tile:(i + 1) * tile, :]
        zero = jnp.zeros_like(q)
        return jnp.concatenate([jnp.where(lane < ATTN_HEAD_DIM, q, zero),
                                jnp.where(lane >= ATTN_HEAD_DIM, q, zero)], axis=0)

    def scores(i, j, qs):
        bt = bias_ref[0, min(i - j, 2)]
        s_refs[i % 2][:, j * tile:(j + 1) * tile] = (
            _dot_nt(qs, k_ref[j * tile:(j + 1) * tile, :]) + jnp.concatenate([bt, bt], axis=0))

    def row_max(i):
        m = jnp.max(s_refs[i % 2][:, :(i + 1) * tile], axis=-1, keepdims=True)
        m_refs[i % 2][...] = jnp.broadcast_to(m, m_refs[i % 2].shape)

    def probs(i, j):
        m = m_refs[i % 2][...]
        for c in range(j * tile, (j + 1) * tile, LANES):
            cols = slice(c, c + LANES)
            p_refs[i % 2][:, cols] = jnp.exp2(s_refs[i % 2][:, cols] - m).astype(BF16)

    def finish(i):
        n_keys = (i + 1) * tile
        acc = _dot(p_refs[i % 2][:, :n_keys], v1_ref[:n_keys, :])
        o_all = acc[:, :dv] / acc[:, dv:]
        o = o_all[:tile] - lam_full * o_all[tile:]
        ms = jnp.mean(o * o, axis=-1, keepdims=True)
        o = o * lax.rsqrt(ms + EPS) * sg_ref[...] * (1.0 - lambda_init)
        o_ref[i * tile:(i + 1) * tile, :] = o.astype(BF16)

    def score_and_prob_steps(i_scores, i_probs):
        qs = stacked_q(i_scores) if i_scores >= 0 else None
        for j in range(max(i_scores, i_probs) + 1):
            if j <= i_scores:
                scores(i_scores, j, qs)
            if j <= i_probs:
                probs(i_probs, j)
        if i_scores >= 0:
            row_max(i_scores)

    last = n_tiles - 1
    score_and_prob_steps(last, -1)
    score_and_prob_steps(last - 1, last)
    for i in range(last, -1, -1):
        finish(i)
        score_and_prob_steps(i - 2, i - 1)


def _attn_call(q2, k2, v2, bias_tiles, lam, subln_gain2, batch, seq, lambda_init):
    t, d = q2.shape
    tile = ATTN_TILE
    heads = d // ATTN_V_DIM
    assert seq % tile == 0
    blk = pl.BlockSpec((seq, ATTN_V_DIM), lambda b, h: (b, h))
    return pl.pallas_call(
        functools.partial(_attn_kernel, tile=tile, lambda_init=lambda_init),
        grid=(batch, heads),
        in_specs=[
            blk, blk, blk,
            pl.BlockSpec((1, 3, tile, tile), lambda b, h: (h, 0, 0, 0)),
            pl.BlockSpec(lam.shape, lambda b, h: (0, 0)),
            pl.BlockSpec((1, ATTN_V_DIM), lambda b, h: (0, 0)),
        ],
        out_specs=blk,
        out_shape=jax.ShapeDtypeStruct((t, d), BF16),
        scratch_shapes=[
            pltpu.VMEM((seq, 2 * ATTN_V_DIM), BF16),
            pltpu.VMEM((2 * tile, seq), F32),
            pltpu.VMEM((2 * tile, seq), F32),
            pltpu.VMEM((2 * tile, seq), BF16),
            pltpu.VMEM((2 * tile, seq), BF16),
            pltpu.VMEM((2 * tile, LANES), F32),
            pltpu.VMEM((2 * tile, LANES), F32),
        ],
        compiler_params=_cparams(2),
        name="diff_attention",
    )(q2, k2, v2, bias_tiles, lam, subln_gain2)


def _hgrn_in_kernel(x_ref, mod_ref, g_ref, w_ref, lbl_ref, q_ref, k_ref, lf_ref, v_ref, og_ref,
                    *, d, layer_j):
    h = _norm_mod(x_ref[...], g_ref[...], mod_ref).astype(BF16)
    logits = lbl_ref[...]
    e = jnp.exp(logits - jnp.max(logits, axis=0, keepdims=True))
    p = e / jnp.sum(e, axis=0, keepdims=True)
    lb = jnp.zeros((1, d), F32)
    for i in range(1, layer_j + 1):
        lb = lb + p[i:i + 1, :]
    log_lb = jnp.log(lb)
    log_1m_lb = jnp.log1p(-lb)

    def dot_cols(w_lo, lo, hi):
        return lambda: _dot(h, w_ref[:, w_lo + lo:w_lo + hi])

    def forget_stage(lo, hi):
        def consume(f):
            ef = jnp.exp(-jnp.abs(f))
            one_plus = 1.0 + ef
            log_sig = jnp.minimum(f, 0.0) - jnp.log(one_plus)
            sig_neg = jnp.where(f >= 0.0, ef, 1.0) / one_plus
            if layer_j == 0:
                lf_ref[:, lo:hi] = log_sig
                k_ref[:, lo:hi] = sig_neg
            else:
                a = log_lb[:, lo:hi]
                b = log_1m_lb[:, lo:hi] + log_sig
                lf_ref[:, lo:hi] = jnp.maximum(a, b) + jnp.log(1.0 + jnp.exp(-jnp.abs(a - b)))
                k_ref[:, lo:hi] = (1.0 - lb[:, lo:hi]) * sig_neg

        return dot_cols(d, lo, hi), consume

    def query_stage(lo, hi):
        def consume(z):
            q_ref[:, lo:hi] = z

        return dot_cols(0, lo, hi), consume

    def value_stage(lo, hi):
        def consume(z):
            v_ref[:, lo:hi] = z.astype(BF16)

        return dot_cols(2 * d, lo, hi), consume

    def gate_stage(lo, hi):
        def consume(g):
            og_ref[:, lo:hi] = g * _sigmoid(g)

        return dot_cols(3 * d, lo, hi), consume

    stages = []
    for lo, hi in _col_pieces(d, HGRN_COLS):
        stages += [forget_stage(lo, hi), query_stage(lo, hi), gate_stage(lo, hi), value_stage(lo, hi)]
    _pipelined(stages)


def _hgrn_in_call(x2, mod3, gain, w_in, lb_logits, layer_j, seq):
    t, d = x2.shape
    tm = min(512, seq)
    assert t % tm == 0 and seq % tm == 0
    per_seq = seq // tm
    row = pl.BlockSpec((tm, d), lambda i: (i, 0))
    f32_out = jax.ShapeDtypeStruct((t, d), F32)
    return pl.pallas_call(
        functools.partial(_hgrn_in_kernel, d=d, layer_j=layer_j),
        grid=(t // tm,),
        in_specs=[
            row,
            pl.BlockSpec((1, 3, d), lambda i: (i // per_seq, 0, 0)),
            _resident((1, d)),
            _resident((d, 4 * d)),
            _resident(lb_logits.shape),
        ],
        out_specs=[row, row, row, row, row],
        out_shape=[f32_out, f32_out, f32_out, jax.ShapeDtypeStruct((t, d), BF16), f32_out],
        compiler_params=_cparams(1),
        name="hgrn_in_proj",
    )(x2, mod3, gain, w_in, lb_logits)


def _rows_at(g_cum, half):
    c = g_cum.shape[0]
    if half >= 4:
        blk = 2 * half
        return jnp.concatenate(
            [jnp.broadcast_to(g_cum[p * blk + half - 1:p * blk + half, :], (blk, g_cum.shape[1]))
             for p in range(c // blk)], axis=0)
    row = lax.broadcasted_iota(jnp.int32, g_cum.shape, 0)
    if half == 2:
        r4 = row % 4
        up1 = pltpu.roll(g_cum, c - 1, 0)
        dn1 = pltpu.roll(g_cum, 1, 0)
        dn2 = pltpu.roll(g_cum, 2, 0)
        return jnp.where(r4 == 0, up1, jnp.where(r4 == 1, g_cum, jnp.where(r4 == 2, dn1, dn2)))
    assert half == 1
    return jnp.where(row % 2 == 0, g_cum, pltpu.roll(g_cum, 1, 0))


def _round_robin(generators, width):
    pending = list(generators)
    active = []
    while pending or active:
        while pending and len(active) < width:
            active.append(pending.pop(0))
        for gen in list(active):
            try:
                next(gen)
            except StopIteration:
                active.remove(gen)


def _hgrn_rec_kernel(q_ref, k_ref, lf_ref, v_ref, og_ref, gain_ref, o_ref,
                     qe_ref, a_ref, u_ref, dl_ref, st_ref, *, chunk):
    seq = q_ref.shape[0]
    c = chunk
    n_chunks = seq // c
    ri = lax.broadcasted_iota(jnp.int32, (c, c), 0)
    ci = lax.broadcasted_iota(jnp.int32, (c, c), 1)
    tri = jnp.where(ri >= ci, 1.0, 0.0).astype(BF16)
    pair_key = jnp.where(ri >= ci, ri ^ ci, -1)
    row = lax.broadcasted_iota(jnp.int32, (c, HGRN_DK), 0)

    def neg_abs(z):
        return lax.bitcast_convert_type(
            lax.bitcast_convert_type(z, jnp.int32) | jnp.int32(-2 ** 31), F32)

    def intra(n):
        sl = slice(n * c, (n + 1) * c)
        g = lf_ref[sl, :]
        q = q_ref[sl, :]
        k = k_ref[sl, :]
        g_hi = g.astype(BF16)
        r1 = g - g_hi.astype(F32)
        g_mid = r1.astype(BF16)
        g_lo = (r1 - g_mid.astype(F32)).astype(BF16)
        g2 = (_dot(tri, g_hi) + (_dot(tri, g_mid) + _dot(tri, g_lo))) * LOG2E
        g2_last = g2[c - 1:c, :]
        yield
        qe_ref[sl, :] = (q * jnp.exp2(g2)).astype(BF16)
        k_dec = (k * jnp.exp2(g2_last - g2)).astype(BF16)
        u_ref[n] = _dot_tn(v_ref[sl, :], k_dec)
        dl_ref[n] = jnp.exp2(g2_last)
        scores = jnp.where(pair_key == 0, _dot_nt(q.astype(BF16), k.astype(BF16)), 0.0)
        yield
        half = 1
        while half < c:
            e = jnp.exp2(neg_abs(g2 - _rows_at(g2, half)))
            if half >= 8:
                qk = jnp.concatenate(
                    [(q if (b % 2) else k)[b * half:(b + 1) * half] for b in range(c // half)], axis=0)
            else:
                qk = jnp.where((row & (2 * half - 1)) >= half, q, k)
            zf = qk * e
            z = zf.astype(BF16)
            if half >= 8:
                z_right = jnp.concatenate(
                    [zf[b * half:(b + 1) * half] for b in range(1, c // half, 2)], axis=0).astype(BF16)
                level = _dot_nt(z_right, z)
                pieces = []
                for b in range(c // half):
                    rows = slice(b * half, (b + 1) * half)
                    piece = scores[rows]
                    if b % 2:
                        lv = level[(b // 2) * half:(b // 2 + 1) * half]
                        piece = jnp.where(pair_key[rows] >= half, lv, piece)
                    pieces.append(piece)
                scores = jnp.concatenate(pieces, axis=0)
            else:
                scores = jnp.where(pair_key >= half, _dot_nt(z, z), scores)
            half *= 2
            yield
        a_ref[n] = scores.astype(BF16)

    def outputs(n):
        sl = slice(n * c, (n + 1) * c)
        o = _dot_nt(qe_ref[sl, :], st_ref[n]) + _dot(a_ref[n], v_ref[sl, :])
        yield
        ms = jnp.mean(o * o, axis=-1, keepdims=True)
        o_ref[sl, :] = (o * lax.rsqrt(ms + EPS) * gain_ref[...] * og_ref[sl, :]).astype(BF16)

    _round_robin([intra(n) for n in range(n_chunks)], HGRN_INTERLEAVE)

    state = jnp.zeros((HGRN_DV, HGRN_DK), F32)
    for n in range(n_chunks):
        st_ref[n] = state.astype(BF16)
        state = dl_ref[n] * state + u_ref[n]

    _round_robin([outputs(n) for n in range(n_chunks)], HGRN_INTERLEAVE)


def _hgrn_rec_call(q2, k2, lf2, v2, og2, out_gain2, batch, seq):
    t, d = q2.shape
    heads = d // HGRN_DV
    c = HGRN_CHUNK
    assert seq % c == 0
    n_chunks = seq // c
    blk = pl.BlockSpec((seq, HGRN_DV), lambda b, h: (b, h))
    return pl.pallas_call(
        functools.partial(_hgrn_rec_kernel, chunk=c),
        grid=(batch, heads),
        in_specs=[blk, blk, blk, blk, blk, pl.BlockSpec((1, HGRN_DV), lambda b, h: (0, h))],
        out_specs=blk,
        out_shape=jax.ShapeDtypeStruct((t, d), BF16),
        scratch_shapes=[
            pltpu.VMEM((seq, HGRN_DK), BF16),
            pltpu.VMEM((n_chunks, c, c), BF16),
            pltpu.VMEM((n_chunks, HGRN_DV, HGRN_DK), F32),
            pltpu.VMEM((n_chunks, 1, HGRN_DK), F32),
            pltpu.VMEM((n_chunks, HGRN_DV, HGRN_DK), BF16),
        ],
        compiler_params=_cparams(2),
        name="hgrn_recurrence",
    )(q2, k2, lf2, v2, og2, out_gain2)


def kernel(x, c, ada_w, ada_b, norm_g, ffn_w_in, ffn_w_down, attn_w_qkv, attn_w_o, attn_q_gain,
           attn_k_gain, attn_lambda, attn_subln_gain, rel_bias, hgrn_w_in, hgrn_w_o,
           hgrn_out_gain, hgrn_lb_logits):
    batch, seq, d = x.shape
    depth = ada_w.shape[0]
    mod = _ada_call(c, ada_w, ada_b).reshape(depth, batch, N_SUBLAYERS, 3, d)
    bias_tiles = _bias_call(rel_bias, ATTN_TILE)
    x2 = x.reshape(batch * seq, d)
    for layer in range(depth):
        gains = norm_g[layer].reshape(N_SUBLAYERS, 1, d)
        x2 = _ffn_call(x2, mod[layer, :, 0], gains[0], ffn_w_in[layer, 0].astype(BF16),
                       ffn_w_down[layer, 0].astype(BF16), seq)
        j = layer // N_MIXERS
        if layer % N_MIXERS == 0:
            reps = QKV_COLS // ATTN_HEAD_DIM
            q2, k2, v2 = _qkv_call(
                x2, mod[layer, :, 1], gains[1], attn_w_qkv[j].astype(BF16),
                jnp.tile(attn_q_gain[j], reps).reshape(1, QKV_COLS),
                jnp.tile(attn_k_gain[j], reps).reshape(1, QKV_COLS), seq)
            lambda_init = 0.8 - 0.6 * math.exp(-0.3 * layer)
            o2 = _attn_call(q2, k2, v2, bias_tiles, attn_lambda[j],
                            attn_subln_gain[j].reshape(1, ATTN_V_DIM), batch, seq, lambda_init)
            w_o = attn_w_o[j]
        else:
            q2, k2, lf2, v2, og2 = _hgrn_in_call(
                x2, mod[layer, :, 1], gains[1], hgrn_w_in[j].astype(BF16), hgrn_lb_logits, j, seq)
            o2 = _hgrn_rec_call(q2, k2, lf2, v2, og2, hgrn_out_gain[j].reshape(1, d), batch, seq)
            w_o = hgrn_w_o[j]
        x2 = _ffn_call(x2, mod[layer, :, 2], gains[2], ffn_w_in[layer, 1].astype(BF16),
                       ffn_w_down[layer, 1].astype(BF16), seq,
                       mixer=(o2, mod[layer, :, 1], w_o.astype(BF16)))
    return x2.reshape(batch, seq, d)
```

```python
import functools
import math
from typing import NamedTuple

import numpy as np
import jax
import jax.numpy as jnp
from jax import lax
from jax.experimental import pallas as pl
from jax.experimental.pallas import tpu as pltpu

F32 = jnp.float32
BF16 = jnp.bfloat16

EPS = 1e-6
N_SUBLAYERS = 3
N_MIXERS = 2
ATTN_HEADS = 8
ATTN_HEAD_DIM = 64
ATTN_V_DIM = 128
NUM_BUCKETS = 32
MAX_DISTANCE = 128
HGRN_HEADS = 8
HGRN_DK = 128
HGRN_DV = 128

LANES = 128
VMEM_LIMIT = 56 * 1024 * 1024
NEG = -1e30
LOG2E = math.log2(math.e)

MXU_COLS = 256
ATTN_TILE = 256
QKV_COLS = MXU_COLS
FFN_COLS = 2 * MXU_COLS
HGRN_COLS = MXU_COLS
HGRN_CHUNK = 128
HGRN_INTERLEAVE = 8


def _cparams(n_axes):
    return pltpu.CompilerParams(
        dimension_semantics=("arbitrary",) * n_axes, vmem_limit_bytes=VMEM_LIMIT)


def _resident(shape):
    return pl.BlockSpec(shape, lambda *_: (0,) * len(shape), pipeline_mode=pl.Buffered(1))


class _Stacked(NamedTuple):
    array: jax.Array
    lead: tuple = ()

    @property
    def shape(self):
        return self.array.shape[len(self.lead):]

    def spec(self):
        lead, tail = self.lead, self.shape
        return pl.BlockSpec((None,) * len(lead) + tail, lambda *_: lead + (0,) * len(tail),
                            pipeline_mode=pl.Buffered(1))


def _sigmoid(z):
    return 1.0 / (1.0 + jnp.exp(-z))


def _dot(a, b):
    return jnp.dot(a, b, preferred_element_type=F32)


def _dot_nt(a, b):
    return lax.dot_general(a, b, (((1,), (1,)), ((), ())), preferred_element_type=F32)


def _dot_tn(a, b):
    return lax.dot_general(a, b, (((0,), (0,)), ((), ())), preferred_element_type=F32)


def _pipelined(stages):
    pending = None
    for produce, consume in stages:
        value = produce()
        if pending is not None:
            pending[1](pending[0])
        pending = (value, consume)
    if pending is not None:
        pending[1](pending[0])


def _col_pieces(total, width):
    return [(lo, min(lo + width, total)) for lo in range(0, total, width)]


def _norm_mod(x, gain, mod_ref):
    ms = jnp.mean(x * x, axis=-1, keepdims=True)
    h = x * lax.rsqrt(ms + EPS) * gain
    return h * (1.0 + mod_ref[0, 1:2, :]) + mod_ref[0, 0:1, :]


def _ada_kernel(c_ref, w_ref, b_ref, o_ref):
    c = c_ref[...]
    ca = (c * _sigmoid(c)).astype(BF16)
    o_ref[0] = _dot(ca, w_ref[0].astype(BF16)) + b_ref[0]


def _ada_call(c, ada_w, ada_b):
    depth, d, n = ada_w.shape
    b = c.shape[0]
    tn = 1152
    assert n % tn == 0
    return pl.pallas_call(
        _ada_kernel,
        grid=(depth, n // tn),
        in_specs=[
            pl.BlockSpec((b, d), lambda l, j: (0, 0)),
            pl.BlockSpec((1, d, tn), lambda l, j: (l, 0, j)),
            pl.BlockSpec((1, 1, tn), lambda l, j: (l, 0, j)),
        ],
        out_specs=pl.BlockSpec((1, b, tn), lambda l, j: (l, 0, j)),
        out_shape=jax.ShapeDtypeStruct((depth, b, n), F32),
        compiler_params=_cparams(2),
        name="ada_mod",
    )(c, ada_w, ada_b.reshape(depth, 1, n))


def _ffn_kernel(*refs, d_ff, with_mixer):
    if with_mixer:
        (x_ref, mix_ref, mixmod_ref, wo_ref, mod_ref, g_ref, win_ref, wdn_ref, o_ref, act_ref) = refs
        x = x_ref[...] + mixmod_ref[0, 2:3, :] * _dot(mix_ref[...], wo_ref[...])
    else:
        (x_ref, mod_ref, g_ref, win_ref, wdn_ref, o_ref, act_ref) = refs
        x = x_ref[...]
    h = _norm_mod(x, g_ref[...], mod_ref).astype(BF16)

    def stage(lo, hi):
        def produce():
            return _dot(h, win_ref[:, lo:hi]), _dot(h, win_ref[:, d_ff + lo:d_ff + hi])

        def consume(ab):
            a, b = ab
            act_ref[:, lo:hi] = (a * _sigmoid(a) * b).astype(BF16)

        return produce, consume

    _pipelined([stage(lo, hi) for lo, hi in _col_pieces(d_ff, FFN_COLS)])
    y = _dot(act_ref[...], wdn_ref[...])
    o_ref[...] = x + 0.5 * mod_ref[0, 2:3, :] * y


def _ffn_call(x2, mod3, gain, w_in, w_down, seq, mixer=None):
    t, d = x2.shape
    d_ff = w_down.shape[0]
    tm = min(1024, seq)
    assert t % tm == 0 and seq % tm == 0 and d_ff % MXU_COLS == 0
    per_seq = seq // tm
    row = pl.BlockSpec((tm, d), lambda i: (i, 0))
    mod_spec = pl.BlockSpec((1, 3, d), lambda i: (i // per_seq, 0, 0))
    operands, in_specs = [x2], [row]
    if mixer is not None:
        mix, mix_mod3, w_o = mixer
        operands += [mix, mix_mod3, w_o.array]
        in_specs += [row, mod_spec, w_o.spec()]
    operands += [mod3, gain, w_in.array, w_down.array]
    in_specs += [mod_spec, _resident((1, d)), w_in.spec(), w_down.spec()]
    return pl.pallas_call(
        functools.partial(_ffn_kernel, d_ff=d_ff, with_mixer=mixer is not None),
        grid=(t // tm,),
        in_specs=in_specs,
        out_specs=row,
        out_shape=jax.ShapeDtypeStruct((t, d), F32),
        scratch_shapes=[pltpu.VMEM((tm, d_ff), BF16)],
        compiler_params=_cparams(1),
        name="ffn_half_step",
    )(*operands)


def _group_mean_sq(z, ones_blk):
    return _dot((z * z).astype(BF16), ones_blk)


def _qkv_kernel(x_ref, mod_ref, g_ref, w_ref, qg_ref, kg_ref, q_ref, k_ref, v_ref, *, d):
    h = _norm_mod(x_ref[...], g_ref[...], mod_ref).astype(BF16)
    tn = QKV_COLS
    r = lax.broadcasted_iota(jnp.int32, (tn, tn), 0) // ATTN_HEAD_DIM
    c = lax.broadcasted_iota(jnp.int32, (tn, tn), 1) // ATTN_HEAD_DIM
    ones_blk = jnp.where(r == c, 1.0 / ATTN_HEAD_DIM, 0.0).astype(BF16)
    q_gain = qg_ref[...] * (ATTN_HEAD_DIM ** -0.5 * LOG2E)
    k_gain = kg_ref[...]

    def normed(w_lo, out_ref, lo, hi, gain):
        def consume(z):
            out_ref[:, lo:hi] = (z * lax.rsqrt(_group_mean_sq(z, ones_blk) + EPS) * gain).astype(BF16)

        return (lambda: _dot(h, w_ref[:, w_lo + lo:w_lo + hi])), consume

    def plain(w_lo, out_ref, lo, hi):
        def consume(z):
            out_ref[:, lo:hi] = z.astype(BF16)

        return (lambda: _dot(h, w_ref[:, w_lo + lo:w_lo + hi])), consume

    stages = []
    for lo, hi in _col_pieces(d, tn):
        stages += [normed(0, q_ref, lo, hi, q_gain), normed(d, k_ref, lo, hi, k_gain),
                   plain(2 * d, v_ref, lo, hi)]
    _pipelined(stages)


def _qkv_call(x2, mod3, gain, w_qkv, q_gain2, k_gain2, seq):
    t, d = x2.shape
    tm = min(512, seq)
    assert t % tm == 0 and seq % tm == 0
    per_seq = seq // tm
    out = jax.ShapeDtypeStruct((t, d), BF16)
    row = pl.BlockSpec((tm, d), lambda i: (i, 0))
    return pl.pallas_call(
        functools.partial(_qkv_kernel, d=d),
        grid=(t // tm,),
        in_specs=[
            row,
            pl.BlockSpec((1, 3, d), lambda i: (i // per_seq, 0, 0)),
            _resident((1, d)),
            w_qkv.spec(),
            _resident((1, QKV_COLS)),
            _resident((1, QKV_COLS)),
        ],
        out_specs=[row, row, row],
        out_shape=[out, out, out],
        compiler_params=_cparams(1),
        name="attn_qkv_proj",
    )(x2, mod3, gain, w_qkv.array, q_gain2, k_gain2)


def _bucket_thresholds():
    max_exact = NUM_BUCKETS // 2
    dist = np.arange(0, 4 * MAX_DISTANCE)
    d_f = np.maximum(dist, 1).astype(np.float32)
    large = max_exact + (np.log(d_f / np.float32(max_exact)) / np.float32(math.log(MAX_DISTANCE / max_exact))
                         * np.float32(NUM_BUCKETS - max_exact)).astype(np.int32)
    bucket = np.where(dist < max_exact, dist, np.minimum(large, NUM_BUCKETS - 1))
    assert np.all(np.diff(bucket) >= 0) and bucket[-1] == NUM_BUCKETS - 1
    return [int(np.argmax(bucket >= b)) for b in range(NUM_BUCKETS)]


def _bias_kernel(rb_ref, o_ref, *, thresholds, tile):
    h = pl.program_id(0)
    i = lax.broadcasted_iota(jnp.int32, (tile, tile), 0)
    j = lax.broadcasted_iota(jnp.int32, (tile, tile), 1)
    for off in range(3):
        dist = off * tile + i - j
        val = jnp.full((tile, tile), rb_ref[0, h], F32)
        for b in range(1, NUM_BUCKETS):
            val = jnp.where(dist >= thresholds[b], rb_ref[b, h], val)
        o_ref[0, off] = jnp.where(dist >= 0, val * LOG2E, NEG)


def _bias_call(rel_bias, tile):
    thresholds = _bucket_thresholds()
    assert thresholds[NUM_BUCKETS - 1] <= tile + 1
    heads = rel_bias.shape[1]
    return pl.pallas_call(
        functools.partial(_bias_kernel, thresholds=thresholds, tile=tile),
        grid=(heads,),
        in_specs=[pl.BlockSpec(memory_space=pltpu.SMEM)],
        out_specs=pl.BlockSpec((1, 3, tile, tile), lambda h: (h, 0, 0, 0)),
        out_shape=jax.ShapeDtypeStruct((heads, 3, tile, tile), F32),
        compiler_params=_cparams(1),
        name="attn_rel_bias_tiles",
    )(rel_bias)


def _attn_kernel(q_ref, k_ref, v_ref, bias_ref, lam_ref, sg_ref, o_ref, v1_ref,
                 s0_ref, s1_ref, p0_ref, p1_ref, m0_ref, m1_ref, *, tile, lambda_init):
    seq = q_ref.shape[0]
    dv = ATTN_V_DIM
    n_tiles = seq // tile
    s_refs, p_refs, m_refs = (s0_ref, s1_ref), (p0_ref, p1_ref), (m0_ref, m1_ref)
    v1_ref[:, :dv] = v_ref[...]
    v1_ref[:, dv:] = jnp.ones((seq, dv), BF16)
    lam = lam_ref[...]
    lam_full = (jnp.exp(jnp.sum(lam[0:1] * lam[1:2], axis=-1, keepdims=True))
                - jnp.exp(jnp.sum(lam[2:3] * lam[3:4], axis=-1, keepdims=True)) + lambda_init)
    lane = lax.broadcasted_iota(jnp.int32, (tile, dv), 1)

    def stacked_q(i):
        q = q_ref[i * tile:(i + 1) * tile, :]
        zero = jnp.zeros_like(q)
        return jnp.concatenate([jnp.where(lane < ATTN_HEAD_DIM, q, zero),
                                jnp.where(lane >= ATTN_HEAD_DIM, q, zero)], axis=0)

    def scores(i, j, qs):
        bt = bias_ref[0, min(i - j, 2)]
        s_refs[i % 2][:, j * tile:(j + 1) * tile] = (
            _dot_nt(qs, k_ref[j * tile:(j + 1) * tile, :]) + jnp.concatenate([bt, bt], axis=0))

    def row_max(i):
        m = jnp.max(s_refs[i % 2][:, :(i + 1) * tile], axis=-1, keepdims=True)
        m_refs[i % 2][...] = jnp.broadcast_to(m, m_refs[i % 2].shape)

    def probs(i, j):
        m = m_refs[i % 2][...]
        for c in range(j * tile, (j + 1) * tile, LANES):
            cols = slice(c, c + LANES)
            p_refs[i % 2][:, cols] = jnp.exp2(s_refs[i % 2][:, cols] - m).astype(BF16)

    def finish(i):
        n_keys = (i + 1) * tile
        acc = _dot(p_refs[i % 2][:, :n_keys], v1_ref[:n_keys, :])
        o_all = acc[:, :dv] / acc[:, dv:]
        o = o_all[:tile] - lam_full * o_all[tile:]
        ms = jnp.mean(o * o, axis=-1, keepdims=True)
        o = o * lax.rsqrt(ms + EPS) * sg_ref[...] * (1.0 - lambda_init)
        o_ref[i * tile:(i + 1) * tile, :] = o.astype(BF16)

    def score_and_prob_steps(i_scores, i_probs):
        qs = stacked_q(i_scores) if i_scores >= 0 else None
        for j in range(max(i_scores, i_probs) + 1):
            if j <= i_scores:
                scores(i_scores, j, qs)
            if j <= i_probs:
                probs(i_probs, j)
        if i_scores >= 0:
            row_max(i_scores)

    last = n_tiles - 1
    score_and_prob_steps(last, -1)
    score_and_prob_steps(last - 1, last)
    for i in range(last, -1, -1):
        finish(i)
        score_and_prob_steps(i - 2, i - 1)


def _attn_call(q2, k2, v2, bias_tiles, lam, subln_gain2, batch, seq, lambda_init):
    t, d = q2.shape
    tile = ATTN_TILE
    heads = d // ATTN_V_DIM
    assert seq % tile == 0
    blk = pl.BlockSpec((seq, ATTN_V_DIM), lambda b, h: (b, h))
    return pl.pallas_call(
        functools.partial(_attn_kernel, tile=tile, lambda_init=lambda_init),
        grid=(batch, heads),
        in_specs=[
            blk, blk, blk,
            pl.BlockSpec((1, 3, tile, tile), lambda b, h: (h, 0, 0, 0)),
            pl.BlockSpec(lam.shape, lambda b, h: (0, 0)),
            pl.BlockSpec((1, ATTN_V_DIM), lambda b, h: (0, 0)),
        ],
        out_specs=blk,
        out_shape=jax.ShapeDtypeStruct((t, d), BF16),
        scratch_shapes=[
            pltpu.VMEM((seq, 2 * ATTN_V_DIM), BF16),
            pltpu.VMEM((2 * tile, seq), F32),
            pltpu.VMEM((2 * tile, seq), F32),
            pltpu.VMEM((2 * tile, seq), BF16),
            pltpu.VMEM((2 * tile, seq), BF16),
            pltpu.VMEM((2 * tile, LANES), F32),
            pltpu.VMEM((2 * tile, LANES), F32),
        ],
        compiler_params=_cparams(2),
        name="diff_attention",
    )(q2, k2, v2, bias_tiles, lam, subln_gain2)


def _hgrn_in_kernel(x_ref, mod_ref, g_ref, w_ref, lbl_ref, q_ref, k_ref, lf_ref, v_ref, og_ref,
                    *, d, layer_j):
    h = _norm_mod(x_ref[...], g_ref[...], mod_ref).astype(BF16)
    logits = lbl_ref[...]
    e = jnp.exp(logits - jnp.max(logits, axis=0, keepdims=True))
    p = e / jnp.sum(e, axis=0, keepdims=True)
    lb = jnp.zeros((1, d), F32)
    for i in range(1, layer_j + 1):
        lb = lb + p[i:i + 1, :]
    log_lb = jnp.log(lb)
    log_1m_lb = jnp.log1p(-lb)

    def dot_cols(w_lo, lo, hi):
        return lambda: _dot(h, w_ref[:, w_lo + lo:w_lo + hi])

    def forget_stage(lo, hi):
        def consume(f):
            ef = jnp.exp(-jnp.abs(f))
            one_plus = 1.0 + ef
            log_sig = jnp.minimum(f, 0.0) - jnp.log(one_plus)
            sig_neg = jnp.where(f >= 0.0, ef, 1.0) / one_plus
            if layer_j == 0:
                lf_ref[:, lo:hi] = log_sig
                k_ref[:, lo:hi] = sig_neg
            else:
                a = log_lb[:, lo:hi]
                b = log_1m_lb[:, lo:hi] + log_sig
                lf_ref[:, lo:hi] = jnp.maximum(a, b) + jnp.log(1.0 + jnp.exp(-jnp.abs(a - b)))
                k_ref[:, lo:hi] = (1.0 - lb[:, lo:hi]) * sig_neg

        return dot_cols(d, lo, hi), consume

    def query_stage(lo, hi):
        def consume(z):
            q_ref[:, lo:hi] = z

        return dot_cols(0, lo, hi), consume

    def value_stage(lo, hi):
        def consume(z):
            v_ref[:, lo:hi] = z.astype(BF16)

        return dot_cols(2 * d, lo, hi), consume

    def gate_stage(lo, hi):
        def consume(g):
            og_ref[:, lo:hi] = g * _sigmoid(g)

        return dot_cols(3 * d, lo, hi), consume

    stages = []
    for lo, hi in _col_pieces(d, HGRN_COLS):
        stages += [forget_stage(lo, hi), query_stage(lo, hi), gate_stage(lo, hi), value_stage(lo, hi)]
    _pipelined(stages)


def _hgrn_in_call(x2, mod3, gain, w_in, lb_logits, layer_j, seq):
    t, d = x2.shape
    tm = min(512, seq)
    assert t % tm == 0 and seq % tm == 0
    per_seq = seq // tm
    row = pl.BlockSpec((tm, d), lambda i: (i, 0))
    f32_out = jax.ShapeDtypeStruct((t, d), F32)
    return pl.pallas_call(
        functools.partial(_hgrn_in_kernel, d=d, layer_j=layer_j),
        grid=(t // tm,),
        in_specs=[
            row,
            pl.BlockSpec((1, 3, d), lambda i: (i // per_seq, 0, 0)),
            _resident((1, d)),
            w_in.spec(),
            _resident(lb_logits.shape),
        ],
        out_specs=[row, row, row, row, row],
        out_shape=[f32_out, f32_out, f32_out, jax.ShapeDtypeStruct((t, d), BF16), f32_out],
        compiler_params=_cparams(1),
        name="hgrn_in_proj",
    )(x2, mod3, gain, w_in.array, lb_logits)


def _rows_at(g_cum, half):
    c = g_cum.shape[0]
    if half >= 4:
        blk = 2 * half
        return jnp.concatenate(
            [jnp.broadcast_to(g_cum[p * blk + half - 1:p * blk + half, :], (blk, g_cum.shape[1]))
             for p in range(c // blk)], axis=0)
    row = lax.broadcasted_iota(jnp.int32, g_cum.shape, 0)
    if half == 2:
        r4 = row % 4
        up1 = pltpu.roll(g_cum, c - 1, 0)
        dn1 = pltpu.roll(g_cum, 1, 0)
        dn2 = pltpu.roll(g_cum, 2, 0)
        return jnp.where(r4 == 0, up1, jnp.where(r4 == 1, g_cum, jnp.where(r4 == 2, dn1, dn2)))
    assert half == 1
    return jnp.where(row % 2 == 0, g_cum, pltpu.roll(g_cum, 1, 0))


def _round_robin(generators, width):
    pending = list(generators)
    active = []
    while pending or active:
        while pending and len(active) < width:
            active.append(pending.pop(0))
        for gen in list(active):
            try:
                next(gen)
            except StopIteration:
                active.remove(gen)


def _hgrn_rec_kernel(q_ref, k_ref, lf_ref, v_ref, og_ref, gain_ref, o_ref,
                     qe_ref, a_ref, u_ref, dl_ref, st_ref, *, chunk):
    seq = q_ref.shape[0]
    c = chunk
    n_chunks = seq // c
    ri = lax.broadcasted_iota(jnp.int32, (c, c), 0)
    ci = lax.broadcasted_iota(jnp.int32, (c, c), 1)
    tri = jnp.where(ri >= ci, 1.0, 0.0).astype(BF16)
    pair_key = jnp.where(ri >= ci, ri ^ ci, -1)
    row = lax.broadcasted_iota(jnp.int32, (c, HGRN_DK), 0)

    def neg_abs(z):
        return lax.bitcast_convert_type(
            lax.bitcast_convert_type(z, jnp.int32) | jnp.int32(-2 ** 31), F32)

    def intra(n):
        sl = slice(n * c, (n + 1) * c)
        g = lf_ref[sl, :]
        q = q_ref[sl, :]
        k = k_ref[sl, :]
        g_hi = g.astype(BF16)
        r1 = g - g_hi.astype(F32)
        g_mid = r1.astype(BF16)
        g_lo = (r1 - g_mid.astype(F32)).astype(BF16)
        g2 = (_dot(tri, g_hi) + (_dot(tri, g_mid) + _dot(tri, g_lo))) * LOG2E
        g2_last = g2[c - 1:c, :]
        yield
        qe_ref[sl, :] = (q * jnp.exp2(g2)).astype(BF16)
        k_dec = (k * jnp.exp2(g2_last - g2)).astype(BF16)
        u_ref[n] = _dot_tn(v_ref[sl, :], k_dec)
        dl_ref[n] = jnp.exp2(g2_last)
        scores = jnp.where(pair_key == 0, _dot_nt(q.astype(BF16), k.astype(BF16)), 0.0)
        yield
        half = 1
        while half < c:
            e = jnp.exp2(neg_abs(g2 - _rows_at(g2, half)))
            if half >= 8:
                qk = jnp.concatenate(
                    [(q if (b % 2) else k)[b * half:(b + 1) * half] for b in range(c // half)], axis=0)
            else:
                qk = jnp.where((row & (2 * half - 1)) >= half, q, k)
            zf = qk * e
            z = zf.astype(BF16)
            if half >= 8:
                z_right = jnp.concatenate(
                    [zf[b * half:(b + 1) * half] for b in range(1, c // half, 2)], axis=0).astype(BF16)
                level = _dot_nt(z_right, z)
                pieces = []
                for b in range(c // half):
                    rows = slice(b * half, (b + 1) * half)
                    piece = scores[rows]
                    if b % 2:
                        lv = level[(b // 2) * half:(b // 2 + 1) * half]
                        piece = jnp.where(pair_key[rows] >= half, lv, piece)
                    pieces.append(piece)
                scores = jnp.concatenate(pieces, axis=0)
            else:
                scores = jnp.where(pair_key >= half, _dot_nt(z, z), scores)
            half *= 2
            yield
        a_ref[n] = scores.astype(BF16)

    def outputs(n):
        sl = slice(n * c, (n + 1) * c)
        o = _dot_nt(qe_ref[sl, :], st_ref[n]) + _dot(a_ref[n], v_ref[sl, :])
        yield
        ms = jnp.mean(o * o, axis=-1, keepdims=True)
        o_ref[sl, :] = (o * lax.rsqrt(ms + EPS) * gain_ref[...] * og_ref[sl, :]).astype(BF16)

    _round_robin([intra(n) for n in range(n_chunks)], HGRN_INTERLEAVE)

    state = jnp.zeros((HGRN_DV, HGRN_DK), F32)
    for n in range(n_chunks):
        st_ref[n] = state.astype(BF16)
        state = dl_ref[n] * state + u_ref[n]

    _round_robin([outputs(n) for n in range(n_chunks)], HGRN_INTERLEAVE)


def _hgrn_rec_call(q2, k2, lf2, v2, og2, out_gain2, batch, seq):
    t, d = q2.shape
    heads = d // HGRN_DV
    c = HGRN_CHUNK
    assert seq % c == 0
    n_chunks = seq // c
    blk = pl.BlockSpec((seq, HGRN_DV), lambda b, h: (b, h))
    return pl.pallas_call(
        functools.partial(_hgrn_rec_kernel, chunk=c),
        grid=(batch, heads),
        in_specs=[blk, blk, blk, blk, blk, pl.BlockSpec((1, HGRN_DV), lambda b, h: (0, h))],
        out_specs=blk,
        out_shape=jax.ShapeDtypeStruct((t, d), BF16),
        scratch_shapes=[
            pltpu.VMEM((seq, HGRN_DK), BF16),
            pltpu.VMEM((n_chunks, c, c), BF16),
            pltpu.VMEM((n_chunks, HGRN_DV, HGRN_DK), F32),
            pltpu.VMEM((n_chunks, 1, HGRN_DK), F32),
            pltpu.VMEM((n_chunks, HGRN_DV, HGRN_DK), BF16),
        ],
        compiler_params=_cparams(2),
        name="hgrn_recurrence",
    )(q2, k2, lf2, v2, og2, out_gain2)


def kernel(x, c, ada_w, ada_b, norm_g, ffn_w_in, ffn_w_down, attn_w_qkv, attn_w_o, attn_q_gain,
           attn_k_gain, attn_lambda, attn_subln_gain, rel_bias, hgrn_w_in, hgrn_w_o,
           hgrn_out_gain, hgrn_lb_logits):
    batch, seq, d = x.shape
    depth = ada_w.shape[0]
    mod = _ada_call(c, ada_w, ada_b).reshape(depth, batch, N_SUBLAYERS, 3, d)
    bias_tiles = _bias_call(rel_bias, ATTN_TILE)
    x2 = x.reshape(batch * seq, d)
    ffn_in_b, ffn_down_b = ffn_w_in.astype(BF16), ffn_w_down.astype(BF16)
    qkv_b, attn_o_b = attn_w_qkv.astype(BF16), attn_w_o.astype(BF16)
    hgrn_in_b, hgrn_o_b = hgrn_w_in.astype(BF16), hgrn_w_o.astype(BF16)
    for layer in range(depth):
        gains = norm_g[layer].reshape(N_SUBLAYERS, 1, d)
        x2 = _ffn_call(x2, mod[layer, :, 0], gains[0], _Stacked(ffn_in_b, (layer, 0)),
                       _Stacked(ffn_down_b, (layer, 0)), seq)
        j = layer // N_MIXERS
        if layer % N_MIXERS == 0:
            reps = QKV_COLS // ATTN_HEAD_DIM
            q2, k2, v2 = _qkv_call(
                x2, mod[layer, :, 1], gains[1], _Stacked(qkv_b, (j,)),
                jnp.tile(attn_q_gain[j], reps).reshape(1, QKV_COLS),
                jnp.tile(attn_k_gain[j], reps).reshape(1, QKV_COLS), seq)
            lambda_init = 0.8 - 0.6 * math.exp(-0.3 * layer)
            o2 = _attn_call(q2, k2, v2, bias_tiles, attn_lambda[j],
                            attn_subln_gain[j].reshape(1, ATTN_V_DIM), batch, seq, lambda_init)
            w_o = _Stacked(attn_o_b, (j,))
        else:
            q2, k2, lf2, v2, og2 = _hgrn_in_call(
                x2, mod[layer, :, 1], gains[1], _Stacked(hgrn_in_b, (j,)), hgrn_lb_logits, j, seq)
            o2 = _hgrn_rec_call(q2, k2, lf2, v2, og2, hgrn_out_gain[j].reshape(1, d), batch, seq)
            w_o = _Stacked(hgrn_o_b, (j,))
        x2 = _ffn_call(x2, mod[layer, :, 2], gains[2], _Stacked(ffn_in_b, (layer, 1)),
                       _Stacked(ffn_down_b, (layer, 1)), seq, mixer=(o2, mod[layer, :, 1], w_o))
    return x2.reshape(batch, seq, d)
```

```python
import functools
import math
from typing import NamedTuple

import numpy as np
import jax
import jax.numpy as jnp
from jax import lax
from jax.experimental import pallas as pl
from jax.experimental.pallas import tpu as pltpu

F32 = jnp.float32
BF16 = jnp.bfloat16

EPS = 1e-6
N_SUBLAYERS = 3
N_MIXERS = 2
ATTN_HEADS = 8
ATTN_HEAD_DIM = 64
ATTN_V_DIM = 128
NUM_BUCKETS = 32
MAX_DISTANCE = 128
HGRN_HEADS = 8
HGRN_DK = 128
HGRN_DV = 128

LANES = 128
VMEM_LIMIT = 56 * 1024 * 1024
NEG = -1e30
LOG2E = math.log2(math.e)

MXU_COLS = 256
ATTN_TILE = 256
QKV_COLS = MXU_COLS
FFN_COLS = 2 * MXU_COLS
HGRN_COLS = MXU_COLS
HGRN_CHUNK = 128
HGRN_INTERLEAVE = 8


def _cparams(n_axes):
    return pltpu.CompilerParams(
        dimension_semantics=("arbitrary",) * n_axes, vmem_limit_bytes=VMEM_LIMIT)


def _resident(shape):
    return pl.BlockSpec(shape, lambda *_: (0,) * len(shape), pipeline_mode=pl.Buffered(1))


class _Stacked(NamedTuple):
    array: jax.Array
    lead: tuple = ()

    @property
    def shape(self):
        return self.array.shape[len(self.lead):]

    def spec(self):
        lead, tail = self.lead, self.shape
        return pl.BlockSpec((None,) * len(lead) + tail, lambda *_: lead + (0,) * len(tail),
                            pipeline_mode=pl.Buffered(1))


def _sigmoid(z):
    return 1.0 / (1.0 + jnp.exp(-z))


def _dot(a, b):
    return jnp.dot(a, b, preferred_element_type=F32)


def _dot_nt(a, b):
    return lax.dot_general(a, b, (((1,), (1,)), ((), ())), preferred_element_type=F32)


def _dot_tn(a, b):
    return lax.dot_general(a, b, (((0,), (0,)), ((), ())), preferred_element_type=F32)


def _pipelined(stages):
    pending = None
    for produce, consume in stages:
        value = produce()
        if pending is not None:
            pending[1](pending[0])
        pending = (value, consume)
    if pending is not None:
        pending[1](pending[0])


def _col_pieces(total, width):
    return [(lo, min(lo + width, total)) for lo in range(0, total, width)]


def _norm_mod(x, gain, mod_ref):
    ms = jnp.mean(x * x, axis=-1, keepdims=True)
    h = x * lax.rsqrt(ms + EPS) * gain
    return h * (1.0 + mod_ref[0, 1:2, :]) + mod_ref[0, 0:1, :]


def _ada_kernel(c_ref, w_ref, b_ref, o_ref):
    c = c_ref[...]
    ca = (c * _sigmoid(c)).astype(BF16)
    o_ref[0] = _dot(ca, w_ref[0].astype(BF16)) + b_ref[0]


def _ada_call(c, ada_w, ada_b):
    depth, d, n = ada_w.shape
    b = c.shape[0]
    tn = 1152
    assert n % tn == 0
    return pl.pallas_call(
        _ada_kernel,
        grid=(depth, n // tn),
        in_specs=[
            pl.BlockSpec((b, d), lambda l, j: (0, 0)),
            pl.BlockSpec((1, d, tn), lambda l, j: (l, 0, j)),
            pl.BlockSpec((1, 1, tn), lambda l, j: (l, 0, j)),
        ],
        out_specs=pl.BlockSpec((1, b, tn), lambda l, j: (l, 0, j)),
        out_shape=jax.ShapeDtypeStruct((depth, b, n), F32),
        compiler_params=_cparams(2),
        name="ada_mod",
    )(c, ada_w, ada_b.reshape(depth, 1, n))


def _ffn_kernel(*refs, d_ff, with_mixer):
    if with_mixer:
        (x_ref, mix_ref, mixmod_ref, wo_ref, mod_ref, g_ref, win_ref, wdn_ref, o_ref, act_ref) = refs
        x = x_ref[...] + mixmod_ref[0, 2:3, :] * _dot(mix_ref[...], wo_ref[...])
    else:
        (x_ref, mod_ref, g_ref, win_ref, wdn_ref, o_ref, act_ref) = refs
        x = x_ref[...]
    h = _norm_mod(x, g_ref[...], mod_ref).astype(BF16)

    def stage(lo, hi):
        def produce():
            return _dot(h, win_ref[:, lo:hi]), _dot(h, win_ref[:, d_ff + lo:d_ff + hi])

        def consume(ab):
            a, b = ab
            act_ref[:, lo:hi] = (a * _sigmoid(a) * b).astype(BF16)

        return produce, consume

    _pipelined([stage(lo, hi) for lo, hi in _col_pieces(d_ff, FFN_COLS)])
    y = _dot(act_ref[...], wdn_ref[...])
    o_ref[...] = x + 0.5 * mod_ref[0, 2:3, :] * y


def _ffn_call(x2, mod3, gain, w_in, w_down, seq, mixer=None):
    t, d = x2.shape
    d_ff = w_down.shape[0]
    tm = min(1024, seq)
    assert t % tm == 0 and seq % tm == 0 and d_ff % MXU_COLS == 0
    per_seq = seq // tm
    row = pl.BlockSpec((tm, d), lambda i: (i, 0))
    mod_spec = pl.BlockSpec((1, 3, d), lambda i: (i // per_seq, 0, 0))
    operands, in_specs = [x2], [row]
    if mixer is not None:
        mix, mix_mod3, w_o = mixer
        operands += [mix, mix_mod3, w_o.array]
        in_specs += [row, mod_spec, w_o.spec()]
    operands += [mod3, gain, w_in.array, w_down.array]
    in_specs += [mod_spec, _resident((1, d)), w_in.spec(), w_down.spec()]
    return pl.pallas_call(
        functools.partial(_ffn_kernel, d_ff=d_ff, with_mixer=mixer is not None),
        grid=(t // tm,),
        in_specs=in_specs,
        out_specs=row,
        out_shape=jax.ShapeDtypeStruct((t, d), F32),
        scratch_shapes=[pltpu.VMEM((tm, d_ff), BF16)],
        compiler_params=_cparams(1),
        name="ffn_half_step",
    )(*operands)


def _group_mean_sq(z, ones_blk):
    return _dot((z * z).astype(BF16), ones_blk)


def _qkv_kernel(x_ref, mod_ref, g_ref, w_ref, qg_ref, kg_ref, q_ref, k_ref, v_ref, *, d):
    h = _norm_mod(x_ref[...], g_ref[...], mod_ref).astype(BF16)
    tn = QKV_COLS
    r = lax.broadcasted_iota(jnp.int32, (tn, tn), 0) // ATTN_HEAD_DIM
    c = lax.broadcasted_iota(jnp.int32, (tn, tn), 1) // ATTN_HEAD_DIM
    ones_blk = jnp.where(r == c, 1.0 / ATTN_HEAD_DIM, 0.0).astype(BF16)
    q_gain = qg_ref[...] * (ATTN_HEAD_DIM ** -0.5 * LOG2E)
    k_gain = kg_ref[...]

    def normed(w_lo, out_ref, lo, hi, gain):
        def consume(z):
            out_ref[:, lo:hi] = (z * lax.rsqrt(_group_mean_sq(z, ones_blk) + EPS) * gain).astype(BF16)

        return (lambda: _dot(h, w_ref[:, w_lo + lo:w_lo + hi])), consume

    def plain(w_lo, out_ref, lo, hi):
        def consume(z):
            out_ref[:, lo:hi] = z.astype(BF16)

        return (lambda: _dot(h, w_ref[:, w_lo + lo:w_lo + hi])), consume

    stages = []
    for lo, hi in _col_pieces(d, tn):
        stages += [normed(0, q_ref, lo, hi, q_gain), normed(d, k_ref, lo, hi, k_gain),
                   plain(2 * d, v_ref, lo, hi)]
    _pipelined(stages)


def _qkv_call(x2, mod3, gain, w_qkv, q_gain2, k_gain2, seq):
    t, d = x2.shape
    tm = min(512, seq)
    assert t % tm == 0 and seq % tm == 0
    per_seq = seq // tm
    out = jax.ShapeDtypeStruct((t, d), BF16)
    row = pl.BlockSpec((tm, d), lambda i: (i, 0))
    return pl.pallas_call(
        functools.partial(_qkv_kernel, d=d),
        grid=(t // tm,),
        in_specs=[
            row,
            pl.BlockSpec((1, 3, d), lambda i: (i // per_seq, 0, 0)),
            _resident((1, d)),
            w_qkv.spec(),
            _resident((1, QKV_COLS)),
            _resident((1, QKV_COLS)),
        ],
        out_specs=[row, row, row],
        out_shape=[out, out, out],
        compiler_params=_cparams(1),
        name="attn_qkv_proj",
    )(x2, mod3, gain, w_qkv.array, q_gain2, k_gain2)


def _bucket_thresholds():
    max_exact = NUM_BUCKETS // 2
    dist = np.arange(0, 4 * MAX_DISTANCE)
    d_f = np.maximum(dist, 1).astype(np.float32)
    large = max_exact + (np.log(d_f / np.float32(max_exact)) / np.float32(math.log(MAX_DISTANCE / max_exact))
                         * np.float32(NUM_BUCKETS - max_exact)).astype(np.int32)
    bucket = np.where(dist < max_exact, dist, np.minimum(large, NUM_BUCKETS - 1))
    assert np.all(np.diff(bucket) >= 0) and bucket[-1] == NUM_BUCKETS - 1
    return [int(np.argmax(bucket >= b)) for b in range(NUM_BUCKETS)]


def _bias_kernel(rb_ref, o_ref, *, thresholds, tile):
    h = pl.program_id(0)
    i = lax.broadcasted_iota(jnp.int32, (tile, tile), 0)
    j = lax.broadcasted_iota(jnp.int32, (tile, tile), 1)
    for off in range(3):
        dist = off * tile + i - j
        val = jnp.full((tile, tile), rb_ref[0, h], F32)
        for b in range(1, NUM_BUCKETS):
            val = jnp.where(dist >= thresholds[b], rb_ref[b, h], val)
        o_ref[0, off] = jnp.where(dist >= 0, val * LOG2E, NEG)


def _bias_call(rel_bias, tile):
    thresholds = _bucket_thresholds()
    assert thresholds[NUM_BUCKETS - 1] <= tile + 1
    heads = rel_bias.shape[1]
    return pl.pallas_call(
        functools.partial(_bias_kernel, thresholds=thresholds, tile=tile),
        grid=(heads,),
        in_specs=[pl.BlockSpec(memory_space=pltpu.SMEM)],
        out_specs=pl.BlockSpec((1, 3, tile, tile), lambda h: (h, 0, 0, 0)),
        out_shape=jax.ShapeDtypeStruct((heads, 3, tile, tile), F32),
        compiler_params=_cparams(1),
        name="attn_rel_bias_tiles",
    )(rel_bias)


def _attn_kernel(q_ref, k_ref, v_ref, bias_ref, lam_ref, sg_ref, o_ref, v1_ref,
                 s0_ref, s1_ref, p0_ref, p1_ref, m0_ref, m1_ref, mf0_ref, mf1_ref,
                 *, tile, lambda_init):
    seq = q_ref.shape[0]
    dv = ATTN_V_DIM
    n_tiles = seq // tile
    s_refs, p_refs = (s0_ref, s1_ref), (p0_ref, p1_ref)
    m_refs, mfar_refs = (m0_ref, m1_ref), (mf0_ref, mf1_ref)
    v1_ref[:, :dv] = v_ref[...]
    v1_ref[:, dv:] = jnp.ones((seq, dv), BF16)
    lam = lam_ref[...]
    lam_full = (jnp.exp(jnp.sum(lam[0:1] * lam[1:2], axis=-1, keepdims=True))
                - jnp.exp(jnp.sum(lam[2:3] * lam[3:4], axis=-1, keepdims=True)) + lambda_init)
    lane = lax.broadcasted_iota(jnp.int32, (tile, dv), 1)

    def stacked_q(i):
        q = q_ref[i * tile:(i + 1) * tile, :]
        zero = jnp.zeros_like(q)
        return jnp.concatenate([jnp.where(lane < ATTN_HEAD_DIM, q, zero),
                                jnp.where(lane >= ATTN_HEAD_DIM, q, zero)], axis=0)

    far_bias = bias_ref[0, 2, 0:1, 0:1]
    running_max = {}

    def is_far(i, j):
        return i - j >= 2

    def scores(i, j, qs):
        s = _dot_nt(qs, k_ref[j * tile:(j + 1) * tile, :])
        if not is_far(i, j):
            bt = bias_ref[0, i - j]
            s = s + jnp.concatenate([bt, bt], axis=0)
        s_refs[i % 2][:, j * tile:(j + 1) * tile] = s
        block_max = functools.reduce(
            jnp.maximum, [s[:, c:c + LANES] for c in range(0, tile, LANES)])
        key = (i, is_far(i, j))
        running_max[key] = (jnp.maximum(running_max[key], block_max)
                            if key in running_max else block_max)

    def row_max(i):
        m = jnp.max(running_max.pop((i, False)), axis=-1, keepdims=True)
        if (i, True) in running_max:
            m_far = jnp.max(running_max.pop((i, True)), axis=-1, keepdims=True) + far_bias
            m = jnp.maximum(m, m_far)
            mfar_refs[i % 2][...] = jnp.broadcast_to(m - far_bias, mfar_refs[i % 2].shape)
        m_refs[i % 2][...] = jnp.broadcast_to(m, m_refs[i % 2].shape)

    def probs(i, j):
        m = (mfar_refs if is_far(i, j) else m_refs)[i % 2][...]
        m_wide = jnp.concatenate([m] * (tile // LANES), axis=1)
        cols = slice(j * tile, (j + 1) * tile)
        p_refs[i % 2][:, cols] = jnp.exp2(s_refs[i % 2][:, cols] - m_wide).astype(BF16)

    def finish(i):
        n_keys = (i + 1) * tile
        acc = _dot(p_refs[i % 2][:, :n_keys], v1_ref[:n_keys, :])
        o_all = acc[:, :dv] / acc[:, dv:]
        o = o_all[:tile] - lam_full * o_all[tile:]
        ms = jnp.mean(o * o, axis=-1, keepdims=True)
        o = o * lax.rsqrt(ms + EPS) * sg_ref[...] * (1.0 - lambda_init)
        o_ref[i * tile:(i + 1) * tile, :] = o.astype(BF16)

    def score_and_prob_steps(i_scores, i_probs):
        qs = stacked_q(i_scores) if i_scores >= 0 else None
        for j in range(max(i_scores, i_probs) + 1):
            if j <= i_scores:
                scores(i_scores, j, qs)
            if j <= i_probs:
                probs(i_probs, j)
        if i_scores >= 0:
            row_max(i_scores)

    last = n_tiles - 1
    score_and_prob_steps(last, -1)
    score_and_prob_steps(last - 1, last)
    for i in range(last, -1, -1):
        finish(i)
        score_and_prob_steps(i - 2, i - 1)


def _attn_call(q2, k2, v2, bias_tiles, lam, subln_gain2, batch, seq, lambda_init):
    t, d = q2.shape
    tile = ATTN_TILE
    heads = d // ATTN_V_DIM
    assert seq % tile == 0
    blk = pl.BlockSpec((seq, ATTN_V_DIM), lambda b, h: (b, h))
    return pl.pallas_call(
        functools.partial(_attn_kernel, tile=tile, lambda_init=lambda_init),
        grid=(batch, heads),
        in_specs=[
            blk, blk, blk,
            pl.BlockSpec((1, 3, tile, tile), lambda b, h: (h, 0, 0, 0)),
            pl.BlockSpec(lam.shape, lambda b, h: (0, 0)),
            pl.BlockSpec((1, ATTN_V_DIM), lambda b, h: (0, 0)),
        ],
        out_specs=blk,
        out_shape=jax.ShapeDtypeStruct((t, d), BF16),
        scratch_shapes=[
            pltpu.VMEM((seq, 2 * ATTN_V_DIM), BF16),
            pltpu.VMEM((2 * tile, seq), F32),
            pltpu.VMEM((2 * tile, seq), F32),
            pltpu.VMEM((2 * tile, seq), BF16),
            pltpu.VMEM((2 * tile, seq), BF16),
            pltpu.VMEM((2 * tile, LANES), F32),
            pltpu.VMEM((2 * tile, LANES), F32),
            pltpu.VMEM((2 * tile, LANES), F32),
            pltpu.VMEM((2 * tile, LANES), F32),
        ],
        compiler_params=_cparams(2),
        name="diff_attention",
    )(q2, k2, v2, bias_tiles, lam, subln_gain2)


def _hgrn_in_kernel(x_ref, mod_ref, g_ref, w_ref, lbl_ref, q_ref, k_ref, lf_ref, v_ref, og_ref,
                    *, d, layer_j):
    h = _norm_mod(x_ref[...], g_ref[...], mod_ref).astype(BF16)
    logits = lbl_ref[...]
    e = jnp.exp(logits - jnp.max(logits, axis=0, keepdims=True))
    p = e / jnp.sum(e, axis=0, keepdims=True)
    lb = jnp.zeros((1, d), F32)
    for i in range(1, layer_j + 1):
        lb = lb + p[i:i + 1, :]
    log_lb = jnp.log(lb)
    log_1m_lb = jnp.log1p(-lb)

    def dot_cols(w_lo, lo, hi):
        return lambda: _dot(h, w_ref[:, w_lo + lo:w_lo + hi])

    def forget_stage(lo, hi):
        def consume(f):
            ef = jnp.exp(-jnp.abs(f))
            one_plus = 1.0 + ef
            log_sig = jnp.minimum(f, 0.0) - jnp.log(one_plus)
            sig_neg = jnp.where(f >= 0.0, ef, 1.0) / one_plus
            if layer_j == 0:
                lf_ref[:, lo:hi] = log_sig
                k_ref[:, lo:hi] = sig_neg
            else:
                a = log_lb[:, lo:hi]
                b = log_1m_lb[:, lo:hi] + log_sig
                lf_ref[:, lo:hi] = jnp.maximum(a, b) + jnp.log(1.0 + jnp.exp(-jnp.abs(a - b)))
                k_ref[:, lo:hi] = (1.0 - lb[:, lo:hi]) * sig_neg

        return dot_cols(d, lo, hi), consume

    def query_stage(lo, hi):
        def consume(z):
            q_ref[:, lo:hi] = z

        return dot_cols(0, lo, hi), consume

    def value_stage(lo, hi):
        def consume(z):
            v_ref[:, lo:hi] = z.astype(BF16)

        return dot_cols(2 * d, lo, hi), consume

    def gate_stage(lo, hi):
        def consume(g):
            og_ref[:, lo:hi] = g * _sigmoid(g)

        return dot_cols(3 * d, lo, hi), consume

    stages = []
    for lo, hi in _col_pieces(d, HGRN_COLS):
        stages += [forget_stage(lo, hi), query_stage(lo, hi), gate_stage(lo, hi), value_stage(lo, hi)]
    _pipelined(stages)


def _hgrn_in_call(x2, mod3, gain, w_in, lb_logits, layer_j, seq):
    t, d = x2.shape
    tm = min(512, seq)
    assert t % tm == 0 and seq % tm == 0
    per_seq = seq // tm
    row = pl.BlockSpec((tm, d), lambda i: (i, 0))
    f32_out = jax.ShapeDtypeStruct((t, d), F32)
    return pl.pallas_call(
        functools.partial(_hgrn_in_kernel, d=d, layer_j=layer_j),
        grid=(t // tm,),
        in_specs=[
            row,
            pl.BlockSpec((1, 3, d), lambda i: (i // per_seq, 0, 0)),
            _resident((1, d)),
            w_in.spec(),
            _resident(lb_logits.shape),
        ],
        out_specs=[row, row, row, row, row],
        out_shape=[f32_out, f32_out, f32_out, jax.ShapeDtypeStruct((t, d), BF16), f32_out],
        compiler_params=_cparams(1),
        name="hgrn_in_proj",
    )(x2, mod3, gain, w_in.array, lb_logits)


def _rows_at(g_cum, half):
    c = g_cum.shape[0]
    if half >= 4:
        blk = 2 * half
        return jnp.concatenate(
            [jnp.broadcast_to(g_cum[p * blk + half - 1:p * blk + half, :], (blk, g_cum.shape[1]))
             for p in range(c // blk)], axis=0)
    row = lax.broadcasted_iota(jnp.int32, g_cum.shape, 0)
    if half == 2:
        r4 = row % 4
        up1 = pltpu.roll(g_cum, c - 1, 0)
        dn1 = pltpu.roll(g_cum, 1, 0)
        dn2 = pltpu.roll(g_cum, 2, 0)
        return jnp.where(r4 == 0, up1, jnp.where(r4 == 1, g_cum, jnp.where(r4 == 2, dn1, dn2)))
    assert half == 1
    return jnp.where(row % 2 == 0, g_cum, pltpu.roll(g_cum, 1, 0))


def _round_robin(generators, width):
    pending = list(generators)
    active = []
    while pending or active:
        while pending and len(active) < width:
            active.append(pending.pop(0))
        for gen in list(active):
            try:
                next(gen)
            except StopIteration:
                active.remove(gen)


def _hgrn_rec_kernel(q_ref, k_ref, lf_ref, v_ref, og_ref, gain_ref, o_ref,
                     qe_ref, a_ref, u_ref, dl_ref, st_ref, *, chunk):
    seq = q_ref.shape[0]
    c = chunk
    n_chunks = seq // c
    ri = lax.broadcasted_iota(jnp.int32, (c, c), 0)
    ci = lax.broadcasted_iota(jnp.int32, (c, c), 1)
    tri = jnp.where(ri >= ci, 1.0, 0.0).astype(BF16)
    pair_key = jnp.where(ri >= ci, ri ^ ci, -1)
    row = lax.broadcasted_iota(jnp.int32, (c, HGRN_DK), 0)

    def neg_abs(z):
        return lax.bitcast_convert_type(
            lax.bitcast_convert_type(z, jnp.int32) | jnp.int32(-2 ** 31), F32)

    def intra(n):
        sl = slice(n * c, (n + 1) * c)
        g = lf_ref[sl, :]
        q = q_ref[sl, :]
        k = k_ref[sl, :]
        g_hi = g.astype(BF16)
        r1 = g - g_hi.astype(F32)
        g_mid = r1.astype(BF16)
        g_lo = (r1 - g_mid.astype(F32)).astype(BF16)
        g2 = (_dot(tri, g_hi) + (_dot(tri, g_mid) + _dot(tri, g_lo))) * LOG2E
        g2_last = g2[c - 1:c, :]
        yield
        qe_ref[sl, :] = (q * jnp.exp2(g2)).astype(BF16)
        k_dec = (k * jnp.exp2(g2_last - g2)).astype(BF16)
        u_ref[n] = _dot_tn(v_ref[sl, :], k_dec)
        dl_ref[n] = jnp.exp2(g2_last)
        scores = jnp.where(pair_key == 0, _dot_nt(q.astype(BF16), k.astype(BF16)), 0.0)
        yield
        half = 1
        while half < c:
            e = jnp.exp2(neg_abs(g2 - _rows_at(g2, half)))
            if half < 8:
                level = _dot_nt((q * e).astype(BF16), (k * e).astype(BF16))
                scores = jnp.where(pair_key >= half, level, scores)
            else:
                qk = jnp.concatenate(
                    [(q if (b % 2) else k)[b * half:(b + 1) * half] for b in range(c // half)], axis=0)
                zf = qk * e
                z = zf.astype(BF16)
                z_right = jnp.concatenate(
                    [zf[b * half:(b + 1) * half] for b in range(1, c // half, 2)], axis=0).astype(BF16)
                level = _dot_nt(z_right, z)
                pieces = []
                for b in range(c // half):
                    rows = slice(b * half, (b + 1) * half)
                    piece = scores[rows]
                    if b % 2:
                        lv = level[(b // 2) * half:(b // 2 + 1) * half]
                        piece = jnp.where(pair_key[rows] >= half, lv, piece)
                    pieces.append(piece)
                scores = jnp.concatenate(pieces, axis=0)
            half *= 2
            yield
        a_ref[n] = scores.astype(BF16)

    def outputs(n):
        sl = slice(n * c, (n + 1) * c)
        o = _dot_nt(qe_ref[sl, :], st_ref[n]) + _dot(a_ref[n], v_ref[sl, :])
        yield
        ms = jnp.mean(o * o, axis=-1, keepdims=True)
        o_ref[sl, :] = (o * lax.rsqrt(ms + EPS) * gain_ref[...] * og_ref[sl, :]).astype(BF16)

    _round_robin([intra(n) for n in range(n_chunks)], HGRN_INTERLEAVE)

    state = jnp.zeros((HGRN_DV, HGRN_DK), F32)
    for n in range(n_chunks):
        st_ref[n] = state.astype(BF16)
        state = dl_ref[n] * state + u_ref[n]

    _round_robin([outputs(n) for n in range(n_chunks)], HGRN_INTERLEAVE)


def _hgrn_rec_call(q2, k2, lf2, v2, og2, out_gain2, batch, seq):
    t, d = q2.shape
    heads = d // HGRN_DV
    c = HGRN_CHUNK
    assert seq % c == 0
    n_chunks = seq // c
    blk = pl.BlockSpec((seq, HGRN_DV), lambda b, h: (b, h))
    return pl.pallas_call(
        functools.partial(_hgrn_rec_kernel, chunk=c),
        grid=(batch, heads),
        in_specs=[blk, blk, blk, blk, blk, pl.BlockSpec((1, HGRN_DV), lambda b, h: (0, h))],
        out_specs=blk,
        out_shape=jax.ShapeDtypeStruct((t, d), BF16),
        scratch_shapes=[
            pltpu.VMEM((seq, HGRN_DK), BF16),
            pltpu.VMEM((n_chunks, c, c), BF16),
            pltpu.VMEM((n_chunks, HGRN_DV, HGRN_DK), F32),
            pltpu.VMEM((n_chunks, 1, HGRN_DK), F32),
            pltpu.VMEM((n_chunks, HGRN_DV, HGRN_DK), BF16),
        ],
        compiler_params=_cparams(2),
        name="hgrn_recurrence",
    )(q2, k2, lf2, v2, og2, out_gain2)


def kernel(x, c, ada_w, ada_b, norm_g, ffn_w_in, ffn_w_down, attn_w_qkv, attn_w_o, attn_q_gain,
           attn_k_gain, attn_lambda, attn_subln_gain, rel_bias, hgrn_w_in, hgrn_w_o,
           hgrn_out_gain, hgrn_lb_logits):
    batch, seq, d = x.shape
    depth = ada_w.shape[0]
    mod = _ada_call(c, ada_w, ada_b).reshape(depth, batch, N_SUBLAYERS, 3, d)
    bias_tiles = _bias_call(rel_bias, ATTN_TILE)
    x2 = x.reshape(batch * seq, d)
    ffn_in_b, ffn_down_b = ffn_w_in.astype(BF16), ffn_w_down.astype(BF16)
    qkv_b, attn_o_b = attn_w_qkv.astype(BF16), attn_w_o.astype(BF16)
    hgrn_in_b, hgrn_o_b = hgrn_w_in.astype(BF16), hgrn_w_o.astype(BF16)
    for layer in range(depth):
        gains = norm_g[layer].reshape(N_SUBLAYERS, 1, d)
        x2 = _ffn_call(x2, mod[layer, :, 0], gains[0], _Stacked(ffn_in_b, (layer, 0)),
                       _Stacked(ffn_down_b, (layer, 0)), seq)
        j = layer // N_MIXERS
        if layer % N_MIXERS == 0:
            reps = QKV_COLS // ATTN_HEAD_DIM
            q2, k2, v2 = _qkv_call(
                x2, mod[layer, :, 1], gains[1], _Stacked(qkv_b, (j,)),
                jnp.tile(attn_q_gain[j], reps).reshape(1, QKV_COLS),
                jnp.tile(attn_k_gain[j], reps).reshape(1, QKV_COLS), seq)
            lambda_init = 0.8 - 0.6 * math.exp(-0.3 * layer)
            o2 = _attn_call(q2, k2, v2, bias_tiles, attn_lambda[j],
                            attn_subln_gain[j].reshape(1, ATTN_V_DIM), batch, seq, lambda_init)
            w_o = _Stacked(attn_o_b, (j,))
        else:
            q2, k2, lf2, v2, og2 = _hgrn_in_call(
                x2, mod[layer, :, 1], gains[1], _Stacked(hgrn_in_b, (j,)), hgrn_lb_logits, j, seq)
            o2 = _hgrn_rec_call(q2, k2, lf2, v2, og2, hgrn_out_gain[j].reshape(1, d), batch, seq)
            w_o = _Stacked(hgrn_o_b, (j,))
        x2 = _ffn_call(x2, mod[layer, :, 2], gains[2], _Stacked(ffn_in_b, (layer, 1)),
                       _Stacked(ffn_down_b, (layer, 1)), seq, mixer=(o2, mod[layer, :, 1], w_o))
    return x2.reshape(batch, seq, d)
```

```python
import functools
import math
from typing import NamedTuple

import numpy as np
import jax
import jax.numpy as jnp
from jax import lax
from jax.experimental import pallas as pl
from jax.experimental.pallas import tpu as pltpu

F32 = jnp.float32
BF16 = jnp.bfloat16

EPS = 1e-6
N_SUBLAYERS = 3
N_MIXERS = 2
ATTN_HEADS = 8
ATTN_HEAD_DIM = 64
ATTN_V_DIM = 128
NUM_BUCKETS = 32
MAX_DISTANCE = 128
HGRN_HEADS = 8
HGRN_DK = 128
HGRN_DV = 128

LANES = 128
VMEM_LIMIT = 56 * 1024 * 1024
NEG = -1e30
LOG2E = math.log2(math.e)

MXU_COLS = 256
ATTN_TILE = 256
QKV_COLS = MXU_COLS
FFN_COLS = 2 * MXU_COLS
FFN_STAGE_ROWS_IN = 32
FFN_STAGE_ROWS_DOWN = 128
HGRN_COLS = MXU_COLS
HGRN_CHUNK = 128
HGRN_INTERLEAVE = 8


def _cparams(n_axes):
    return pltpu.CompilerParams(
        dimension_semantics=("arbitrary",) * n_axes, vmem_limit_bytes=VMEM_LIMIT)


def _resident(shape):
    return pl.BlockSpec(shape, lambda *_: (0,) * len(shape), pipeline_mode=pl.Buffered(1))


class _Stacked(NamedTuple):
    array: jax.Array
    lead: tuple = ()

    @property
    def shape(self):
        return self.array.shape[len(self.lead):]

    def spec(self):
        lead, tail = self.lead, self.shape
        return pl.BlockSpec((None,) * len(lead) + tail, lambda *_: lead + (0,) * len(tail),
                            pipeline_mode=pl.Buffered(1))


def _sigmoid(z):
    return 1.0 / (1.0 + jnp.exp(-z))


def _dot(a, b):
    return jnp.dot(a, b, preferred_element_type=F32)


def _dot_nt(a, b):
    return lax.dot_general(a, b, (((1,), (1,)), ((), ())), preferred_element_type=F32)


def _dot_tn(a, b):
    return lax.dot_general(a, b, (((0,), (0,)), ((), ())), preferred_element_type=F32)


def _pipelined(stages):
    pending = None
    for produce, consume in stages:
        value = produce()
        if pending is not None:
            pending[1](pending[0])
        pending = (value, consume)
    if pending is not None:
        pending[1](pending[0])


def _col_pieces(total, width):
    return [(lo, min(lo + width, total)) for lo in range(0, total, width)]


def _norm_mod(x, gain, mod_ref):
    ms = jnp.mean(x * x, axis=-1, keepdims=True)
    h = x * lax.rsqrt(ms + EPS) * gain
    return h * (1.0 + mod_ref[0, 1:2, :]) + mod_ref[0, 0:1, :]


def _ada_kernel(c_ref, w_ref, b_ref, o_ref):
    c = c_ref[...]
    ca = (c * _sigmoid(c)).astype(BF16)
    o_ref[0] = _dot(ca, w_ref[0].astype(BF16)) + b_ref[0]


def _ada_call(c, ada_w, ada_b):
    depth, d, n = ada_w.shape
    b = c.shape[0]
    tn = 1152
    assert n % tn == 0
    return pl.pallas_call(
        _ada_kernel,
        grid=(depth, n // tn),
        in_specs=[
            pl.BlockSpec((b, d), lambda l, j: (0, 0)),
            pl.BlockSpec((1, d, tn), lambda l, j: (l, 0, j)),
            pl.BlockSpec((1, 1, tn), lambda l, j: (l, 0, j)),
        ],
        out_specs=pl.BlockSpec((1, b, tn), lambda l, j: (l, 0, j)),
        out_shape=jax.ShapeDtypeStruct((depth, b, n), F32),
        compiler_params=_cparams(2),
        name="ada_mod",
    )(c, ada_w, ada_b.reshape(depth, 1, n))


def _fetch_bf16(w_hbm, lead, dst_ref, stage_ref, sem_ref):
    rows = stage_ref.shape[1]
    n_rows = dst_ref.shape[0]
    assert n_rows % rows == 0
    n_chunks = n_rows // rows

    def copy(c):
        src = w_hbm.at[lead + (pl.ds(c * rows, rows),)]
        return pltpu.make_async_copy(src, stage_ref.at[c % 2], sem_ref.at[c % 2])

    copy(0).start()
    for c in range(n_chunks):
        if c + 1 < n_chunks:
            copy(c + 1).start()
        copy(c).wait()
        dst_ref[c * rows:(c + 1) * rows, :] = stage_ref[c % 2].astype(BF16)


def _ffn_kernel(*refs, d_ff, with_mixer, leads):
    if with_mixer:
        (x_ref, mix_ref, mixmod_ref, mod_ref, g_ref, win_hbm, wdn_hbm, wo_hbm, o_ref,
         act_ref, win_ref, wdn_ref, stage_in_ref, stage_dn_ref, sem_in_ref, sem_dn_ref, wo_ref) = refs
    else:
        (x_ref, mod_ref, g_ref, win_hbm, wdn_hbm, o_ref,
         act_ref, win_ref, wdn_ref, stage_in_ref, stage_dn_ref, sem_in_ref, sem_dn_ref) = refs

    @pl.when(pl.program_id(0) == 0)
    def _():
        _fetch_bf16(win_hbm, leads[0], win_ref, stage_in_ref, sem_in_ref)
        _fetch_bf16(wdn_hbm, leads[1], wdn_ref, stage_dn_ref, sem_dn_ref)
        if with_mixer:
            _fetch_bf16(wo_hbm, leads[2], wo_ref, stage_dn_ref, sem_dn_ref)

    if with_mixer:
        x = x_ref[...] + mixmod_ref[0, 2:3, :] * _dot(mix_ref[...], wo_ref[...])
    else:
        x = x_ref[...]
    h = _norm_mod(x, g_ref[...], mod_ref).astype(BF16)

    def stage(lo, hi):
        def produce():
            return _dot(h, win_ref[:, lo:hi]), _dot(h, win_ref[:, d_ff + lo:d_ff + hi])

        def consume(ab):
            a, b = ab
            act_ref[:, lo:hi] = (a * _sigmoid(a) * b).astype(BF16)

        return produce, consume

    _pipelined([stage(lo, hi) for lo, hi in _col_pieces(d_ff, FFN_COLS)])
    y = _dot(act_ref[...], wdn_ref[...])
    o_ref[...] = x + 0.5 * mod_ref[0, 2:3, :] * y


def _ffn_call(x2, mod3, gain, w_in, w_down, seq, mixer=None):
    t, d = x2.shape
    d_ff = w_down.shape[0]
    tm = min(1024, seq)
    assert t % tm == 0 and seq % tm == 0 and d_ff % MXU_COLS == 0
    per_seq = seq // tm
    row = pl.BlockSpec((tm, d), lambda i: (i, 0))
    mod_spec = pl.BlockSpec((1, 3, d), lambda i: (i // per_seq, 0, 0))
    in_hbm = pl.BlockSpec(memory_space=pl.ANY)
    operands, in_specs = [x2], [row]
    leads = (w_in.lead, w_down.lead)
    if mixer is not None:
        mix, mix_mod3, w_o = mixer
        operands += [mix, mix_mod3]
        in_specs += [row, mod_spec]
        leads += (w_o.lead,)
    operands += [mod3, gain, w_in.array, w_down.array]
    in_specs += [mod_spec, _resident((1, d)), in_hbm, in_hbm]
    scratch = [
        pltpu.VMEM((tm, d_ff), BF16),
        pltpu.VMEM((d, 2 * d_ff), BF16),
        pltpu.VMEM((d_ff, d), BF16),
        pltpu.VMEM((2, FFN_STAGE_ROWS_IN, 2 * d_ff), F32),
        pltpu.VMEM((2, FFN_STAGE_ROWS_DOWN, d), F32),
        pltpu.SemaphoreType.DMA((2,)),
        pltpu.SemaphoreType.DMA((2,)),
    ]
    if mixer is not None:
        operands += [w_o.array]
        in_specs += [in_hbm]
        scratch += [pltpu.VMEM((d, d), BF16)]
    return pl.pallas_call(
        functools.partial(_ffn_kernel, d_ff=d_ff, with_mixer=mixer is not None, leads=leads),
        grid=(t // tm,),
        in_specs=in_specs,
        out_specs=row,
        out_shape=jax.ShapeDtypeStruct((t, d), F32),
        scratch_shapes=scratch,
        compiler_params=_cparams(1),
        name="ffn_half_step",
    )(*operands)


def _group_mean_sq(z, ones_blk):
    return _dot((z * z).astype(BF16), ones_blk)


def _qkv_kernel(x_ref, mod_ref, g_ref, w_ref, qg_ref, kg_ref, q_ref, k_ref, v_ref, *, d):
    h = _norm_mod(x_ref[...], g_ref[...], mod_ref).astype(BF16)
    tn = QKV_COLS
    r = lax.broadcasted_iota(jnp.int32, (tn, tn), 0) // ATTN_HEAD_DIM
    c = lax.broadcasted_iota(jnp.int32, (tn, tn), 1) // ATTN_HEAD_DIM
    ones_blk = jnp.where(r == c, 1.0 / ATTN_HEAD_DIM, 0.0).astype(BF16)
    q_gain = qg_ref[...] * (ATTN_HEAD_DIM ** -0.5 * LOG2E)
    k_gain = kg_ref[...]

    def normed(w_lo, out_ref, lo, hi, gain):
        def consume(z):
            out_ref[:, lo:hi] = (z * lax.rsqrt(_group_mean_sq(z, ones_blk) + EPS) * gain).astype(BF16)

        return (lambda: _dot(h, w_ref[:, w_lo + lo:w_lo + hi])), consume

    def plain(w_lo, out_ref, lo, hi):
        def consume(z):
            out_ref[:, lo:hi] = z.astype(BF16)

        return (lambda: _dot(h, w_ref[:, w_lo + lo:w_lo + hi])), consume

    stages = []
    for lo, hi in _col_pieces(d, tn):
        stages += [normed(0, q_ref, lo, hi, q_gain), normed(d, k_ref, lo, hi, k_gain),
                   plain(2 * d, v_ref, lo, hi)]
    _pipelined(stages)


def _qkv_call(x2, mod3, gain, w_qkv, q_gain2, k_gain2, seq):
    t, d = x2.shape
    tm = min(512, seq)
    assert t % tm == 0 and seq % tm == 0
    per_seq = seq // tm
    out = jax.ShapeDtypeStruct((t, d), BF16)
    row = pl.BlockSpec((tm, d), lambda i: (i, 0))
    return pl.pallas_call(
        functools.partial(_qkv_kernel, d=d),
        grid=(t // tm,),
        in_specs=[
            row,
            pl.BlockSpec((1, 3, d), lambda i: (i // per_seq, 0, 0)),
            _resident((1, d)),
            w_qkv.spec(),
            _resident((1, QKV_COLS)),
            _resident((1, QKV_COLS)),
        ],
        out_specs=[row, row, row],
        out_shape=[out, out, out],
        compiler_params=_cparams(1),
        name="attn_qkv_proj",
    )(x2, mod3, gain, w_qkv.array, q_gain2, k_gain2)


def _bucket_thresholds():
    max_exact = NUM_BUCKETS // 2
    dist = np.arange(0, 4 * MAX_DISTANCE)
    d_f = np.maximum(dist, 1).astype(np.float32)
    large = max_exact + (np.log(d_f / np.float32(max_exact)) / np.float32(math.log(MAX_DISTANCE / max_exact))
                         * np.float32(NUM_BUCKETS - max_exact)).astype(np.int32)
    bucket = np.where(dist < max_exact, dist, np.minimum(large, NUM_BUCKETS - 1))
    assert np.all(np.diff(bucket) >= 0) and bucket[-1] == NUM_BUCKETS - 1
    return [int(np.argmax(bucket >= b)) for b in range(NUM_BUCKETS)]


def _bias_kernel(rb_ref, o_ref, *, thresholds, tile):
    h = pl.program_id(0)
    i = lax.broadcasted_iota(jnp.int32, (tile, tile), 0)
    j = lax.broadcasted_iota(jnp.int32, (tile, tile), 1)
    for off in range(3):
        dist = off * tile + i - j
        val = jnp.full((tile, tile), rb_ref[0, h], F32)
        for b in range(1, NUM_BUCKETS):
            val = jnp.where(dist >= thresholds[b], rb_ref[b, h], val)
        o_ref[0, off] = jnp.where(dist >= 0, val * LOG2E, NEG)


def _bias_call(rel_bias, tile):
    thresholds = _bucket_thresholds()
    assert thresholds[NUM_BUCKETS - 1] <= tile + 1
    heads = rel_bias.shape[1]
    return pl.pallas_call(
        functools.partial(_bias_kernel, thresholds=thresholds, tile=tile),
        grid=(heads,),
        in_specs=[pl.BlockSpec(memory_space=pltpu.SMEM)],
        out_specs=pl.BlockSpec((1, 3, tile, tile), lambda h: (h, 0, 0, 0)),
        out_shape=jax.ShapeDtypeStruct((heads, 3, tile, tile), F32),
        compiler_params=_cparams(1),
        name="attn_rel_bias_tiles",
    )(rel_bias)


def _attn_kernel(q_ref, k_ref, v_ref, bias_ref, lam_ref, sg_ref, o_ref, v1_ref, kt_ref,
                 s0_ref, s1_ref, p0_ref, p1_ref, m0_ref, m1_ref, mf0_ref, mf1_ref,
                 *, tile, lambda_init):
    seq = q_ref.shape[0]
    dv = ATTN_V_DIM
    n_tiles = seq // tile
    s_refs, p_refs = (s0_ref, s1_ref), (p0_ref, p1_ref)
    m_refs, mfar_refs = (m0_ref, m1_ref), (mf0_ref, mf1_ref)
    v1_ref[:, :dv] = v_ref[...]
    v1_ref[:, dv:] = jnp.ones((seq, dv), BF16)
    kt_ref[...] = k_ref[...].T
    lam = lam_ref[...]
    lam_full = (jnp.exp(jnp.sum(lam[0:1] * lam[1:2], axis=-1, keepdims=True))
                - jnp.exp(jnp.sum(lam[2:3] * lam[3:4], axis=-1, keepdims=True)) + lambda_init)
    lane = lax.broadcasted_iota(jnp.int32, (tile, dv), 1)

    def stacked_q(i):
        q = q_ref[i * tile:(i + 1) * tile, :]
        zero = jnp.zeros_like(q)
        return jnp.concatenate([jnp.where(lane < ATTN_HEAD_DIM, q, zero),
                                jnp.where(lane >= ATTN_HEAD_DIM, q, zero)], axis=0)

    far_bias = bias_ref[0, 2, 0:1, 0:1]
    running_max = {}

    def is_far(i, j):
        return i - j >= 2

    def scores(i, j, qs):
        s = _dot(qs, kt_ref[:, j * tile:(j + 1) * tile])
        if not is_far(i, j):
            bt = bias_ref[0, i - j]
            s = s + jnp.concatenate([bt, bt], axis=0)
        s_refs[i % 2][:, j * tile:(j + 1) * tile] = s
        block_max = functools.reduce(
            jnp.maximum, [s[:, c:c + LANES] for c in range(0, tile, LANES)])
        key = (i, is_far(i, j))
        running_max[key] = (jnp.maximum(running_max[key], block_max)
                            if key in running_max else block_max)

    def row_max(i):
        m = jnp.max(running_max.pop((i, False)), axis=-1, keepdims=True)
        if (i, True) in running_max:
            m_far = jnp.max(running_max.pop((i, True)), axis=-1, keepdims=True) + far_bias
            m = jnp.maximum(m, m_far)
            mfar_refs[i % 2][...] = jnp.broadcast_to(m - far_bias, mfar_refs[i % 2].shape)
        m_refs[i % 2][...] = jnp.broadcast_to(m, m_refs[i % 2].shape)

    def probs(i, j):
        m = (mfar_refs if is_far(i, j) else m_refs)[i % 2][...]
        m_wide = jnp.concatenate([m] * (tile // LANES), axis=1)
        cols = slice(j * tile, (j + 1) * tile)
        p_refs[i % 2][:, cols] = jnp.exp2(s_refs[i % 2][:, cols] - m_wide).astype(BF16)

    def finish(i):
        n_keys = (i + 1) * tile
        acc = _dot(p_refs[i % 2][:, :n_keys], v1_ref[:n_keys, :])
        o_all = acc[:, :dv] / acc[:, dv:]
        o = o_all[:tile] - lam_full * o_all[tile:]
        ms = jnp.mean(o * o, axis=-1, keepdims=True)
        o = o * lax.rsqrt(ms + EPS) * sg_ref[...] * (1.0 - lambda_init)
        o_ref[i * tile:(i + 1) * tile, :] = o.astype(BF16)

    def score_and_prob_steps(i_scores, i_probs):
        qs = stacked_q(i_scores) if i_scores >= 0 else None
        for j in range(max(i_scores, i_probs) + 1):
            if j <= i_scores:
                scores(i_scores, j, qs)
            if j <= i_probs:
                probs(i_probs, j)
        if i_scores >= 0:
            row_max(i_scores)

    last = n_tiles - 1
    score_and_prob_steps(last, -1)
    score_and_prob_steps(last - 1, last)
    for i in range(last, -1, -1):
        finish(i)
        score_and_prob_steps(i - 2, i - 1)


def _attn_call(q2, k2, v2, bias_tiles, lam, subln_gain2, batch, seq, lambda_init):
    t, d = q2.shape
    tile = ATTN_TILE
    heads = d // ATTN_V_DIM
    assert seq % tile == 0
    blk = pl.BlockSpec((seq, ATTN_V_DIM), lambda b, h: (b, h))
    return pl.pallas_call(
        functools.partial(_attn_kernel, tile=tile, lambda_init=lambda_init),
        grid=(batch, heads),
        in_specs=[
            blk, blk, blk,
            pl.BlockSpec((1, 3, tile, tile), lambda b, h: (h, 0, 0, 0)),
            pl.BlockSpec(lam.shape, lambda b, h: (0, 0)),
            pl.BlockSpec((1, ATTN_V_DIM), lambda b, h: (0, 0)),
        ],
        out_specs=blk,
        out_shape=jax.ShapeDtypeStruct((t, d), BF16),
        scratch_shapes=[
            pltpu.VMEM((seq, 2 * ATTN_V_DIM), BF16),
            pltpu.VMEM((ATTN_V_DIM, seq), BF16),
            pltpu.VMEM((2 * tile, seq), F32),
            pltpu.VMEM((2 * tile, seq), F32),
            pltpu.VMEM((2 * tile, seq), BF16),
            pltpu.VMEM((2 * tile, seq), BF16),
            pltpu.VMEM((2 * tile, LANES), F32),
            pltpu.VMEM((2 * tile, LANES), F32),
            pltpu.VMEM((2 * tile, LANES), F32),
            pltpu.VMEM((2 * tile, LANES), F32),
        ],
        compiler_params=_cparams(2),
        name="diff_attention",
    )(q2, k2, v2, bias_tiles, lam, subln_gain2)


def _hgrn_in_kernel(x_ref, mod_ref, g_ref, w_ref, lbl_ref, q_ref, k_ref, lf_ref, v_ref, og_ref,
                    *, d, layer_j):
    h = _norm_mod(x_ref[...], g_ref[...], mod_ref).astype(BF16)
    logits = lbl_ref[...]
    e = jnp.exp(logits - jnp.max(logits, axis=0, keepdims=True))
    p = e / jnp.sum(e, axis=0, keepdims=True)
    lb = jnp.zeros((1, d), F32)
    for i in range(1, layer_j + 1):
        lb = lb + p[i:i + 1, :]
    log_lb = jnp.log(lb)
    log_1m_lb = jnp.log1p(-lb)

    def dot_cols(w_lo, lo, hi):
        return lambda: _dot(h, w_ref[:, w_lo + lo:w_lo + hi])

    def forget_stage(lo, hi):
        def consume(f):
            ef = jnp.exp(-jnp.abs(f))
            one_plus = 1.0 + ef
            log_sig = jnp.minimum(f, 0.0) - jnp.log(one_plus)
            sig_neg = jnp.where(f >= 0.0, ef, 1.0) / one_plus
            if layer_j == 0:
                lf_ref[:, lo:hi] = log_sig
                k_ref[:, lo:hi] = sig_neg
            else:
                a = log_lb[:, lo:hi]
                b = log_1m_lb[:, lo:hi] + log_sig
                lf_ref[:, lo:hi] = jnp.maximum(a, b) + jnp.log(1.0 + jnp.exp(-jnp.abs(a - b)))
                k_ref[:, lo:hi] = (1.0 - lb[:, lo:hi]) * sig_neg

        return dot_cols(d, lo, hi), consume

    def query_stage(lo, hi):
        def consume(z):
            q_ref[:, lo:hi] = z

        return dot_cols(0, lo, hi), consume

    def value_stage(lo, hi):
        def consume(z):
            v_ref[:, lo:hi] = z.astype(BF16)

        return dot_cols(2 * d, lo, hi), consume

    def gate_stage(lo, hi):
        def consume(g):
            og_ref[:, lo:hi] = g * _sigmoid(g)

        return dot_cols(3 * d, lo, hi), consume

    stages = []
    for lo, hi in _col_pieces(d, HGRN_COLS):
        stages += [forget_stage(lo, hi), query_stage(lo, hi), gate_stage(lo, hi), value_stage(lo, hi)]
    _pipelined(stages)


def _hgrn_in_call(x2, mod3, gain, w_in, lb_logits, layer_j, seq):
    t, d = x2.shape
    tm = min(512, seq)
    assert t % tm == 0 and seq % tm == 0
    per_seq = seq // tm
    row = pl.BlockSpec((tm, d), lambda i: (i, 0))
    f32_out = jax.ShapeDtypeStruct((t, d), F32)
    return pl.pallas_call(
        functools.partial(_hgrn_in_kernel, d=d, layer_j=layer_j),
        grid=(t // tm,),
        in_specs=[
            row,
            pl.BlockSpec((1, 3, d), lambda i: (i // per_seq, 0, 0)),
            _resident((1, d)),
            w_in.spec(),
            _resident(lb_logits.shape),
        ],
        out_specs=[row, row, row, row, row],
        out_shape=[f32_out, f32_out, f32_out, jax.ShapeDtypeStruct((t, d), BF16), f32_out],
        compiler_params=_cparams(1),
        name="hgrn_in_proj",
    )(x2, mod3, gain, w_in.array, lb_logits)


def _rows_at(g_cum, half):
    c = g_cum.shape[0]
    if half >= 4:
        blk = 2 * half
        return jnp.concatenate(
            [jnp.broadcast_to(g_cum[p * blk + half - 1:p * blk + half, :], (blk, g_cum.shape[1]))
             for p in range(c // blk)], axis=0)
    row = lax.broadcasted_iota(jnp.int32, g_cum.shape, 0)
    if half == 2:
        r4 = row % 4
        up1 = pltpu.roll(g_cum, c - 1, 0)
        dn1 = pltpu.roll(g_cum, 1, 0)
        dn2 = pltpu.roll(g_cum, 2, 0)
        return jnp.where(r4 == 0, up1, jnp.where(r4 == 1, g_cum, jnp.where(r4 == 2, dn1, dn2)))
    assert half == 1
    return jnp.where(row % 2 == 0, g_cum, pltpu.roll(g_cum, 1, 0))


def _round_robin(generators, width):
    pending = list(generators)
    active = []
    while pending or active:
        while pending and len(active) < width:
            active.append(pending.pop(0))
        for gen in list(active):
            try:
                next(gen)
            except StopIteration:
                active.remove(gen)


def _hgrn_rec_kernel(q_ref, k_ref, lf_ref, v_ref, og_ref, gain_ref, o_ref,
                     qe_ref, a_ref, u_ref, dl_ref, st_ref, *, chunk):
    seq = q_ref.shape[0]
    c = chunk
    n_chunks = seq // c
    ri = lax.broadcasted_iota(jnp.int32, (c, c), 0)
    ci = lax.broadcasted_iota(jnp.int32, (c, c), 1)
    tri = jnp.where(ri >= ci, 1.0, 0.0).astype(BF16)
    pair_key = jnp.where(ri >= ci, ri ^ ci, -1)
    row = lax.broadcasted_iota(jnp.int32, (c, HGRN_DK), 0)

    def neg_abs(z):
        return lax.bitcast_convert_type(
            lax.bitcast_convert_type(z, jnp.int32) | jnp.int32(-2 ** 31), F32)

    def intra(n):
        sl = slice(n * c, (n + 1) * c)
        g = lf_ref[sl, :]
        q = q_ref[sl, :]
        k = k_ref[sl, :]
        g_hi = g.astype(BF16)
        r1 = g - g_hi.astype(F32)
        g_mid = r1.astype(BF16)
        g_lo = (r1 - g_mid.astype(F32)).astype(BF16)
        g2 = (_dot(tri, g_hi) + (_dot(tri, g_mid) + _dot(tri, g_lo))) * LOG2E
        g2_last = g2[c - 1:c, :]
        yield
        qe_ref[sl, :] = (q * jnp.exp2(g2)).astype(BF16)
        k_dec = (k * jnp.exp2(g2_last - g2)).astype(BF16)
        u_ref[n] = _dot_tn(v_ref[sl, :], k_dec)
        dl_ref[n] = jnp.exp2(g2_last)
        scores = jnp.where(pair_key == 0, _dot_nt(q.astype(BF16), k.astype(BF16)), 0.0)
        yield
        half = 1
        while half < c:
            e = jnp.exp2(neg_abs(g2 - _rows_at(g2, half)))
            if half < 8:
                level = _dot_nt((q * e).astype(BF16), (k * e).astype(BF16))
                scores = jnp.where(pair_key >= half, level, scores)
            else:
                qk = jnp.concatenate(
                    [(q if (b % 2) else k)[b * half:(b + 1) * half] for b in range(c // half)], axis=0)
                zf = qk * e
                z = zf.astype(BF16)
                z_right = jnp.concatenate(
                    [zf[b * half:(b + 1) * half] for b in range(1, c // half, 2)], axis=0).astype(BF16)
                level = _dot_nt(z_right, z)
                pieces = []
                for b in range(c // half):
                    rows = slice(b * half, (b + 1) * half)
                    piece = scores[rows]
                    if b % 2:
                        lv = level[(b // 2) * half:(b // 2 + 1) * half]
                        piece = jnp.where(pair_key[rows] >= half, lv, piece)
                    pieces.append(piece)
                scores = jnp.concatenate(pieces, axis=0)
            half *= 2
            yield
        a_ref[n] = scores.astype(BF16)

    def outputs(n):
        sl = slice(n * c, (n + 1) * c)
        o = _dot_nt(qe_ref[sl, :], st_ref[n]) + _dot(a_ref[n], v_ref[sl, :])
        yield
        ms = jnp.mean(o * o, axis=-1, keepdims=True)
        o_ref[sl, :] = (o * lax.rsqrt(ms + EPS) * gain_ref[...] * og_ref[sl, :]).astype(BF16)

    _round_robin([intra(n) for n in range(n_chunks)], HGRN_INTERLEAVE)

    state = jnp.zeros((HGRN_DV, HGRN_DK), F32)
    for n in range(n_chunks):
        st_ref[n] = state.astype(BF16)
        state = dl_ref[n] * state + u_ref[n]

    _round_robin([outputs(n) for n in range(n_chunks)], HGRN_INTERLEAVE)


def _hgrn_rec_call(q2, k2, lf2, v2, og2, out_gain2, batch, seq):
    t, d = q2.shape
    heads = d // HGRN_DV
    c = HGRN_CHUNK
    assert seq % c == 0
    n_chunks = seq // c
    blk = pl.BlockSpec((seq, HGRN_DV), lambda b, h: (b, h))
    return pl.pallas_call(
        functools.partial(_hgrn_rec_kernel, chunk=c),
        grid=(batch, heads),
        in_specs=[blk, blk, blk, blk, blk, pl.BlockSpec((1, HGRN_DV), lambda b, h: (0, h))],
        out_specs=blk,
        out_shape=jax.ShapeDtypeStruct((t, d), BF16),
        scratch_shapes=[
            pltpu.VMEM((seq, HGRN_DK), BF16),
            pltpu.VMEM((n_chunks, c, c), BF16),
            pltpu.VMEM((n_chunks, HGRN_DV, HGRN_DK), F32),
            pltpu.VMEM((n_chunks, 1, HGRN_DK), F32),
            pltpu.VMEM((n_chunks, HGRN_DV, HGRN_DK), BF16),
        ],
        compiler_params=_cparams(2),
        name="hgrn_recurrence",
    )(q2, k2, lf2, v2, og2, out_gain2)


def kernel(x, c, ada_w, ada_b, norm_g, ffn_w_in, ffn_w_down, attn_w_qkv, attn_w_o, attn_q_gain,
           attn_k_gain, attn_lambda, attn_subln_gain, rel_bias, hgrn_w_in, hgrn_w_o,
           hgrn_out_gain, hgrn_lb_logits):
    batch, seq, d = x.shape
    depth = ada_w.shape[0]
    mod = _ada_call(c, ada_w, ada_b).reshape(depth, batch, N_SUBLAYERS, 3, d)
    bias_tiles = _bias_call(rel_bias, ATTN_TILE)
    x2 = x.reshape(batch * seq, d)
    qkv_b, hgrn_in_b = attn_w_qkv.astype(BF16), hgrn_w_in.astype(BF16)
    for layer in range(depth):
        gains = norm_g[layer].reshape(N_SUBLAYERS, 1, d)
        x2 = _ffn_call(x2, mod[layer, :, 0], gains[0], _Stacked(ffn_w_in, (layer, 0)),
                       _Stacked(ffn_w_down, (layer, 0)), seq)
        j = layer // N_MIXERS
        if layer % N_MIXERS == 0:
            reps = QKV_COLS // ATTN_HEAD_DIM
            q2, k2, v2 = _qkv_call(
                x2, mod[layer, :, 1], gains[1], _Stacked(qkv_b, (j,)),
                jnp.tile(attn_q_gain[j], reps).reshape(1, QKV_COLS),
                jnp.tile(attn_k_gain[j], reps).reshape(1, QKV_COLS), seq)
            lambda_init = 0.8 - 0.6 * math.exp(-0.3 * layer)
            o2 = _attn_call(q2, k2, v2, bias_tiles, attn_lambda[j],
                            attn_subln_gain[j].reshape(1, ATTN_V_DIM), batch, seq, lambda_init)
            w_o = _Stacked(attn_w_o, (j,))
        else:
            q2, k2, lf2, v2, og2 = _hgrn_in_call(
                x2, mod[layer, :, 1], gains[1], _Stacked(hgrn_in_b, (j,)), hgrn_lb_logits, j, seq)
            o2 = _hgrn_rec_call(q2, k2, lf2, v2, og2, hgrn_out_gain[j].reshape(1, d), batch, seq)
            w_o = _Stacked(hgrn_w_o, (j,))
        x2 = _ffn_call(x2, mod[layer, :, 2], gains[2], _Stacked(ffn_w_in, (layer, 1)),
                       _Stacked(ffn_w_down, (layer, 1)), seq, mixer=(o2, mod[layer, :, 1], w_o))
    return x2.reshape(batch, seq, d)
```

```python
import functools
import math
from typing import NamedTuple

import numpy as np
import jax
import jax.numpy as jnp
from jax import lax
from jax.experimental import pallas as pl
from jax.experimental.pallas import tpu as pltpu

F32 = jnp.float32
BF16 = jnp.bfloat16

EPS = 1e-6
N_SUBLAYERS = 3
N_MIXERS = 2
ATTN_HEADS = 8
ATTN_HEAD_DIM = 64
ATTN_V_DIM = 128
NUM_BUCKETS = 32
MAX_DISTANCE = 128
HGRN_HEADS = 8
HGRN_DK = 128
HGRN_DV = 128

LANES = 128
VMEM_LIMIT = 56 * 1024 * 1024
NEG = -1e30
LOG2E = math.log2(math.e)

MXU_COLS = 256
ATTN_TILE = 256
QKV_COLS = MXU_COLS
FFN_COLS = 2 * MXU_COLS
HGRN_COLS = MXU_COLS
HGRN_CHUNK = 128
HGRN_INTERLEAVE = 8


def _cparams(n_axes):
    return pltpu.CompilerParams(
        dimension_semantics=("arbitrary",) * n_axes, vmem_limit_bytes=VMEM_LIMIT)


def _resident(shape):
    return pl.BlockSpec(shape, lambda *_: (0,) * len(shape), pipeline_mode=pl.Buffered(1))


class _Stacked(NamedTuple):
    array: jax.Array
    lead: tuple = ()

    @property
    def shape(self):
        return self.array.shape[len(self.lead):]

    def spec(self):
        lead, tail = self.lead, self.shape
        return pl.BlockSpec((None,) * len(lead) + tail, lambda *_: lead + (0,) * len(tail),
                            pipeline_mode=pl.Buffered(1))


def _sigmoid(z):
    return 1.0 / (1.0 + jnp.exp(-z))


def _dot(a, b):
    return jnp.dot(a, b, preferred_element_type=F32)


def _dot_nt(a, b):
    return lax.dot_general(a, b, (((1,), (1,)), ((), ())), preferred_element_type=F32)


def _dot_tn(a, b):
    return lax.dot_general(a, b, (((0,), (0,)), ((), ())), preferred_element_type=F32)


def _pipelined(stages):
    pending = None
    for produce, consume in stages:
        value = produce()
        if pending is not None:
            pending[1](pending[0])
        pending = (value, consume)
    if pending is not None:
        pending[1](pending[0])


def _col_pieces(total, width):
    return [(lo, min(lo + width, total)) for lo in range(0, total, width)]


def _store_heads(ref, lo, hi, value):
    for col in range(lo, hi, LANES):
        ref[col // LANES] = value[:, col - lo:col - lo + LANES]


def _head_major(heads, rows):
    return pl.BlockSpec((heads, rows, LANES), lambda i: (0, i, 0))


def _norm_mod(x, gain, mod_ref):
    ms = jnp.mean(x * x, axis=-1, keepdims=True)
    h = x * lax.rsqrt(ms + EPS) * gain
    return h * (1.0 + mod_ref[0, 1:2, :]) + mod_ref[0, 0:1, :]


def _ada_kernel(c_ref, w_ref, b_ref, o_ref):
    c = c_ref[...]
    ca = (c * _sigmoid(c)).astype(BF16)
    o_ref[0] = _dot(ca, w_ref[0].astype(BF16)) + b_ref[0]


def _ada_call(c, ada_w, ada_b):
    depth, d, n = ada_w.shape
    b = c.shape[0]
    tn = 1152
    assert n % tn == 0
    return pl.pallas_call(
        _ada_kernel,
        grid=(depth, n // tn),
        in_specs=[
            pl.BlockSpec((b, d), lambda l, j: (0, 0)),
            pl.BlockSpec((1, d, tn), lambda l, j: (l, 0, j)),
            pl.BlockSpec((1, 1, tn), lambda l, j: (l, 0, j)),
        ],
        out_specs=pl.BlockSpec((1, b, tn), lambda l, j: (l, 0, j)),
        out_shape=jax.ShapeDtypeStruct((depth, b, n), F32),
        compiler_params=_cparams(2),
        name="ada_mod",
    )(c, ada_w, ada_b.reshape(depth, 1, n))


def _ffn_kernel(*refs, d_ff, with_mixer):
    if with_mixer:
        (x_ref, mix_ref, mixmod_ref, wo_ref, mod_ref, g_ref, win_ref, wdn_ref, o_ref, act_ref) = refs
        mix = jnp.concatenate([mix_ref[hd] for hd in range(mix_ref.shape[0])], axis=1)
        x = x_ref[...] + mixmod_ref[0, 2:3, :] * _dot(mix, wo_ref[...])
    else:
        (x_ref, mod_ref, g_ref, win_ref, wdn_ref, o_ref, act_ref) = refs
        x = x_ref[...]
    h = _norm_mod(x, g_ref[...], mod_ref).astype(BF16)

    def stage(lo, hi):
        def produce():
            return _dot(h, win_ref[:, lo:hi]), _dot(h, win_ref[:, d_ff + lo:d_ff + hi])

        def consume(ab):
            a, b = ab
            act_ref[:, lo:hi] = (a * _sigmoid(a) * b).astype(BF16)

        return produce, consume

    _pipelined([stage(lo, hi) for lo, hi in _col_pieces(d_ff, FFN_COLS)])
    y = _dot(act_ref[...], wdn_ref[...])
    o_ref[...] = x + 0.5 * mod_ref[0, 2:3, :] * y


def _ffn_call(x2, mod3, gain, w_in, w_down, seq, mixer=None):
    t, d = x2.shape
    d_ff = w_down.shape[0]
    tm = min(1024, seq)
    assert t % tm == 0 and seq % tm == 0 and d_ff % MXU_COLS == 0
    per_seq = seq // tm
    row = pl.BlockSpec((tm, d), lambda i: (i, 0))
    mod_spec = pl.BlockSpec((1, 3, d), lambda i: (i // per_seq, 0, 0))
    operands, in_specs = [x2], [row]
    if mixer is not None:
        mix, mix_mod3, w_o = mixer
        operands += [mix, mix_mod3, w_o.array]
        in_specs += [_head_major(mix.shape[0], tm), mod_spec, w_o.spec()]
    operands += [mod3, gain, w_in.array, w_down.array]
    in_specs += [mod_spec, _resident((1, d)), w_in.spec(), w_down.spec()]
    return pl.pallas_call(
        functools.partial(_ffn_kernel, d_ff=d_ff, with_mixer=mixer is not None),
        grid=(t // tm,),
        in_specs=in_specs,
        out_specs=row,
        out_shape=jax.ShapeDtypeStruct((t, d), F32),
        scratch_shapes=[pltpu.VMEM((tm, d_ff), BF16)],
        compiler_params=_cparams(1),
        name="ffn_half_step",
    )(*operands)


def _group_mean_sq(z, ones_blk):
    return _dot((z * z).astype(BF16), ones_blk)


def _qkv_kernel(x_ref, mod_ref, g_ref, w_ref, qg_ref, kg_ref, q_ref, k_ref, v_ref, *, d):
    h = _norm_mod(x_ref[...], g_ref[...], mod_ref).astype(BF16)
    tn = QKV_COLS
    r = lax.broadcasted_iota(jnp.int32, (tn, tn), 0) // ATTN_HEAD_DIM
    c = lax.broadcasted_iota(jnp.int32, (tn, tn), 1) // ATTN_HEAD_DIM
    ones_blk = jnp.where(r == c, 1.0 / ATTN_HEAD_DIM, 0.0).astype(BF16)
    q_gain = qg_ref[...] * (ATTN_HEAD_DIM ** -0.5 * LOG2E)
    k_gain = kg_ref[...]

    def normed(w_lo, out_ref, lo, hi, gain):
        def consume(z):
            _store_heads(out_ref, lo, hi,
                         (z * lax.rsqrt(_group_mean_sq(z, ones_blk) + EPS) * gain).astype(BF16))

        return (lambda: _dot(h, w_ref[:, w_lo + lo:w_lo + hi])), consume

    def plain(w_lo, out_ref, lo, hi):
        def consume(z):
            _store_heads(out_ref, lo, hi, z.astype(BF16))

        return (lambda: _dot(h, w_ref[:, w_lo + lo:w_lo + hi])), consume

    stages = []
    for lo, hi in _col_pieces(d, tn):
        stages += [normed(0, q_ref, lo, hi, q_gain), normed(d, k_ref, lo, hi, k_gain),
                   plain(2 * d, v_ref, lo, hi)]
    _pipelined(stages)


def _qkv_call(x2, mod3, gain, w_qkv, q_gain2, k_gain2, seq):
    t, d = x2.shape
    tm = min(512, seq)
    assert t % tm == 0 and seq % tm == 0
    per_seq = seq // tm
    heads = d // LANES
    out = jax.ShapeDtypeStruct((heads, t, LANES), BF16)
    row = pl.BlockSpec((tm, d), lambda i: (i, 0))
    out_spec = _head_major(heads, tm)
    return pl.pallas_call(
        functools.partial(_qkv_kernel, d=d),
        grid=(t // tm,),
        in_specs=[
            row,
            pl.BlockSpec((1, 3, d), lambda i: (i // per_seq, 0, 0)),
            _resident((1, d)),
            w_qkv.spec(),
            _resident((1, QKV_COLS)),
            _resident((1, QKV_COLS)),
        ],
        out_specs=[out_spec, out_spec, out_spec],
        out_shape=[out, out, out],
        compiler_params=_cparams(1),
        name="attn_qkv_proj",
    )(x2, mod3, gain, w_qkv.array, q_gain2, k_gain2)


def _bucket_thresholds():
    max_exact = NUM_BUCKETS // 2
    dist = np.arange(0, 4 * MAX_DISTANCE)
    d_f = np.maximum(dist, 1).astype(np.float32)
    large = max_exact + (np.log(d_f / np.float32(max_exact)) / np.float32(math.log(MAX_DISTANCE / max_exact))
                         * np.float32(NUM_BUCKETS - max_exact)).astype(np.int32)
    bucket = np.where(dist < max_exact, dist, np.minimum(large, NUM_BUCKETS - 1))
    assert np.all(np.diff(bucket) >= 0) and bucket[-1] == NUM_BUCKETS - 1
    return [int(np.argmax(bucket >= b)) for b in range(NUM_BUCKETS)]


def _bias_kernel(rb_ref, o_ref, *, thresholds, tile):
    h = pl.program_id(0)
    i = lax.broadcasted_iota(jnp.int32, (tile, tile), 0)
    j = lax.broadcasted_iota(jnp.int32, (tile, tile), 1)
    for off in range(3):
        dist = off * tile + i - j
        val = jnp.full((tile, tile), rb_ref[0, h], F32)
        for b in range(1, NUM_BUCKETS):
            val = jnp.where(dist >= thresholds[b], rb_ref[b, h], val)
        o_ref[0, off] = jnp.where(dist >= 0, val * LOG2E, NEG)


def _bias_call(rel_bias, tile):
    thresholds = _bucket_thresholds()
    assert thresholds[NUM_BUCKETS - 1] <= tile + 1
    heads = rel_bias.shape[1]
    return pl.pallas_call(
        functools.partial(_bias_kernel, thresholds=thresholds, tile=tile),
        grid=(heads,),
        in_specs=[pl.BlockSpec(memory_space=pltpu.SMEM)],
        out_specs=pl.BlockSpec((1, 3, tile, tile), lambda h: (h, 0, 0, 0)),
        out_shape=jax.ShapeDtypeStruct((heads, 3, tile, tile), F32),
        compiler_params=_cparams(1),
        name="attn_rel_bias_tiles",
    )(rel_bias)


def _attn_kernel(q_ref, k_ref, v_ref, bias_ref, lam_ref, sg_ref, o_ref, v1_ref,
                 s0_ref, s1_ref, p0_ref, p1_ref, m0_ref, m1_ref, mf0_ref, mf1_ref,
                 *, tile, lambda_init):
    seq = q_ref.shape[0]
    dv = ATTN_V_DIM
    n_tiles = seq // tile
    s_refs, p_refs = (s0_ref, s1_ref), (p0_ref, p1_ref)
    m_refs, mfar_refs = (m0_ref, m1_ref), (mf0_ref, mf1_ref)
    v1_ref[:, :dv] = v_ref[...]
    v1_ref[:, dv:] = jnp.ones((seq, dv), BF16)
    lam = lam_ref[...]
    lam_full = (jnp.exp(jnp.sum(lam[0:1] * lam[1:2], axis=-1, keepdims=True))
                - jnp.exp(jnp.sum(lam[2:3] * lam[3:4], axis=-1, keepdims=True)) + lambda_init)
    lane = lax.broadcasted_iota(jnp.int32, (tile, dv), 1)

    def stacked_q(i):
        q = q_ref[i * tile:(i + 1) * tile, :]
        zero = jnp.zeros_like(q)
        return jnp.concatenate([jnp.where(lane < ATTN_HEAD_DIM, q, zero),
                                jnp.where(lane >= ATTN_HEAD_DIM, q, zero)], axis=0)

    far_bias = bias_ref[0, 2, 0:1, 0:1]
    running_max = {}

    def is_far(i, j):
        return i - j >= 2

    def scores(i, j, qs):
        s = _dot_nt(qs, k_ref[j * tile:(j + 1) * tile, :])
        if not is_far(i, j):
            bt = bias_ref[0, i - j]
            s = s + jnp.concatenate([bt, bt], axis=0)
        s_refs[i % 2][:, j * tile:(j + 1) * tile] = s
        block_max = functools.reduce(
            jnp.maximum, [s[:, c:c + LANES] for c in range(0, tile, LANES)])
        key = (i, is_far(i, j))
        running_max[key] = (jnp.maximum(running_max[key], block_max)
                            if key in running_max else block_max)

    def row_max(i):
        m = jnp.max(running_max.pop((i, False)), axis=-1, keepdims=True)
        if (i, True) in running_max:
            m_far = jnp.max(running_max.pop((i, True)), axis=-1, keepdims=True) + far_bias
            m = jnp.maximum(m, m_far)
            mfar_refs[i % 2][...] = jnp.broadcast_to(m - far_bias, mfar_refs[i % 2].shape)
        m_refs[i % 2][...] = jnp.broadcast_to(m, m_refs[i % 2].shape)

    def probs(i, j):
        m = (mfar_refs if is_far(i, j) else m_refs)[i % 2][...]
        m_wide = jnp.concatenate([m] * (tile // LANES), axis=1)
        cols = slice(j * tile, (j + 1) * tile)
        p_refs[i % 2][:, cols] = jnp.exp2(s_refs[i % 2][:, cols] - m_wide).astype(BF16)

    def finish(i):
        n_keys = (i + 1) * tile
        acc = _dot(p_refs[i % 2][:, :n_keys], v1_ref[:n_keys, :])
        o_all = acc[:, :dv] / acc[:, dv:]
        o = o_all[:tile] - lam_full * o_all[tile:]
        ms = jnp.mean(o * o, axis=-1, keepdims=True)
        o = o * lax.rsqrt(ms + EPS) * sg_ref[...] * (1.0 - lambda_init)
        o_ref[i * tile:(i + 1) * tile, :] = o.astype(BF16)

    def score_and_prob_steps(i_scores, i_probs):
        qs = stacked_q(i_scores) if i_scores >= 0 else None
        for j in range(max(i_scores, i_probs) + 1):
            if j <= i_scores:
                scores(i_scores, j, qs)
            if j <= i_probs:
                probs(i_probs, j)
        if i_scores >= 0:
            row_max(i_scores)

    last = n_tiles - 1
    score_and_prob_steps(last, -1)
    score_and_prob_steps(last - 1, last)
    for i in range(last, -1, -1):
        finish(i)
        score_and_prob_steps(i - 2, i - 1)


def _attn_call(q2, k2, v2, bias_tiles, lam, subln_gain2, batch, seq, lambda_init):
    heads, t, _ = q2.shape
    tile = ATTN_TILE
    assert seq % tile == 0
    blk = pl.BlockSpec((None, seq, ATTN_V_DIM), lambda b, h: (h, b, 0))
    return pl.pallas_call(
        functools.partial(_attn_kernel, tile=tile, lambda_init=lambda_init),
        grid=(batch, heads),
        in_specs=[
            blk, blk, blk,
            pl.BlockSpec((1, 3, tile, tile), lambda b, h: (h, 0, 0, 0)),
            pl.BlockSpec(lam.shape, lambda b, h: (0, 0)),
            pl.BlockSpec((1, ATTN_V_DIM), lambda b, h: (0, 0)),
        ],
        out_specs=blk,
        out_shape=jax.ShapeDtypeStruct((heads, t, ATTN_V_DIM), BF16),
        scratch_shapes=[
            pltpu.VMEM((seq, 2 * ATTN_V_DIM), BF16),
            pltpu.VMEM((2 * tile, seq), F32),
            pltpu.VMEM((2 * tile, seq), F32),
            pltpu.VMEM((2 * tile, seq), BF16),
            pltpu.VMEM((2 * tile, seq), BF16),
            pltpu.VMEM((2 * tile, LANES), F32),
            pltpu.VMEM((2 * tile, LANES), F32),
            pltpu.VMEM((2 * tile, LANES), F32),
            pltpu.VMEM((2 * tile, LANES), F32),
        ],
        compiler_params=_cparams(2),
        name="diff_attention",
    )(q2, k2, v2, bias_tiles, lam, subln_gain2)


def _hgrn_in_kernel(x_ref, mod_ref, g_ref, w_ref, lbl_ref, q_ref, k_ref, lf_ref, v_ref, og_ref,
                    *, d, layer_j):
    h = _norm_mod(x_ref[...], g_ref[...], mod_ref).astype(BF16)
    logits = lbl_ref[...]
    e = jnp.exp(logits - jnp.max(logits, axis=0, keepdims=True))
    p = e / jnp.sum(e, axis=0, keepdims=True)
    lb = jnp.zeros((1, d), F32)
    for i in range(1, layer_j + 1):
        lb = lb + p[i:i + 1, :]
    log_lb = jnp.log(lb)
    log_1m_lb = jnp.log1p(-lb)

    def dot_cols(w_lo, lo, hi):
        return lambda: _dot(h, w_ref[:, w_lo + lo:w_lo + hi])

    def forget_stage(lo, hi):
        def consume(f):
            ef = jnp.exp(-jnp.abs(f))
            one_plus = 1.0 + ef
            log_sig = jnp.minimum(f, 0.0) - jnp.log(one_plus)
            sig_neg = jnp.where(f >= 0.0, ef, 1.0) / one_plus
            if layer_j == 0:
                _store_heads(lf_ref, lo, hi, log_sig)
                _store_heads(k_ref, lo, hi, sig_neg)
            else:
                a = log_lb[:, lo:hi]
                b = log_1m_lb[:, lo:hi] + log_sig
                _store_heads(lf_ref, lo, hi,
                             jnp.maximum(a, b) + jnp.log(1.0 + jnp.exp(-jnp.abs(a - b))))
                _store_heads(k_ref, lo, hi, (1.0 - lb[:, lo:hi]) * sig_neg)

        return dot_cols(d, lo, hi), consume

    def query_stage(lo, hi):
        def consume(z):
            _store_heads(q_ref, lo, hi, z)

        return dot_cols(0, lo, hi), consume

    def value_stage(lo, hi):
        def consume(z):
            _store_heads(v_ref, lo, hi, z.astype(BF16))

        return dot_cols(2 * d, lo, hi), consume

    def gate_stage(lo, hi):
        def consume(g):
            _store_heads(og_ref, lo, hi, g * _sigmoid(g))

        return dot_cols(3 * d, lo, hi), consume

    stages = []
    for lo, hi in _col_pieces(d, HGRN_COLS):
        stages += [forget_stage(lo, hi), query_stage(lo, hi), gate_stage(lo, hi), value_stage(lo, hi)]
    _pipelined(stages)


def _hgrn_in_call(x2, mod3, gain, w_in, lb_logits, layer_j, seq):
    t, d = x2.shape
    tm = min(512, seq)
    assert t % tm == 0 and seq % tm == 0
    per_seq = seq // tm
    heads = d // LANES
    row = pl.BlockSpec((tm, d), lambda i: (i, 0))
    out_spec = _head_major(heads, tm)
    f32_out = jax.ShapeDtypeStruct((heads, t, LANES), F32)
    return pl.pallas_call(
        functools.partial(_hgrn_in_kernel, d=d, layer_j=layer_j),
        grid=(t // tm,),
        in_specs=[
            row,
            pl.BlockSpec((1, 3, d), lambda i: (i // per_seq, 0, 0)),
            _resident((1, d)),
            w_in.spec(),
            _resident(lb_logits.shape),
        ],
        out_specs=[out_spec] * 5,
        out_shape=[f32_out, f32_out, f32_out, jax.ShapeDtypeStruct((heads, t, LANES), BF16), f32_out],
        compiler_params=_cparams(1),
        name="hgrn_in_proj",
    )(x2, mod3, gain, w_in.array, lb_logits)


def _rows_at(g_cum, half):
    c = g_cum.shape[0]
    if half >= 4:
        blk = 2 * half
        return jnp.concatenate(
            [jnp.broadcast_to(g_cum[p * blk + half - 1:p * blk + half, :], (blk, g_cum.shape[1]))
             for p in range(c // blk)], axis=0)
    row = lax.broadcasted_iota(jnp.int32, g_cum.shape, 0)
    if half == 2:
        r4 = row % 4
        up1 = pltpu.roll(g_cum, c - 1, 0)
        dn1 = pltpu.roll(g_cum, 1, 0)
        dn2 = pltpu.roll(g_cum, 2, 0)
        return jnp.where(r4 == 0, up1, jnp.where(r4 == 1, g_cum, jnp.where(r4 == 2, dn1, dn2)))
    assert half == 1
    return jnp.where(row % 2 == 0, g_cum, pltpu.roll(g_cum, 1, 0))


def _round_robin(generators, width):
    pending = list(generators)
    active = []
    while pending or active:
        while pending and len(active) < width:
            active.append(pending.pop(0))
        for gen in list(active):
            try:
                next(gen)
            except StopIteration:
                active.remove(gen)


def _hgrn_rec_kernel(q_ref, k_ref, lf_ref, v_ref, og_ref, gain_ref, o_ref,
                     qe_ref, a_ref, u_ref, dl_ref, st_ref, *, chunk):
    seq = q_ref.shape[0]
    c = chunk
    n_chunks = seq // c
    ri = lax.broadcasted_iota(jnp.int32, (c, c), 0)
    ci = lax.broadcasted_iota(jnp.int32, (c, c), 1)
    tri = jnp.where(ri >= ci, 1.0, 0.0).astype(BF16)
    pair_key = jnp.where(ri >= ci, ri ^ ci, -1)
    row = lax.broadcasted_iota(jnp.int32, (c, HGRN_DK), 0)

    def neg_abs(z):
        return lax.bitcast_convert_type(
            lax.bitcast_convert_type(z, jnp.int32) | jnp.int32(-2 ** 31), F32)

    def intra(n):
        sl = slice(n * c, (n + 1) * c)
        g = lf_ref[sl, :]
        q = q_ref[sl, :]
        k = k_ref[sl, :]
        g_hi = g.astype(BF16)
        r1 = g - g_hi.astype(F32)
        g_mid = r1.astype(BF16)
        g_lo = (r1 - g_mid.astype(F32)).astype(BF16)
        g2 = (_dot(tri, g_hi) + (_dot(tri, g_mid) + _dot(tri, g_lo))) * LOG2E
        g2_last = g2[c - 1:c, :]
        yield
        qe_ref[sl, :] = (q * jnp.exp2(g2)).astype(BF16)
        k_dec = (k * jnp.exp2(g2_last - g2)).astype(BF16)
        u_ref[n] = _dot_tn(v_ref[sl, :], k_dec)
        dl_ref[n] = jnp.exp2(g2_last)
        scores = jnp.where(pair_key == 0, _dot_nt(q.astype(BF16), k.astype(BF16)), 0.0)
        yield
        half = 1
        while half < c:
            e = jnp.exp2(neg_abs(g2 - _rows_at(g2, half)))
            if half < 8:
                level = _dot_nt((q * e).astype(BF16), (k * e).astype(BF16))
                scores = jnp.where(pair_key >= half, level, scores)
            else:
                qk = jnp.concatenate(
                    [(q if (b % 2) else k)[b * half:(b + 1) * half] for b in range(c // half)], axis=0)
                zf = qk * e
                z = zf.astype(BF16)
                z_right = jnp.concatenate(
                    [zf[b * half:(b + 1) * half] for b in range(1, c // half, 2)], axis=0).astype(BF16)
                level = _dot_nt(z_right, z)
                pieces = []
                for b in range(c // half):
                    rows = slice(b * half, (b + 1) * half)
                    piece = scores[rows]
                    if b % 2:
                        lv = level[(b // 2) * half:(b // 2 + 1) * half]
                        piece = jnp.where(pair_key[rows] >= half, lv, piece)
                    pieces.append(piece)
                scores = jnp.concatenate(pieces, axis=0)
            half *= 2
            yield
        a_ref[n] = scores.astype(BF16)

    def outputs(n):
        sl = slice(n * c, (n + 1) * c)
        o = _dot_nt(qe_ref[sl, :], st_ref[n]) + _dot(a_ref[n], v_ref[sl, :])
        yield
        ms = jnp.mean(o * o, axis=-1, keepdims=True)
        o_ref[sl, :] = (o * lax.rsqrt(ms + EPS) * gain_ref[...] * og_ref[sl, :]).astype(BF16)

    _round_robin([intra(n) for n in range(n_chunks)], HGRN_INTERLEAVE)

    state = jnp.zeros((HGRN_DV, HGRN_DK), F32)
    for n in range(n_chunks):
        st_ref[n] = state.astype(BF16)
        state = dl_ref[n] * state + u_ref[n]

    _round_robin([outputs(n) for n in range(n_chunks)], HGRN_INTERLEAVE)


def _hgrn_rec_call(q2, k2, lf2, v2, og2, out_gain2, batch, seq):
    heads, t, _ = q2.shape
    c = HGRN_CHUNK
    assert seq % c == 0
    n_chunks = seq // c
    blk = pl.BlockSpec((None, seq, HGRN_DV), lambda b, h: (h, b, 0))
    return pl.pallas_call(
        functools.partial(_hgrn_rec_kernel, chunk=c),
        grid=(batch, heads),
        in_specs=[blk, blk, blk, blk, blk, pl.BlockSpec((1, HGRN_DV), lambda b, h: (0, h))],
        out_specs=blk,
        out_shape=jax.ShapeDtypeStruct((heads, t, HGRN_DV), BF16),
        scratch_shapes=[
            pltpu.VMEM((seq, HGRN_DK), BF16),
            pltpu.VMEM((n_chunks, c, c), BF16),
            pltpu.VMEM((n_chunks, HGRN_DV, HGRN_DK), F32),
            pltpu.VMEM((n_chunks, 1, HGRN_DK), F32),
            pltpu.VMEM((n_chunks, HGRN_DV, HGRN_DK), BF16),
        ],
        compiler_params=_cparams(2),
        name="hgrn_recurrence",
    )(q2, k2, lf2, v2, og2, out_gain2)


def kernel(x, c, ada_w, ada_b, norm_g, ffn_w_in, ffn_w_down, attn_w_qkv, attn_w_o, attn_q_gain,
           attn_k_gain, attn_lambda, attn_subln_gain, rel_bias, hgrn_w_in, hgrn_w_o,
           hgrn_out_gain, hgrn_lb_logits):
    batch, seq, d = x.shape
    depth = ada_w.shape[0]
    mod = _ada_call(c, ada_w, ada_b).reshape(depth, batch, N_SUBLAYERS, 3, d)
    bias_tiles = _bias_call(rel_bias, ATTN_TILE)
    x2 = x.reshape(batch * seq, d)
    ffn_in_b, ffn_down_b = ffn_w_in.astype(BF16), ffn_w_down.astype(BF16)
    qkv_b, attn_o_b = attn_w_qkv.astype(BF16), attn_w_o.astype(BF16)
    hgrn_in_b, hgrn_o_b = hgrn_w_in.astype(BF16), hgrn_w_o.astype(BF16)
    for layer in range(depth):
        gains = norm_g[layer].reshape(N_SUBLAYERS, 1, d)
        x2 = _ffn_call(x2, mod[layer, :, 0], gains[0], _Stacked(ffn_in_b, (layer, 0)),
                       _Stacked(ffn_down_b, (layer, 0)), seq)
        j = layer // N_MIXERS
        if layer % N_MIXERS == 0:
            reps = QKV_COLS // ATTN_HEAD_DIM
            q2, k2, v2 = _qkv_call(
                x2, mod[layer, :, 1], gains[1], _Stacked(qkv_b, (j,)),
                jnp.tile(attn_q_gain[j], reps).reshape(1, QKV_COLS),
                jnp.tile(attn_k_gain[j], reps).reshape(1, QKV_COLS), seq)
            lambda_init = 0.8 - 0.6 * math.exp(-0.3 * layer)
            o2 = _attn_call(q2, k2, v2, bias_tiles, attn_lambda[j],
                            attn_subln_gain[j].reshape(1, ATTN_V_DIM), batch, seq, lambda_init)
            w_o = _Stacked(attn_o_b, (j,))
        else:
            q2, k2, lf2, v2, og2 = _hgrn_in_call(
                x2, mod[layer, :, 1], gains[1], _Stacked(hgrn_in_b, (j,)), hgrn_lb_logits, j, seq)
            o2 = _hgrn_rec_call(q2, k2, lf2, v2, og2, hgrn_out_gain[j].reshape(1, d), batch, seq)
            w_o = _Stacked(hgrn_o_b, (j,))
        x2 = _ffn_call(x2, mod[layer, :, 2], gains[2], _Stacked(ffn_in_b, (layer, 1)),
                       _Stacked(ffn_down_b, (layer, 1)), seq, mixer=(o2, mod[layer, :, 1], w_o))
    return x2.reshape(batch, seq, d)
```

```python
import functools
import math
from typing import NamedTuple

import numpy as np
import jax
import jax.numpy as jnp
from jax import lax
from jax.experimental import pallas as pl
from jax.experimental.pallas import tpu as pltpu

F32 = jnp.float32
BF16 = jnp.bfloat16

EPS = 1e-6
N_SUBLAYERS = 3
N_MIXERS = 2
ATTN_HEADS = 8
ATTN_HEAD_DIM = 64
ATTN_V_DIM = 128
NUM_BUCKETS = 32
MAX_DISTANCE = 128
HGRN_HEADS = 8
HGRN_DK = 128
HGRN_DV = 128

LANES = 128
VMEM_LIMIT = 56 * 1024 * 1024
NEG = -1e30
LOG2E = math.log2(math.e)

MXU_COLS = 256
ATTN_TILE = 256
QKV_COLS = MXU_COLS
FFN_COLS = 2 * MXU_COLS
HGRN_COLS = MXU_COLS
HGRN_CHUNK = 128
HGRN_HEADS_PER_STEP = 4
ADA_COLS = 4608
HGRN_INTERLEAVE = 8


def _cparams(n_axes):
    return pltpu.CompilerParams(
        dimension_semantics=("arbitrary",) * n_axes, vmem_limit_bytes=VMEM_LIMIT)


def _resident(shape):
    return pl.BlockSpec(shape, lambda *_: (0,) * len(shape), pipeline_mode=pl.Buffered(1))


class _Stacked(NamedTuple):
    array: jax.Array
    lead: tuple = ()

    @property
    def shape(self):
        return self.array.shape[len(self.lead):]

    def spec(self):
        lead, tail = self.lead, self.shape
        return pl.BlockSpec((None,) * len(lead) + tail, lambda *_: lead + (0,) * len(tail),
                            pipeline_mode=pl.Buffered(1))


def _sigmoid(z):
    return 1.0 / (1.0 + jnp.exp(-z))


def _dot(a, b):
    return jnp.dot(a, b, preferred_element_type=F32)


def _dot_nt(a, b):
    return lax.dot_general(a, b, (((1,), (1,)), ((), ())), preferred_element_type=F32)


def _dot_tn(a, b):
    return lax.dot_general(a, b, (((0,), (0,)), ((), ())), preferred_element_type=F32)


def _pipelined(stages):
    pending = None
    for produce, consume in stages:
        value = produce()
        if pending is not None:
            pending[1](pending[0])
        pending = (value, consume)
    if pending is not None:
        pending[1](pending[0])


def _col_pieces(total, width):
    return [(lo, min(lo + width, total)) for lo in range(0, total, width)]


def _store_heads(ref, lo, hi, value):
    for col in range(lo, hi, LANES):
        ref[col // LANES] = value[:, col - lo:col - lo + LANES]


def _head_major(heads, rows):
    return pl.BlockSpec((heads, rows, LANES), lambda i: (0, i, 0))


def _norm_mod(x, gain, mod_ref):
    ms = jnp.mean(x * x, axis=-1, keepdims=True)
    h = x * lax.rsqrt(ms + EPS) * gain
    return h * (1.0 + mod_ref[0, 1:2, :]) + mod_ref[0, 0:1, :]


def _ada_kernel(c_ref, w_ref, b_ref, o_ref):
    c = c_ref[...]
    ca = (c * _sigmoid(c)).astype(BF16)
    o_ref[0] = _dot(ca, w_ref[0].astype(BF16)) + b_ref[0]


def _ada_call(c, ada_w, ada_b):
    depth, d, n = ada_w.shape
    b = c.shape[0]
    tn = ADA_COLS
    assert n % tn == 0
    return pl.pallas_call(
        _ada_kernel,
        grid=(depth, n // tn),
        in_specs=[
            pl.BlockSpec((b, d), lambda l, j: (0, 0)),
            pl.BlockSpec((1, d, tn), lambda l, j: (l, 0, j)),
            pl.BlockSpec((1, 1, tn), lambda l, j: (l, 0, j)),
        ],
        out_specs=pl.BlockSpec((1, b, tn), lambda l, j: (l, 0, j)),
        out_shape=jax.ShapeDtypeStruct((depth, b, n), F32),
        compiler_params=_cparams(2),
        name="ada_mod",
    )(c, ada_w, ada_b.reshape(depth, 1, n))


def _ffn_kernel(*refs, d_ff, with_mixer):
    if with_mixer:
        (x_ref, mix_ref, mixmod_ref, wo_ref, mod_ref, g_ref, win_ref, wdn_ref, o_ref, act_ref) = refs
        mix = jnp.concatenate([mix_ref[hd] for hd in range(mix_ref.shape[0])], axis=1)
        x = x_ref[...] + mixmod_ref[0, 2:3, :] * _dot(mix, wo_ref[...])
    else:
        (x_ref, mod_ref, g_ref, win_ref, wdn_ref, o_ref, act_ref) = refs
        x = x_ref[...]
    h = _norm_mod(x, g_ref[...], mod_ref).astype(BF16)

    def stage(lo, hi):
        def produce():
            return _dot(h, win_ref[:, lo:hi]), _dot(h, win_ref[:, d_ff + lo:d_ff + hi])

        def consume(ab):
            a, b = ab
            act_ref[:, lo:hi] = (a * _sigmoid(a) * b).astype(BF16)

        return produce, consume

    _pipelined([stage(lo, hi) for lo, hi in _col_pieces(d_ff, FFN_COLS)])
    y = _dot(act_ref[...], wdn_ref[...])
    o_ref[...] = x + 0.5 * mod_ref[0, 2:3, :] * y


def _ffn_call(x2, mod3, gain, w_in, w_down, seq, mixer=None):
    t, d = x2.shape
    d_ff = w_down.shape[0]
    tm = min(1024, seq)
    assert t % tm == 0 and seq % tm == 0 and d_ff % MXU_COLS == 0
    per_seq = seq // tm
    row = pl.BlockSpec((tm, d), lambda i: (i, 0))
    mod_spec = pl.BlockSpec((1, 3, d), lambda i: (i // per_seq, 0, 0))
    operands, in_specs = [x2], [row]
    if mixer is not None:
        mix, mix_mod3, w_o = mixer
        operands += [mix, mix_mod3, w_o.array]
        in_specs += [_head_major(mix.shape[0], tm), mod_spec, w_o.spec()]
    operands += [mod3, gain, w_in.array, w_down.array]
    in_specs += [mod_spec, _resident((1, d)), w_in.spec(), w_down.spec()]
    return pl.pallas_call(
        functools.partial(_ffn_kernel, d_ff=d_ff, with_mixer=mixer is not None),
        grid=(t // tm,),
        in_specs=in_specs,
        out_specs=row,
        out_shape=jax.ShapeDtypeStruct((t, d), F32),
        scratch_shapes=[pltpu.VMEM((tm, d_ff), BF16)],
        compiler_params=_cparams(1),
        name="ffn_half_step",
    )(*operands)


def _group_mean_sq(z, ones_blk):
    return _dot((z * z).astype(BF16), ones_blk)


def _qkv_kernel(x_ref, mod_ref, g_ref, w_ref, qg_ref, kg_ref, q_ref, k_ref, v_ref, *, d):
    h = _norm_mod(x_ref[...], g_ref[...], mod_ref).astype(BF16)
    tn = QKV_COLS
    r = lax.broadcasted_iota(jnp.int32, (tn, tn), 0) // ATTN_HEAD_DIM
    c = lax.broadcasted_iota(jnp.int32, (tn, tn), 1) // ATTN_HEAD_DIM
    ones_blk = jnp.where(r == c, 1.0 / ATTN_HEAD_DIM, 0.0).astype(BF16)
    q_gain = qg_ref[...] * (ATTN_HEAD_DIM ** -0.5 * LOG2E)
    k_gain = kg_ref[...]

    def normed(w_lo, out_ref, lo, hi, gain):
        def consume(z):
            _store_heads(out_ref, lo, hi,
                         (z * lax.rsqrt(_group_mean_sq(z, ones_blk) + EPS) * gain).astype(BF16))

        return (lambda: _dot(h, w_ref[:, w_lo + lo:w_lo + hi])), consume

    def plain(w_lo, out_ref, lo, hi):
        def consume(z):
            _store_heads(out_ref, lo, hi, z.astype(BF16))

        return (lambda: _dot(h, w_ref[:, w_lo + lo:w_lo + hi])), consume

    stages = []
    for lo, hi in _col_pieces(d, tn):
        stages += [normed(0, q_ref, lo, hi, q_gain), normed(d, k_ref, lo, hi, k_gain),
                   plain(2 * d, v_ref, lo, hi)]
    _pipelined(stages)


def _qkv_call(x2, mod3, gain, w_qkv, q_gain2, k_gain2, seq):
    t, d = x2.shape
    tm = min(1024, seq)
    assert t % tm == 0 and seq % tm == 0
    per_seq = seq // tm
    heads = d // LANES
    out = jax.ShapeDtypeStruct((heads, t, LANES), BF16)
    row = pl.BlockSpec((tm, d), lambda i: (i, 0))
    out_spec = _head_major(heads, tm)
    return pl.pallas_call(
        functools.partial(_qkv_kernel, d=d),
        grid=(t // tm,),
        in_specs=[
            row,
            pl.BlockSpec((1, 3, d), lambda i: (i // per_seq, 0, 0)),
            _resident((1, d)),
            w_qkv.spec(),
            _resident((1, QKV_COLS)),
            _resident((1, QKV_COLS)),
        ],
        out_specs=[out_spec, out_spec, out_spec],
        out_shape=[out, out, out],
        compiler_params=_cparams(1),
        name="attn_qkv_proj",
    )(x2, mod3, gain, w_qkv.array, q_gain2, k_gain2)


def _bucket_thresholds():
    max_exact = NUM_BUCKETS // 2
    dist = np.arange(0, 4 * MAX_DISTANCE)
    d_f = np.maximum(dist, 1).astype(np.float32)
    large = max_exact + (np.log(d_f / np.float32(max_exact)) / np.float32(math.log(MAX_DISTANCE / max_exact))
                         * np.float32(NUM_BUCKETS - max_exact)).astype(np.int32)
    bucket = np.where(dist < max_exact, dist, np.minimum(large, NUM_BUCKETS - 1))
    assert np.all(np.diff(bucket) >= 0) and bucket[-1] == NUM_BUCKETS - 1
    return [int(np.argmax(bucket >= b)) for b in range(NUM_BUCKETS)]


def _bias_kernel(rb_ref, o_ref, *, thresholds, tile):
    h = pl.program_id(0)
    i = lax.broadcasted_iota(jnp.int32, (tile, tile), 0)
    j = lax.broadcasted_iota(jnp.int32, (tile, tile), 1)
    for off in range(3):
        dist = off * tile + i - j
        val = jnp.full((tile, tile), rb_ref[0, h], F32)
        for b in range(1, NUM_BUCKETS):
            val = jnp.where(dist >= thresholds[b], rb_ref[b, h], val)
        o_ref[0, off] = jnp.where(dist >= 0, val * LOG2E, NEG)


def _bias_call(rel_bias, tile):
    thresholds = _bucket_thresholds()
    assert thresholds[NUM_BUCKETS - 1] <= tile + 1
    heads = rel_bias.shape[1]
    return pl.pallas_call(
        functools.partial(_bias_kernel, thresholds=thresholds, tile=tile),
        grid=(heads,),
        in_specs=[pl.BlockSpec(memory_space=pltpu.SMEM)],
        out_specs=pl.BlockSpec((1, 3, tile, tile), lambda h: (h, 0, 0, 0)),
        out_shape=jax.ShapeDtypeStruct((heads, 3, tile, tile), F32),
        compiler_params=_cparams(1),
        name="attn_rel_bias_tiles",
    )(rel_bias)


def _attn_kernel(q_ref, k_ref, v_ref, bias_ref, lam_ref, sg_ref, o_ref, v1_ref,
                 s0_ref, s1_ref, p0_ref, p1_ref, m0_ref, m1_ref, mf0_ref, mf1_ref,
                 *, tile, lambda_init):
    lam = lam_ref[...]
    lam_full = (jnp.exp(jnp.sum(lam[0:1] * lam[1:2], axis=-1, keepdims=True))
                - jnp.exp(jnp.sum(lam[2:3] * lam[3:4], axis=-1, keepdims=True)) + lambda_init)
    scratch = (v1_ref, (s0_ref, s1_ref), (p0_ref, p1_ref), (m0_ref, m1_ref), (mf0_ref, mf1_ref))

    def one_head(hd, carry):
        _attn_head(q_ref.at[hd], k_ref.at[hd], v_ref.at[hd], bias_ref.at[hd], sg_ref, o_ref.at[hd],
                   scratch, lam_full, tile=tile, lambda_init=lambda_init)
        return carry

    lax.fori_loop(0, q_ref.shape[0], one_head, 0)


def _attn_head(q_ref, k_ref, v_ref, bias_ref, sg_ref, o_ref, scratch, lam_full, *, tile, lambda_init):
    v1_ref, s_refs, p_refs, m_refs, mfar_refs = scratch
    seq = q_ref.shape[0]
    dv = ATTN_V_DIM
    n_tiles = seq // tile
    v1_ref[:, :dv] = v_ref[...]
    v1_ref[:, dv:] = jnp.ones((seq, dv), BF16)
    lane = lax.broadcasted_iota(jnp.int32, (tile, dv), 1)

    def stacked_q(i):
        q = q_ref[i * tile:(i + 1) * tile, :]
        zero = jnp.zeros_like(q)
        return jnp.concatenate([jnp.where(lane < ATTN_HEAD_DIM, q, zero),
                                jnp.where(lane >= ATTN_HEAD_DIM, q, zero)], axis=0)

    far_bias = bias_ref[2, 0:1, 0:1]
    running_max = {}

    def is_far(i, j):
        return i - j >= 2

    def scores(i, j, qs):
        s = _dot_nt(qs, k_ref[j * tile:(j + 1) * tile, :])
        if not is_far(i, j):
            bt = bias_ref[i - j]
            s = s + jnp.concatenate([bt, bt], axis=0)
        s_refs[i % 2][:, j * tile:(j + 1) * tile] = s
        block_max = functools.reduce(
            jnp.maximum, [s[:, c:c + LANES] for c in range(0, tile, LANES)])
        key = (i, is_far(i, j))
        running_max[key] = (jnp.maximum(running_max[key], block_max)
                            if key in running_max else block_max)

    def row_max(i):
        m = jnp.max(running_max.pop((i, False)), axis=-1, keepdims=True)
        if (i, True) in running_max:
            m_far = jnp.max(running_max.pop((i, True)), axis=-1, keepdims=True) + far_bias
            m = jnp.maximum(m, m_far)
            mfar_refs[i % 2][...] = jnp.broadcast_to(m - far_bias, mfar_refs[i % 2].shape)
        m_refs[i % 2][...] = jnp.broadcast_to(m, m_refs[i % 2].shape)

    def probs(i, j):
        m = (mfar_refs if is_far(i, j) else m_refs)[i % 2][...]
        m_wide = jnp.concatenate([m] * (tile // LANES), axis=1)
        cols = slice(j * tile, (j + 1) * tile)
        p_refs[i % 2][:, cols] = jnp.exp2(s_refs[i % 2][:, cols] - m_wide).astype(BF16)

    def finish(i):
        n_keys = (i + 1) * tile
        acc = _dot(p_refs[i % 2][:, :n_keys], v1_ref[:n_keys, :])
        o_all = acc[:, :dv] / acc[:, dv:]
        o = o_all[:tile] - lam_full * o_all[tile:]
        ms = jnp.mean(o * o, axis=-1, keepdims=True)
        o = o * lax.rsqrt(ms + EPS) * sg_ref[...] * (1.0 - lambda_init)
        o_ref[i * tile:(i + 1) * tile, :] = o.astype(BF16)

    def score_and_prob_steps(i_scores, i_probs):
        qs = stacked_q(i_scores) if i_scores >= 0 else None
        for j in range(max(i_scores, i_probs) + 1):
            if j <= i_scores:
                scores(i_scores, j, qs)
            if j <= i_probs:
                probs(i_probs, j)
        if i_scores >= 0:
            row_max(i_scores)

    last = n_tiles - 1
    score_and_prob_steps(last, -1)
    score_and_prob_steps(last - 1, last)
    for i in range(last, -1, -1):
        finish(i)
        score_and_prob_steps(i - 2, i - 1)


def _attn_call(q2, k2, v2, bias_tiles, lam, subln_gain2, batch, seq, lambda_init):
    heads, t, _ = q2.shape
    tile = ATTN_TILE
    assert seq % tile == 0
    blk = pl.BlockSpec((heads, seq, ATTN_V_DIM), lambda b: (0, b, 0))
    return pl.pallas_call(
        functools.partial(_attn_kernel, tile=tile, lambda_init=lambda_init),
        grid=(batch,),
        in_specs=[
            blk, blk, blk,
            _resident(bias_tiles.shape),
            _resident(lam.shape),
            _resident((1, ATTN_V_DIM)),
        ],
        out_specs=blk,
        out_shape=jax.ShapeDtypeStruct((heads, t, ATTN_V_DIM), BF16),
        scratch_shapes=[
            pltpu.VMEM((seq, 2 * ATTN_V_DIM), BF16),
            pltpu.VMEM((2 * tile, seq), F32),
            pltpu.VMEM((2 * tile, seq), F32),
            pltpu.VMEM((2 * tile, seq), BF16),
            pltpu.VMEM((2 * tile, seq), BF16),
            pltpu.VMEM((2 * tile, LANES), F32),
            pltpu.VMEM((2 * tile, LANES), F32),
            pltpu.VMEM((2 * tile, LANES), F32),
            pltpu.VMEM((2 * tile, LANES), F32),
        ],
        compiler_params=_cparams(1),
        name="diff_attention",
    )(q2, k2, v2, bias_tiles, lam, subln_gain2)


def _hgrn_in_kernel(x_ref, mod_ref, g_ref, w_ref, lbl_ref, q_ref, k_ref, lf_ref, v_ref, og_ref,
                    *, d, layer_j):
    h = _norm_mod(x_ref[...], g_ref[...], mod_ref).astype(BF16)
    logits = lbl_ref[...]
    e = jnp.exp(logits - jnp.max(logits, axis=0, keepdims=True))
    p = e / jnp.sum(e, axis=0, keepdims=True)
    lb = jnp.zeros((1, d), F32)
    for i in range(1, layer_j + 1):
        lb = lb + p[i:i + 1, :]
    log_lb = jnp.log(lb)
    log_1m_lb = jnp.log1p(-lb)

    def dot_cols(w_lo, lo, hi):
        return lambda: _dot(h, w_ref[:, w_lo + lo:w_lo + hi])

    def forget_stage(lo, hi):
        def consume(f):
            ef = jnp.exp(-jnp.abs(f))
            one_plus = 1.0 + ef
            log_sig = jnp.minimum(f, 0.0) - jnp.log(one_plus)
            sig_neg = jnp.where(f >= 0.0, ef, 1.0) / one_plus
            if layer_j == 0:
                _store_heads(lf_ref, lo, hi, log_sig)
                _store_heads(k_ref, lo, hi, sig_neg)
            else:
                a = log_lb[:, lo:hi]
                b = log_1m_lb[:, lo:hi] + log_sig
                _store_heads(lf_ref, lo, hi,
                             jnp.maximum(a, b) + jnp.log(1.0 + jnp.exp(-jnp.abs(a - b))))
                _store_heads(k_ref, lo, hi, (1.0 - lb[:, lo:hi]) * sig_neg)

        return dot_cols(d, lo, hi), consume

    def query_stage(lo, hi):
        def consume(z):
            _store_heads(q_ref, lo, hi, z)

        return dot_cols(0, lo, hi), consume

    def value_stage(lo, hi):
        def consume(z):
            _store_heads(v_ref, lo, hi, z.astype(BF16))

        return dot_cols(2 * d, lo, hi), consume

    def gate_stage(lo, hi):
        def consume(g):
            _store_heads(og_ref, lo, hi, g * _sigmoid(g))

        return dot_cols(3 * d, lo, hi), consume

    stages = []
    for lo, hi in _col_pieces(d, HGRN_COLS):
        stages += [forget_stage(lo, hi), query_stage(lo, hi), gate_stage(lo, hi), value_stage(lo, hi)]
    _pipelined(stages)


def _hgrn_in_call(x2, mod3, gain, w_in, lb_logits, layer_j, seq):
    t, d = x2.shape
    tm = min(512, seq)
    assert t % tm == 0 and seq % tm == 0
    per_seq = seq // tm
    heads = d // LANES
    row = pl.BlockSpec((tm, d), lambda i: (i, 0))
    out_spec = _head_major(heads, tm)
    f32_out = jax.ShapeDtypeStruct((heads, t, LANES), F32)
    return pl.pallas_call(
        functools.partial(_hgrn_in_kernel, d=d, layer_j=layer_j),
        grid=(t // tm,),
        in_specs=[
            row,
            pl.BlockSpec((1, 3, d), lambda i: (i // per_seq, 0, 0)),
            _resident((1, d)),
            w_in.spec(),
            _resident(lb_logits.shape),
        ],
        out_specs=[out_spec] * 5,
        out_shape=[f32_out, f32_out, f32_out, jax.ShapeDtypeStruct((heads, t, LANES), BF16), f32_out],
        compiler_params=_cparams(1),
        name="hgrn_in_proj",
    )(x2, mod3, gain, w_in.array, lb_logits)


def _rows_at(g_cum, half):
    c = g_cum.shape[0]
    if half >= 4:
        blk = 2 * half
        return jnp.concatenate(
            [jnp.broadcast_to(g_cum[p * blk + half - 1:p * blk + half, :], (blk, g_cum.shape[1]))
             for p in range(c // blk)], axis=0)
    row = lax.broadcasted_iota(jnp.int32, g_cum.shape, 0)
    if half == 2:
        r4 = row % 4
        up1 = pltpu.roll(g_cum, c - 1, 0)
        dn1 = pltpu.roll(g_cum, 1, 0)
        dn2 = pltpu.roll(g_cum, 2, 0)
        return jnp.where(r4 == 0, up1, jnp.where(r4 == 1, g_cum, jnp.where(r4 == 2, dn1, dn2)))
    assert half == 1
    return jnp.where(row % 2 == 0, g_cum, pltpu.roll(g_cum, 1, 0))


def _round_robin(generators, width):
    pending = list(generators)
    active = []
    while pending or active:
        while pending and len(active) < width:
            active.append(pending.pop(0))
        for gen in list(active):
            try:
                next(gen)
            except StopIteration:
                active.remove(gen)


def _hgrn_rec_kernel(q_ref, k_ref, lf_ref, v_ref, og_ref, gain_ref, o_ref, *scratch, chunk):
    def one_head(hd, carry):
        _hgrn_rec_head(q_ref.at[hd], k_ref.at[hd], lf_ref.at[hd], v_ref.at[hd], og_ref.at[hd],
                       gain_ref.at[hd], o_ref.at[hd], *scratch, chunk=chunk)
        return carry

    lax.fori_loop(0, q_ref.shape[0], one_head, 0)


def _hgrn_rec_head(q_ref, k_ref, lf_ref, v_ref, og_ref, gain_ref, o_ref,
                   qe_ref, a_ref, u_ref, dl_ref, st_ref, *, chunk):
    seq = q_ref.shape[0]
    c = chunk
    n_chunks = seq // c
    ri = lax.broadcasted_iota(jnp.int32, (c, c), 0)
    ci = lax.broadcasted_iota(jnp.int32, (c, c), 1)
    tri = jnp.where(ri >= ci, 1.0, 0.0).astype(BF16)
    pair_key = jnp.where(ri >= ci, ri ^ ci, -1)
    row = lax.broadcasted_iota(jnp.int32, (c, HGRN_DK), 0)

    def neg_abs(z):
        return lax.bitcast_convert_type(
            lax.bitcast_convert_type(z, jnp.int32) | jnp.int32(-2 ** 31), F32)

    def intra(n):
        sl = slice(n * c, (n + 1) * c)
        g = lf_ref[sl, :]
        q = q_ref[sl, :]
        k = k_ref[sl, :]
        g_hi = g.astype(BF16)
        r1 = g - g_hi.astype(F32)
        g_mid = r1.astype(BF16)
        g_lo = (r1 - g_mid.astype(F32)).astype(BF16)
        g2 = (_dot(tri, g_hi) + (_dot(tri, g_mid) + _dot(tri, g_lo))) * LOG2E
        g2_last = g2[c - 1:c, :]
        yield
        qe_ref[sl, :] = (q * jnp.exp2(g2)).astype(BF16)
        k_dec = (k * jnp.exp2(g2_last - g2)).astype(BF16)
        u_ref[n] = _dot_tn(v_ref[sl, :], k_dec)
        dl_ref[n] = jnp.exp2(g2_last)
        scores = jnp.where(pair_key == 0, _dot_nt(q.astype(BF16), k.astype(BF16)), 0.0)
        yield
        half = 1
        while half < c:
            e = jnp.exp2(neg_abs(g2 - _rows_at(g2, half)))
            if half < 8:
                level = _dot_nt((q * e).astype(BF16), (k * e).astype(BF16))
                scores = jnp.where(pair_key >= half, level, scores)
            else:
                qk = jnp.concatenate(
                    [(q if (b % 2) else k)[b * half:(b + 1) * half] for b in range(c // half)], axis=0)
                zf = qk * e
                z = zf.astype(BF16)
                z_right = jnp.concatenate(
                    [zf[b * half:(b + 1) * half] for b in range(1, c // half, 2)], axis=0).astype(BF16)
                level = _dot_nt(z_right, z)
                pieces = []
                for b in range(c // half):
                    rows = slice(b * half, (b + 1) * half)
                    piece = scores[rows]
                    if b % 2:
                        lv = level[(b // 2) * half:(b // 2 + 1) * half]
                        piece = jnp.where(pair_key[rows] >= half, lv, piece)
                    pieces.append(piece)
                scores = jnp.concatenate(pieces, axis=0)
            half *= 2
            yield
        a_ref[n] = scores.astype(BF16)

    def outputs(n):
        sl = slice(n * c, (n + 1) * c)
        o = _dot_nt(qe_ref[sl, :], st_ref[n]) + _dot(a_ref[n], v_ref[sl, :])
        yield
        ms = jnp.mean(o * o, axis=-1, keepdims=True)
        o_ref[sl, :] = (o * lax.rsqrt(ms + EPS) * gain_ref[...] * og_ref[sl, :]).astype(BF16)

    _round_robin([intra(n) for n in range(n_chunks)], HGRN_INTERLEAVE)

    state = jnp.zeros((HGRN_DV, HGRN_DK), F32)
    for n in range(n_chunks):
        st_ref[n] = state.astype(BF16)
        state = dl_ref[n] * state + u_ref[n]

    _round_robin([outputs(n) for n in range(n_chunks)], HGRN_INTERLEAVE)


def _hgrn_rec_call(q2, k2, lf2, v2, og2, out_gain2, batch, seq):
    heads, t, _ = q2.shape
    c = HGRN_CHUNK
    assert seq % c == 0
    n_chunks = seq // c
    per_step = HGRN_HEADS_PER_STEP
    assert heads % per_step == 0
    blk = pl.BlockSpec((per_step, seq, HGRN_DV), lambda b, g: (g, b, 0))
    return pl.pallas_call(
        functools.partial(_hgrn_rec_kernel, chunk=c),
        grid=(batch, heads // per_step),
        in_specs=[blk, blk, blk, blk, blk,
                  pl.BlockSpec((per_step, 1, HGRN_DV), lambda b, g: (g, 0, 0))],
        out_specs=blk,
        out_shape=jax.ShapeDtypeStruct((heads, t, HGRN_DV), BF16),
        scratch_shapes=[
            pltpu.VMEM((seq, HGRN_DK), BF16),
            pltpu.VMEM((n_chunks, c, c), BF16),
            pltpu.VMEM((n_chunks, HGRN_DV, HGRN_DK), F32),
            pltpu.VMEM((n_chunks, 1, HGRN_DK), F32),
            pltpu.VMEM((n_chunks, HGRN_DV, HGRN_DK), BF16),
        ],
        compiler_params=_cparams(2),
        name="hgrn_recurrence",
    )(q2, k2, lf2, v2, og2, out_gain2)


def kernel(x, c, ada_w, ada_b, norm_g, ffn_w_in, ffn_w_down, attn_w_qkv, attn_w_o, attn_q_gain,
           attn_k_gain, attn_lambda, attn_subln_gain, rel_bias, hgrn_w_in, hgrn_w_o,
           hgrn_out_gain, hgrn_lb_logits):
    batch, seq, d = x.shape
    depth = ada_w.shape[0]
    mod = _ada_call(c, ada_w, ada_b).reshape(depth, batch, N_SUBLAYERS, 3, d)
    bias_tiles = _bias_call(rel_bias, ATTN_TILE)
    x2 = x.reshape(batch * seq, d)
    ffn_in_b, ffn_down_b = ffn_w_in.astype(BF16), ffn_w_down.astype(BF16)
    qkv_b, attn_o_b = attn_w_qkv.astype(BF16), attn_w_o.astype(BF16)
    hgrn_in_b, hgrn_o_b = hgrn_w_in.astype(BF16), hgrn_w_o.astype(BF16)
    for layer in range(depth):
        gains = norm_g[layer].reshape(N_SUBLAYERS, 1, d)
        x2 = _ffn_call(x2, mod[layer, :, 0], gains[0], _Stacked(ffn_in_b, (layer, 0)),
                       _Stacked(ffn_down_b, (layer, 0)), seq)
        j = layer // N_MIXERS
        if layer % N_MIXERS == 0:
            reps = QKV_COLS // ATTN_HEAD_DIM
            q2, k2, v2 = _qkv_call(
                x2, mod[layer, :, 1], gains[1], _Stacked(qkv_b, (j,)),
                jnp.tile(attn_q_gain[j], reps).reshape(1, QKV_COLS),
                jnp.tile(attn_k_gain[j], reps).reshape(1, QKV_COLS), seq)
            lambda_init = 0.8 - 0.6 * math.exp(-0.3 * layer)
            o2 = _attn_call(q2, k2, v2, bias_tiles, attn_lambda[j],
                            attn_subln_gain[j].reshape(1, ATTN_V_DIM), batch, seq, lambda_init)
            w_o = _Stacked(attn_o_b, (j,))
        else:
            q2, k2, lf2, v2, og2 = _hgrn_in_call(
                x2, mod[layer, :, 1], gains[1], _Stacked(hgrn_in_b, (j,)), hgrn_lb_logits, j, seq)
            o2 = _hgrn_rec_call(q2, k2, lf2, v2, og2,
                                hgrn_out_gain[j].reshape(HGRN_HEADS, 1, HGRN_DV), batch, seq)
            w_o = _Stacked(hgrn_o_b, (j,))
        x2 = _ffn_call(x2, mod[layer, :, 2], gains[2], _Stacked(ffn_in_b, (layer, 1)),
                       _Stacked(ffn_down_b, (layer, 1)), seq, mixer=(o2, mod[layer, :, 1], w_o))
    return x2.reshape(batch, seq, d)
```

```python
import functools
import math
from typing import NamedTuple

import numpy as np
import jax
import jax.numpy as jnp
from jax import lax
from jax.experimental import pallas as pl
from jax.experimental.pallas import tpu as pltpu

F32 = jnp.float32
BF16 = jnp.bfloat16

EPS = 1e-6
N_SUBLAYERS = 3
N_MIXERS = 2
ATTN_HEADS = 8
ATTN_HEAD_DIM = 64
ATTN_V_DIM = 128
NUM_BUCKETS = 32
MAX_DISTANCE = 128
HGRN_HEADS = 8
HGRN_DK = 128
HGRN_DV = 128

LANES = 128
VMEM_LIMIT = 56 * 1024 * 1024
NEG = -1e30
LOG2E = math.log2(math.e)

MXU_COLS = 256
ATTN_TILE = 256
QKV_COLS = MXU_COLS
FFN_COLS = 2 * MXU_COLS
HGRN_COLS = MXU_COLS
HGRN_CHUNK = 128
HGRN_HEADS_PER_STEP = 1
ADA_COLS = 4608
HGRN_INTERLEAVE = 16


def _cparams(n_axes):
    return pltpu.CompilerParams(
        dimension_semantics=("arbitrary",) * n_axes, vmem_limit_bytes=VMEM_LIMIT)


def _resident(shape):
    return pl.BlockSpec(shape, lambda *_: (0,) * len(shape), pipeline_mode=pl.Buffered(1))


class _Stacked(NamedTuple):
    array: jax.Array
    lead: tuple = ()

    @property
    def shape(self):
        return self.array.shape[len(self.lead):]

    def spec(self):
        lead, tail = self.lead, self.shape
        return pl.BlockSpec((None,) * len(lead) + tail, lambda *_: lead + (0,) * len(tail),
                            pipeline_mode=pl.Buffered(1))


def _sigmoid(z):
    return 1.0 / (1.0 + jnp.exp(-z))


def _dot(a, b):
    return jnp.dot(a, b, preferred_element_type=F32)


def _dot_nt(a, b):
    return lax.dot_general(a, b, (((1,), (1,)), ((), ())), preferred_element_type=F32)


def _dot_tn(a, b):
    return lax.dot_general(a, b, (((0,), (0,)), ((), ())), preferred_element_type=F32)


def _pipelined(stages):
    pending = None
    for produce, consume in stages:
        value = produce()
        if pending is not None:
            pending[1](pending[0])
        pending = (value, consume)
    if pending is not None:
        pending[1](pending[0])


def _col_pieces(total, width):
    return [(lo, min(lo + width, total)) for lo in range(0, total, width)]


def _store_heads(ref, lo, hi, value):
    for col in range(lo, hi, LANES):
        ref[col // LANES] = value[:, col - lo:col - lo + LANES]


def _head_major(heads, rows):
    return pl.BlockSpec((heads, rows, LANES), lambda i: (0, i, 0))


def _norm_mod(x, gain, mod_ref):
    ms = jnp.mean(x * x, axis=-1, keepdims=True)
    h = x * lax.rsqrt(ms + EPS) * gain
    return h * (1.0 + mod_ref[0, 1:2, :]) + mod_ref[0, 0:1, :]


def _ada_kernel(c_ref, w_ref, b_ref, o_ref):
    c = c_ref[...]
    ca = (c * _sigmoid(c)).astype(BF16)
    o_ref[0] = _dot(ca, w_ref[0].astype(BF16)) + b_ref[0]


def _ada_call(c, ada_w, ada_b):
    depth, d, n = ada_w.shape
    b = c.shape[0]
    tn = ADA_COLS
    assert n % tn == 0
    return pl.pallas_call(
        _ada_kernel,
        grid=(depth, n // tn),
        in_specs=[
            pl.BlockSpec((b, d), lambda l, j: (0, 0)),
            pl.BlockSpec((1, d, tn), lambda l, j: (l, 0, j)),
            pl.BlockSpec((1, 1, tn), lambda l, j: (l, 0, j)),
        ],
        out_specs=pl.BlockSpec((1, b, tn), lambda l, j: (l, 0, j)),
        out_shape=jax.ShapeDtypeStruct((depth, b, n), F32),
        compiler_params=_cparams(2),
        name="ada_mod",
    )(c, ada_w, ada_b.reshape(depth, 1, n))


def _ffn_kernel(*refs, d_ff, with_mixer):
    if with_mixer:
        (x_ref, mix_ref, mixmod_ref, wo_ref, mod_ref, g_ref, win_ref, wdn_ref, o_ref, act_ref) = refs
        mix = jnp.concatenate([mix_ref[hd] for hd in range(mix_ref.shape[0])], axis=1)
        x = x_ref[...] + mixmod_ref[0, 2:3, :] * _dot(mix, wo_ref[...])
    else:
        (x_ref, mod_ref, g_ref, win_ref, wdn_ref, o_ref, act_ref) = refs
        x = x_ref[...]
    h = _norm_mod(x, g_ref[...], mod_ref).astype(BF16)

    def stage(lo, hi):
        def produce():
            return _dot(h, win_ref[:, lo:hi]), _dot(h, win_ref[:, d_ff + lo:d_ff + hi])

        def consume(ab):
            a, b = ab
            act_ref[:, lo:hi] = (a * _sigmoid(a) * b).astype(BF16)

        return produce, consume

    _pipelined([stage(lo, hi) for lo, hi in _col_pieces(d_ff, FFN_COLS)])
    y = _dot(act_ref[...], wdn_ref[...])
    o_ref[...] = x + 0.5 * mod_ref[0, 2:3, :] * y


def _ffn_call(x2, mod3, gain, w_in, w_down, seq, mixer=None):
    t, d = x2.shape
    d_ff = w_down.shape[0]
    tm = min(1024, seq)
    assert t % tm == 0 and seq % tm == 0 and d_ff % MXU_COLS == 0
    per_seq = seq // tm
    row = pl.BlockSpec((tm, d), lambda i: (i, 0))
    mod_spec = pl.BlockSpec((1, 3, d), lambda i: (i // per_seq, 0, 0))
    operands, in_specs = [x2], [row]
    if mixer is not None:
        mix, mix_mod3, w_o = mixer
        operands += [mix, mix_mod3, w_o.array]
        in_specs += [_head_major(mix.shape[0], tm), mod_spec, w_o.spec()]
    operands += [mod3, gain, w_in.array, w_down.array]
    in_specs += [mod_spec, _resident((1, d)), w_in.spec(), w_down.spec()]
    return pl.pallas_call(
        functools.partial(_ffn_kernel, d_ff=d_ff, with_mixer=mixer is not None),
        grid=(t // tm,),
        in_specs=in_specs,
        out_specs=row,
        out_shape=jax.ShapeDtypeStruct((t, d), F32),
        scratch_shapes=[pltpu.VMEM((tm, d_ff), BF16)],
        compiler_params=_cparams(1),
        name="ffn_half_step",
    )(*operands)


def _group_mean_sq(z, ones_blk):
    return _dot((z * z).astype(BF16), ones_blk)


def _qkv_kernel(x_ref, mod_ref, g_ref, w_ref, qg_ref, kg_ref, q_ref, k_ref, v_ref, *, d):
    h = _norm_mod(x_ref[...], g_ref[...], mod_ref).astype(BF16)
    tn = QKV_COLS
    r = lax.broadcasted_iota(jnp.int32, (tn, tn), 0) // ATTN_HEAD_DIM
    c = lax.broadcasted_iota(jnp.int32, (tn, tn), 1) // ATTN_HEAD_DIM
    ones_blk = jnp.where(r == c, 1.0 / ATTN_HEAD_DIM, 0.0).astype(BF16)
    q_gain = qg_ref[...] * (ATTN_HEAD_DIM ** -0.5 * LOG2E)
    k_gain = kg_ref[...]

    def normed(w_lo, out_ref, lo, hi, gain):
        def consume(z):
            _store_heads(out_ref, lo, hi,
                         (z * lax.rsqrt(_group_mean_sq(z, ones_blk) + EPS) * gain).astype(BF16))

        return (lambda: _dot(h, w_ref[:, w_lo + lo:w_lo + hi])), consume

    def plain(w_lo, out_ref, lo, hi):
        def consume(z):
            _store_heads(out_ref, lo, hi, z.astype(BF16))

        return (lambda: _dot(h, w_ref[:, w_lo + lo:w_lo + hi])), consume

    stages = []
    for lo, hi in _col_pieces(d, tn):
        stages += [normed(0, q_ref, lo, hi, q_gain), normed(d, k_ref, lo, hi, k_gain),
                   plain(2 * d, v_ref, lo, hi)]
    _pipelined(stages)


def _qkv_call(x2, mod3, gain, w_qkv, q_gain2, k_gain2, seq):
    t, d = x2.shape
    tm = min(1024, seq)
    assert t % tm == 0 and seq % tm == 0
    per_seq = seq // tm
    heads = d // LANES
    out = jax.ShapeDtypeStruct((heads, t, LANES), BF16)
    row = pl.BlockSpec((tm, d), lambda i: (i, 0))
    out_spec = _head_major(heads, tm)
    return pl.pallas_call(
        functools.partial(_qkv_kernel, d=d),
        grid=(t // tm,),
        in_specs=[
            row,
            pl.BlockSpec((1, 3, d), lambda i: (i // per_seq, 0, 0)),
            _resident((1, d)),
            w_qkv.spec(),
            _resident((1, QKV_COLS)),
            _resident((1, QKV_COLS)),
        ],
        out_specs=[out_spec, out_spec, out_spec],
        out_shape=[out, out, out],
        compiler_params=_cparams(1),
        name="attn_qkv_proj",
    )(x2, mod3, gain, w_qkv.array, q_gain2, k_gain2)


def _bucket_thresholds():
    max_exact = NUM_BUCKETS // 2
    dist = np.arange(0, 4 * MAX_DISTANCE)
    d_f = np.maximum(dist, 1).astype(np.float32)
    large = max_exact + (np.log(d_f / np.float32(max_exact)) / np.float32(math.log(MAX_DISTANCE / max_exact))
                         * np.float32(NUM_BUCKETS - max_exact)).astype(np.int32)
    bucket = np.where(dist < max_exact, dist, np.minimum(large, NUM_BUCKETS - 1))
    assert np.all(np.diff(bucket) >= 0) and bucket[-1] == NUM_BUCKETS - 1
    return [int(np.argmax(bucket >= b)) for b in range(NUM_BUCKETS)]


def _bias_kernel(rb_ref, o_ref, *, thresholds, tile):
    h = pl.program_id(0)
    i = lax.broadcasted_iota(jnp.int32, (tile, tile), 0)
    j = lax.broadcasted_iota(jnp.int32, (tile, tile), 1)
    for off in range(3):
        dist = off * tile + i - j
        val = jnp.full((tile, tile), rb_ref[0, h], F32)
        for b in range(1, NUM_BUCKETS):
            val = jnp.where(dist >= thresholds[b], rb_ref[b, h], val)
        o_ref[0, off] = jnp.where(dist >= 0, val * LOG2E, NEG)


def _bias_call(rel_bias, tile):
    thresholds = _bucket_thresholds()
    assert thresholds[NUM_BUCKETS - 1] <= tile + 1
    heads = rel_bias.shape[1]
    return pl.pallas_call(
        functools.partial(_bias_kernel, thresholds=thresholds, tile=tile),
        grid=(heads,),
        in_specs=[pl.BlockSpec(memory_space=pltpu.SMEM)],
        out_specs=pl.BlockSpec((1, 3, tile, tile), lambda h: (h, 0, 0, 0)),
        out_shape=jax.ShapeDtypeStruct((heads, 3, tile, tile), F32),
        compiler_params=_cparams(1),
        name="attn_rel_bias_tiles",
    )(rel_bias)


def _attn_kernel(q_ref, k_ref, v_ref, bias_ref, lam_ref, sg_ref, o_ref, v1_ref,
                 s0_ref, s1_ref, p0_ref, p1_ref, m0_ref, m1_ref, mf0_ref, mf1_ref,
                 *, tile, lambda_init):
    lam = lam_ref[...]
    lam_full = (jnp.exp(jnp.sum(lam[0:1] * lam[1:2], axis=-1, keepdims=True))
                - jnp.exp(jnp.sum(lam[2:3] * lam[3:4], axis=-1, keepdims=True)) + lambda_init)
    scratch = (v1_ref, (s0_ref, s1_ref), (p0_ref, p1_ref), (m0_ref, m1_ref), (mf0_ref, mf1_ref))

    def one_head(hd, carry):
        _attn_head(q_ref.at[hd], k_ref.at[hd], v_ref.at[hd], bias_ref.at[hd], sg_ref, o_ref.at[hd],
                   scratch, lam_full, tile=tile, lambda_init=lambda_init)
        return carry

    lax.fori_loop(0, q_ref.shape[0], one_head, 0)


def _attn_head(q_ref, k_ref, v_ref, bias_ref, sg_ref, o_ref, scratch, lam_full, *, tile, lambda_init):
    v1_ref, s_refs, p_refs, m_refs, mfar_refs = scratch
    seq = q_ref.shape[0]
    dv = ATTN_V_DIM
    n_tiles = seq // tile
    v1_ref[:, :dv] = v_ref[...]
    v1_ref[:, dv:] = jnp.ones((seq, dv), BF16)
    lane = lax.broadcasted_iota(jnp.int32, (tile, dv), 1)

    def stacked_q(i):
        q = q_ref[i * tile:(i + 1) * tile, :]
        zero = jnp.zeros_like(q)
        return jnp.concatenate([jnp.where(lane < ATTN_HEAD_DIM, q, zero),
                                jnp.where(lane >= ATTN_HEAD_DIM, q, zero)], axis=0)

    far_bias = bias_ref[2, 0:1, 0:1]
    running_max = {}

    def is_far(i, j):
        return i - j >= 2

    def scores(i, j, qs):
        s = _dot_nt(qs, k_ref[j * tile:(j + 1) * tile, :])
        if not is_far(i, j):
            bt = bias_ref[i - j]
            s = s + jnp.concatenate([bt, bt], axis=0)
        s_refs[i % 2][:, j * tile:(j + 1) * tile] = s
        block_max = functools.reduce(
            jnp.maximum, [s[:, c:c + LANES] for c in range(0, tile, LANES)])
        key = (i, is_far(i, j))
        running_max[key] = (jnp.maximum(running_max[key], block_max)
                            if key in running_max else block_max)

    def row_max(i):
        m = jnp.max(running_max.pop((i, False)), axis=-1, keepdims=True)
        if (i, True) in running_max:
            m_far = jnp.max(running_max.pop((i, True)), axis=-1, keepdims=True) + far_bias
            m = jnp.maximum(m, m_far)
            mfar_refs[i % 2][...] = jnp.broadcast_to(m - far_bias, mfar_refs[i % 2].shape)
        m_refs[i % 2][...] = jnp.broadcast_to(m, m_refs[i % 2].shape)

    def probs(i, j):
        m = (mfar_refs if is_far(i, j) else m_refs)[i % 2][...]
        m_wide = jnp.concatenate([m] * (tile // LANES), axis=1)
        cols = slice(j * tile, (j + 1) * tile)
        p_refs[i % 2][:, cols] = jnp.exp2(s_refs[i % 2][:, cols] - m_wide).astype(BF16)

    def finish(i):
        n_keys = (i + 1) * tile
        acc = _dot(p_refs[i % 2][:, :n_keys], v1_ref[:n_keys, :])
        o_all = acc[:, :dv] / acc[:, dv:]
        o = o_all[:tile] - lam_full * o_all[tile:]
        ms = jnp.mean(o * o, axis=-1, keepdims=True)
        o = o * lax.rsqrt(ms + EPS) * sg_ref[...] * (1.0 - lambda_init)
        o_ref[i * tile:(i + 1) * tile, :] = o.astype(BF16)

    def score_and_prob_steps(i_scores, i_probs):
        qs = stacked_q(i_scores) if i_scores >= 0 else None
        for j in range(max(i_scores, i_probs) + 1):
            if j <= i_scores:
                scores(i_scores, j, qs)
            if j <= i_probs:
                probs(i_probs, j)
        if i_scores >= 0:
            row_max(i_scores)

    last = n_tiles - 1
    score_and_prob_steps(last, -1)
    score_and_prob_steps(last - 1, last)
    for i in range(last, -1, -1):
        finish(i)
        score_and_prob_steps(i - 2, i - 1)


def _attn_call(q2, k2, v2, bias_tiles, lam, subln_gain2, batch, seq, lambda_init):
    heads, t, _ = q2.shape
    tile = ATTN_TILE
    assert seq % tile == 0
    blk = pl.BlockSpec((heads, seq, ATTN_V_DIM), lambda b: (0, b, 0))
    return pl.pallas_call(
        functools.partial(_attn_kernel, tile=tile, lambda_init=lambda_init),
        grid=(batch,),
        in_specs=[
            blk, blk, blk,
            _resident(bias_tiles.shape),
            _resident(lam.shape),
            _resident((1, ATTN_V_DIM)),
        ],
        out_specs=blk,
        out_shape=jax.ShapeDtypeStruct((heads, t, ATTN_V_DIM), BF16),
        scratch_shapes=[
            pltpu.VMEM((seq, 2 * ATTN_V_DIM), BF16),
            pltpu.VMEM((2 * tile, seq), F32),
            pltpu.VMEM((2 * tile, seq), F32),
            pltpu.VMEM((2 * tile, seq), BF16),
            pltpu.VMEM((2 * tile, seq), BF16),
            pltpu.VMEM((2 * tile, LANES), F32),
            pltpu.VMEM((2 * tile, LANES), F32),
            pltpu.VMEM((2 * tile, LANES), F32),
            pltpu.VMEM((2 * tile, LANES), F32),
        ],
        compiler_params=_cparams(1),
        name="diff_attention",
    )(q2, k2, v2, bias_tiles, lam, subln_gain2)


def _hgrn_in_kernel(x_ref, mod_ref, g_ref, w_ref, lbl_ref, q_ref, k_ref, lf_ref, v_ref, og_ref,
                    *, d, layer_j):
    h = _norm_mod(x_ref[...], g_ref[...], mod_ref).astype(BF16)
    logits = lbl_ref[...]
    e = jnp.exp(logits - jnp.max(logits, axis=0, keepdims=True))
    p = e / jnp.sum(e, axis=0, keepdims=True)
    lb = jnp.zeros((1, d), F32)
    for i in range(1, layer_j + 1):
        lb = lb + p[i:i + 1, :]
    log_lb = jnp.log(lb)
    log_1m_lb = jnp.log1p(-lb)

    def dot_cols(w_lo, lo, hi):
        return lambda: _dot(h, w_ref[:, w_lo + lo:w_lo + hi])

    def forget_stage(lo, hi):
        def consume(f):
            ef = jnp.exp(-jnp.abs(f))
            one_plus = 1.0 + ef
            log_sig = jnp.minimum(f, 0.0) - jnp.log(one_plus)
            sig_neg = jnp.where(f >= 0.0, ef, 1.0) / one_plus
            if layer_j == 0:
                _store_heads(lf_ref, lo, hi, log_sig)
                _store_heads(k_ref, lo, hi, sig_neg)
            else:
                a = log_lb[:, lo:hi]
                b = log_1m_lb[:, lo:hi] + log_sig
                _store_heads(lf_ref, lo, hi,
                             jnp.maximum(a, b) + jnp.log(1.0 + jnp.exp(-jnp.abs(a - b))))
                _store_heads(k_ref, lo, hi, (1.0 - lb[:, lo:hi]) * sig_neg)

        return dot_cols(d, lo, hi), consume

    def query_stage(lo, hi):
        def consume(z):
            _store_heads(q_ref, lo, hi, z)

        return dot_cols(0, lo, hi), consume

    def value_stage(lo, hi):
        def consume(z):
            _store_heads(v_ref, lo, hi, z.astype(BF16))

        return dot_cols(2 * d, lo, hi), consume

    def gate_stage(lo, hi):
        def consume(g):
            _store_heads(og_ref, lo, hi, g * _sigmoid(g))

        return dot_cols(3 * d, lo, hi), consume

    stages = []
    for lo, hi in _col_pieces(d, HGRN_COLS):
        stages += [forget_stage(lo, hi), query_stage(lo, hi), gate_stage(lo, hi), value_stage(lo, hi)]
    _pipelined(stages)


def _hgrn_in_call(x2, mod3, gain, w_in, lb_logits, layer_j, seq):
    t, d = x2.shape
    tm = min(512, seq)
    assert t % tm == 0 and seq % tm == 0
    per_seq = seq // tm
    heads = d // LANES
    row = pl.BlockSpec((tm, d), lambda i: (i, 0))
    out_spec = _head_major(heads, tm)
    f32_out = jax.ShapeDtypeStruct((heads, t, LANES), F32)
    return pl.pallas_call(
        functools.partial(_hgrn_in_kernel, d=d, layer_j=layer_j),
        grid=(t // tm,),
        in_specs=[
            row,
            pl.BlockSpec((1, 3, d), lambda i: (i // per_seq, 0, 0)),
            _resident((1, d)),
            w_in.spec(),
            _resident(lb_logits.shape),
        ],
        out_specs=[out_spec] * 5,
        out_shape=[f32_out, f32_out, f32_out, jax.ShapeDtypeStruct((heads, t, LANES), BF16), f32_out],
        compiler_params=_cparams(1),
        name="hgrn_in_proj",
    )(x2, mod3, gain, w_in.array, lb_logits)


def _rows_at(g_cum, half):
    c = g_cum.shape[0]
    if half >= 4:
        blk = 2 * half
        return jnp.concatenate(
            [jnp.broadcast_to(g_cum[p * blk + half - 1:p * blk + half, :], (blk, g_cum.shape[1]))
             for p in range(c // blk)], axis=0)
    row = lax.broadcasted_iota(jnp.int32, g_cum.shape, 0)
    if half == 2:
        r4 = row % 4
        up1 = pltpu.roll(g_cum, c - 1, 0)
        dn1 = pltpu.roll(g_cum, 1, 0)
        dn2 = pltpu.roll(g_cum, 2, 0)
        return jnp.where(r4 == 0, up1, jnp.where(r4 == 1, g_cum, jnp.where(r4 == 2, dn1, dn2)))
    assert half == 1
    return jnp.where(row % 2 == 0, g_cum, pltpu.roll(g_cum, 1, 0))


def _round_robin(generators, width):
    pending = list(generators)
    active = []
    while pending or active:
        while pending and len(active) < width:
            active.append(pending.pop(0))
        for gen in list(active):
            try:
                next(gen)
            except StopIteration:
                active.remove(gen)


def _hgrn_rec_kernel(q_ref, k_ref, lf_ref, v_ref, og_ref, gain_ref, o_ref, *scratch, chunk):
    def one_head(hd, carry):
        _hgrn_rec_head(q_ref.at[hd], k_ref.at[hd], lf_ref.at[hd], v_ref.at[hd], og_ref.at[hd],
                       gain_ref.at[hd], o_ref.at[hd], *scratch, chunk=chunk)
        return carry

    lax.fori_loop(0, q_ref.shape[0], one_head, 0)


def _hgrn_rec_head(q_ref, k_ref, lf_ref, v_ref, og_ref, gain_ref, o_ref,
                   qe_ref, a_ref, u_ref, dl_ref, st_ref, *, chunk):
    seq = q_ref.shape[0]
    c = chunk
    n_chunks = seq // c
    ri = lax.broadcasted_iota(jnp.int32, (c, c), 0)
    ci = lax.broadcasted_iota(jnp.int32, (c, c), 1)
    tri = jnp.where(ri >= ci, 1.0, 0.0).astype(BF16)
    pair_key = jnp.where(ri >= ci, ri ^ ci, -1)
    row = lax.broadcasted_iota(jnp.int32, (c, HGRN_DK), 0)

    def neg_abs(z):
        return lax.bitcast_convert_type(
            lax.bitcast_convert_type(z, jnp.int32) | jnp.int32(-2 ** 31), F32)

    def intra(n):
        sl = slice(n * c, (n + 1) * c)
        g = lf_ref[sl, :]
        q = q_ref[sl, :]
        k = k_ref[sl, :]
        g_hi = g.astype(BF16)
        r1 = g - g_hi.astype(F32)
        g_mid = r1.astype(BF16)
        g_lo = (r1 - g_mid.astype(F32)).astype(BF16)
        g2 = (_dot(tri, g_hi) + (_dot(tri, g_mid) + _dot(tri, g_lo))) * LOG2E
        g2_last = g2[c - 1:c, :]
        yield
        qe_ref[sl, :] = (q * jnp.exp2(g2)).astype(BF16)
        k_dec = (k * jnp.exp2(g2_last - g2)).astype(BF16)
        u_ref[n] = _dot_tn(v_ref[sl, :], k_dec)
        dl_ref[n] = jnp.exp2(g2_last)
        scores = jnp.where(pair_key == 0, _dot_nt(q.astype(BF16), k.astype(BF16)), 0.0)
        yield
        half = 1
        while half < c:
            e = jnp.exp2(neg_abs(g2 - _rows_at(g2, half)))
            if half < 8:
                level = _dot_nt((q * e).astype(BF16), (k * e).astype(BF16))
                scores = jnp.where(pair_key >= half, level, scores)
            else:
                qk = jnp.concatenate(
                    [(q if (b % 2) else k)[b * half:(b + 1) * half] for b in range(c // half)], axis=0)
                zf = qk * e
                z = zf.astype(BF16)
                z_right = jnp.concatenate(
                    [zf[b * half:(b + 1) * half] for b in range(1, c // half, 2)], axis=0).astype(BF16)
                level = _dot_nt(z_right, z)
                pieces = []
                for b in range(c // half):
                    rows = slice(b * half, (b + 1) * half)
                    piece = scores[rows]
                    if b % 2:
                        lv = level[(b // 2) * half:(b // 2 + 1) * half]
                        piece = jnp.where(pair_key[rows] >= half, lv, piece)
                    pieces.append(piece)
                scores = jnp.concatenate(pieces, axis=0)
            half *= 2
            yield
        a_ref[n] = scores.astype(BF16)

    def outputs(n):
        sl = slice(n * c, (n + 1) * c)
        o = _dot_nt(qe_ref[sl, :], st_ref[n]) + _dot(a_ref[n], v_ref[sl, :])
        yield
        ms = jnp.mean(o * o, axis=-1, keepdims=True)
        o_ref[sl, :] = (o * lax.rsqrt(ms + EPS) * gain_ref[...] * og_ref[sl, :]).astype(BF16)

    _round_robin([intra(n) for n in range(n_chunks)], HGRN_INTERLEAVE)

    state = jnp.zeros((HGRN_DV, HGRN_DK), F32)
    for n in range(n_chunks):
        st_ref[n] = state.astype(BF16)
        state = dl_ref[n] * state + u_ref[n]

    _round_robin([outputs(n) for n in range(n_chunks)], HGRN_INTERLEAVE)


def _hgrn_rec_call(q2, k2, lf2, v2, og2, out_gain2, batch, seq):
    heads, t, _ = q2.shape
    c = HGRN_CHUNK
    assert seq % c == 0
    n_chunks = seq // c
    per_step = HGRN_HEADS_PER_STEP
    assert heads % per_step == 0
    blk = pl.BlockSpec((per_step, seq, HGRN_DV), lambda b, g: (g, b, 0))
    return pl.pallas_call(
        functools.partial(_hgrn_rec_kernel, chunk=c),
        grid=(batch, heads // per_step),
        in_specs=[blk, blk, blk, blk, blk,
                  pl.BlockSpec((per_step, 1, HGRN_DV), lambda b, g: (g, 0, 0))],
        out_specs=blk,
        out_shape=jax.ShapeDtypeStruct((heads, t, HGRN_DV), BF16),
        scratch_shapes=[
            pltpu.VMEM((seq, HGRN_DK), BF16),
            pltpu.VMEM((n_chunks, c, c), BF16),
            pltpu.VMEM((n_chunks, HGRN_DV, HGRN_DK), F32),
            pltpu.VMEM((n_chunks, 1, HGRN_DK), F32),
            pltpu.VMEM((n_chunks, HGRN_DV, HGRN_DK), BF16),
        ],
        compiler_params=_cparams(2),
        name="hgrn_recurrence",
    )(q2, k2, lf2, v2, og2, out_gain2)


def kernel(x, c, ada_w, ada_b, norm_g, ffn_w_in, ffn_w_down, attn_w_qkv, attn_w_o, attn_q_gain,
           attn_k_gain, attn_lambda, attn_subln_gain, rel_bias, hgrn_w_in, hgrn_w_o,
           hgrn_out_gain, hgrn_lb_logits):
    batch, seq, d = x.shape
    depth = ada_w.shape[0]
    mod = _ada_call(c, ada_w, ada_b).reshape(depth, batch, N_SUBLAYERS, 3, d)
    bias_tiles = _bias_call(rel_bias, ATTN_TILE)
    x2 = x.reshape(batch * seq, d)
    ffn_in_b, ffn_down_b = ffn_w_in.astype(BF16), ffn_w_down.astype(BF16)
    qkv_b, attn_o_b = attn_w_qkv.astype(BF16), attn_w_o.astype(BF16)
    hgrn_in_b, hgrn_o_b = hgrn_w_in.astype(BF16), hgrn_w_o.astype(BF16)
    for layer in range(depth):
        gains = norm_g[layer].reshape(N_SUBLAYERS, 1, d)
        x2 = _ffn_call(x2, mod[layer, :, 0], gains[0], _Stacked(ffn_in_b, (layer, 0)),
                       _Stacked(ffn_down_b, (layer, 0)), seq)
        j = layer // N_MIXERS
        if layer % N_MIXERS == 0:
            reps = QKV_COLS // ATTN_HEAD_DIM
            q2, k2, v2 = _qkv_call(
                x2, mod[layer, :, 1], gains[1], _Stacked(qkv_b, (j,)),
                jnp.tile(attn_q_gain[j], reps).reshape(1, QKV_COLS),
                jnp.tile(attn_k_gain[j], reps).reshape(1, QKV_COLS), seq)
            lambda_init = 0.8 - 0.6 * math.exp(-0.3 * layer)
            o2 = _attn_call(q2, k2, v2, bias_tiles, attn_lambda[j],
                            attn_subln_gain[j].reshape(1, ATTN_V_DIM), batch, seq, lambda_init)
            w_o = _Stacked(attn_o_b, (j,))
        else:
            q2, k2, lf2, v2, og2 = _hgrn_in_call(
                x2, mod[layer, :, 1], gains[1], _Stacked(hgrn_in_b, (j,)), hgrn_lb_logits, j, seq)
            o2 = _hgrn_rec_call(q2, k2, lf2, v2, og2,
                                hgrn_out_gain[j].reshape(HGRN_HEADS, 1, HGRN_DV), batch, seq)
            w_o = _Stacked(hgrn_o_b, (j,))
        x2 = _ffn_call(x2, mod[layer, :, 2], gains[2], _Stacked(ffn_in_b, (layer, 1)),
                       _Stacked(ffn_down_b, (layer, 1)), seq, mixer=(o2, mod[layer, :, 1], w_o))
    return x2.reshape(batch, seq, d)
```

```python
import functools
import math
from typing import NamedTuple

import numpy as np
import jax
import jax.numpy as jnp
from jax import lax
from jax.experimental import pallas as pl
from jax.experimental.pallas import tpu as pltpu

F32 = jnp.float32
BF16 = jnp.bfloat16

EPS = 1e-6
N_SUBLAYERS = 3
N_MIXERS = 2
ATTN_HEADS = 8
ATTN_HEAD_DIM = 64
ATTN_V_DIM = 128
NUM_BUCKETS = 32
MAX_DISTANCE = 128
HGRN_HEADS = 8
HGRN_DK = 128
HGRN_DV = 128

LANES = 128
VMEM_LIMIT = 60 * 1024 * 1024
NEG = -1e30
LOG2E = math.log2(math.e)

MXU_COLS = 256
ATTN_TILE = 256
QKV_COLS = MXU_COLS
FFN_COLS = 2 * MXU_COLS
HGRN_COLS = MXU_COLS
HGRN_CHUNK = 128
HGRN_HEADS_PER_STEP = 1
ADA_COLS = 4608
HGRN_INTERLEAVE = 16


def _cparams(n_axes):
    return pltpu.CompilerParams(
        dimension_semantics=("arbitrary",) * n_axes, vmem_limit_bytes=VMEM_LIMIT)


def _resident(shape):
    return pl.BlockSpec(shape, lambda *_: (0,) * len(shape), pipeline_mode=pl.Buffered(1))


class _Stacked(NamedTuple):
    array: jax.Array
    lead: tuple = ()

    @property
    def shape(self):
        return self.array.shape[len(self.lead):]

    def spec(self):
        lead, tail = self.lead, self.shape
        return pl.BlockSpec((None,) * len(lead) + tail, lambda *_: lead + (0,) * len(tail),
                            pipeline_mode=pl.Buffered(1))


def _sigmoid(z):
    return 1.0 / (1.0 + jnp.exp(-z))


def _dot(a, b):
    return jnp.dot(a, b, preferred_element_type=F32)


def _dot_nt(a, b):
    return lax.dot_general(a, b, (((1,), (1,)), ((), ())), preferred_element_type=F32)


def _dot_tn(a, b):
    return lax.dot_general(a, b, (((0,), (0,)), ((), ())), preferred_element_type=F32)


def _pipelined(stages):
    pending = None
    for produce, consume in stages:
        value = produce()
        if pending is not None:
            pending[1](pending[0])
        pending = (value, consume)
    if pending is not None:
        pending[1](pending[0])


def _col_pieces(total, width):
    return [(lo, min(lo + width, total)) for lo in range(0, total, width)]


def _store_heads(ref, lo, hi, value):
    for col in range(lo, hi, LANES):
        ref[col // LANES] = value[:, col - lo:col - lo + LANES]


def _head_major(heads, rows):
    return pl.BlockSpec((heads, rows, LANES), lambda i: (0, i, 0))


def _norm_mod(x, gain, mod_ref):
    ms = jnp.mean(x * x, axis=-1, keepdims=True)
    h = x * lax.rsqrt(ms + EPS) * gain
    return h * (1.0 + mod_ref[0, 1:2, :]) + mod_ref[0, 0:1, :]


def _ada_kernel(c_ref, w_ref, b_ref, o_ref):
    c = c_ref[...]
    ca = (c * _sigmoid(c)).astype(BF16)
    o_ref[0] = _dot(ca, w_ref[0].astype(BF16)) + b_ref[0]


def _ada_call(c, ada_w, ada_b):
    depth, d, n = ada_w.shape
    b = c.shape[0]
    tn = ADA_COLS
    assert n % tn == 0
    return pl.pallas_call(
        _ada_kernel,
        grid=(depth, n // tn),
        in_specs=[
            pl.BlockSpec((b, d), lambda l, j: (0, 0)),
            pl.BlockSpec((1, d, tn), lambda l, j: (l, 0, j)),
            pl.BlockSpec((1, 1, tn), lambda l, j: (l, 0, j)),
        ],
        out_specs=pl.BlockSpec((1, b, tn), lambda l, j: (l, 0, j)),
        out_shape=jax.ShapeDtypeStruct((depth, b, n), F32),
        compiler_params=_cparams(2),
        name="ada_mod",
    )(c, ada_w, ada_b.reshape(depth, 1, n))


def _ffn_kernel(*refs, d_ff, with_mixer, n_cast):
    n_in = (8 if with_mixer else 5) + n_cast
    ins, outs, act_ref = refs[:n_in], refs[n_in:-1], refs[-1]
    if with_mixer:
        x_ref, mix_ref, mixmod_ref, wo_ref, mod_ref, g_ref, win_ref, wdn_ref = ins[:8]
    else:
        x_ref, mod_ref, g_ref, win_ref, wdn_ref = ins[:5]
    o_ref = outs[0]
    for src_ref, dst_ref in zip(ins[n_in - n_cast:], outs[1:]):
        dst_ref[...] = src_ref[...].astype(BF16)

    if with_mixer:
        mix = jnp.concatenate([mix_ref[hd] for hd in range(mix_ref.shape[0])], axis=1)
        x = x_ref[...] + mixmod_ref[0, 2:3, :] * _dot(mix, wo_ref[...])
    else:
        x = x_ref[...]
    h = _norm_mod(x, g_ref[...], mod_ref).astype(BF16)

    def stage(lo, hi):
        def produce():
            return _dot(h, win_ref[:, lo:hi]), _dot(h, win_ref[:, d_ff + lo:d_ff + hi])

        def consume(ab):
            a, b = ab
            act_ref[:, lo:hi] = (a * _sigmoid(a) * b).astype(BF16)

        return produce, consume

    _pipelined([stage(lo, hi) for lo, hi in _col_pieces(d_ff, FFN_COLS)])
    y = _dot(act_ref[...], wdn_ref[...])
    o_ref[...] = x + 0.5 * mod_ref[0, 2:3, :] * y


def _ffn_call(x2, mod3, gain, w_in, w_down, seq, mixer=None, cast_next=()):
    t, d = x2.shape
    d_ff = w_down.shape[0]
    tm = min(1024, seq)
    assert t % tm == 0 and seq % tm == 0 and d_ff % MXU_COLS == 0
    steps = t // tm
    per_seq = seq // tm
    row = pl.BlockSpec((tm, d), lambda i: (i, 0))
    mod_spec = pl.BlockSpec((1, 3, d), lambda i: (i // per_seq, 0, 0))
    operands, in_specs = [x2], [row]
    if mixer is not None:
        mix, mix_mod3, w_o = mixer
        operands += [mix, mix_mod3, w_o.array]
        in_specs += [_head_major(mix.shape[0], tm), mod_spec, w_o.spec()]
    operands += [mod3, gain, w_in.array, w_down.array]
    in_specs += [mod_spec, _resident((1, d)), w_in.spec(), w_down.spec()]
    out_specs, out_shapes = [row], [jax.ShapeDtypeStruct((t, d), F32)]
    for w in cast_next:
        n_rows, n_cols = w.shape
        assert n_rows % (steps * 16) == 0
        lead, blk_rows = w.lead, n_rows // steps
        operands.append(w.array)
        in_specs.append(pl.BlockSpec((None,) * len(lead) + (blk_rows, n_cols),
                                     lambda i, lead=lead: lead + (i, 0)))
        out_specs.append(pl.BlockSpec((blk_rows, n_cols), lambda i: (i, 0)))
        out_shapes.append(jax.ShapeDtypeStruct((n_rows, n_cols), BF16))
    outs = pl.pallas_call(
        functools.partial(_ffn_kernel, d_ff=d_ff, with_mixer=mixer is not None,
                          n_cast=len(cast_next)),
        grid=(steps,),
        in_specs=in_specs,
        out_specs=out_specs,
        out_shape=out_shapes,
        scratch_shapes=[pltpu.VMEM((tm, d_ff), BF16)],
        compiler_params=_cparams(1),
        name="ffn_half_step",
    )(*operands)
    return outs[0], [_Stacked(w) for w in outs[1:]]


def _group_mean_sq(z, ones_blk):
    return _dot((z * z).astype(BF16), ones_blk)


def _qkv_kernel(x_ref, mod_ref, g_ref, w_ref, qg_ref, kg_ref, q_ref, k_ref, v_ref, *, d):
    h = _norm_mod(x_ref[...], g_ref[...], mod_ref).astype(BF16)
    tn = QKV_COLS
    r = lax.broadcasted_iota(jnp.int32, (tn, tn), 0) // ATTN_HEAD_DIM
    c = lax.broadcasted_iota(jnp.int32, (tn, tn), 1) // ATTN_HEAD_DIM
    ones_blk = jnp.where(r == c, 1.0 / ATTN_HEAD_DIM, 0.0).astype(BF16)
    q_gain = qg_ref[...] * (ATTN_HEAD_DIM ** -0.5 * LOG2E)
    k_gain = kg_ref[...]

    def normed(w_lo, out_ref, lo, hi, gain):
        def consume(z):
            _store_heads(out_ref, lo, hi,
                         (z * lax.rsqrt(_group_mean_sq(z, ones_blk) + EPS) * gain).astype(BF16))

        return (lambda: _dot(h, w_ref[:, w_lo + lo:w_lo + hi])), consume

    def plain(w_lo, out_ref, lo, hi):
        def consume(z):
            _store_heads(out_ref, lo, hi, z.astype(BF16))

        return (lambda: _dot(h, w_ref[:, w_lo + lo:w_lo + hi])), consume

    stages = []
    for lo, hi in _col_pieces(d, tn):
        stages += [normed(0, q_ref, lo, hi, q_gain), normed(d, k_ref, lo, hi, k_gain),
                   plain(2 * d, v_ref, lo, hi)]
    _pipelined(stages)


def _qkv_call(x2, mod3, gain, w_qkv, q_gain2, k_gain2, seq):
    t, d = x2.shape
    tm = min(1024, seq)
    assert t % tm == 0 and seq % tm == 0
    per_seq = seq // tm
    heads = d // LANES
    out = jax.ShapeDtypeStruct((heads, t, LANES), BF16)
    row = pl.BlockSpec((tm, d), lambda i: (i, 0))
    out_spec = _head_major(heads, tm)
    return pl.pallas_call(
        functools.partial(_qkv_kernel, d=d),
        grid=(t // tm,),
        in_specs=[
            row,
            pl.BlockSpec((1, 3, d), lambda i: (i // per_seq, 0, 0)),
            _resident((1, d)),
            w_qkv.spec(),
            _resident((1, QKV_COLS)),
            _resident((1, QKV_COLS)),
        ],
        out_specs=[out_spec, out_spec, out_spec],
        out_shape=[out, out, out],
        compiler_params=_cparams(1),
        name="attn_qkv_proj",
    )(x2, mod3, gain, w_qkv.array, q_gain2, k_gain2)


def _bucket_thresholds():
    max_exact = NUM_BUCKETS // 2
    dist = np.arange(0, 4 * MAX_DISTANCE)
    d_f = np.maximum(dist, 1).astype(np.float32)
    large = max_exact + (np.log(d_f / np.float32(max_exact)) / np.float32(math.log(MAX_DISTANCE / max_exact))
                         * np.float32(NUM_BUCKETS - max_exact)).astype(np.int32)
    bucket = np.where(dist < max_exact, dist, np.minimum(large, NUM_BUCKETS - 1))
    assert np.all(np.diff(bucket) >= 0) and bucket[-1] == NUM_BUCKETS - 1
    return [int(np.argmax(bucket >= b)) for b in range(NUM_BUCKETS)]


def _bias_kernel(rb_ref, o_ref, *, thresholds, tile):
    h = pl.program_id(0)
    i = lax.broadcasted_iota(jnp.int32, (tile, tile), 0)
    j = lax.broadcasted_iota(jnp.int32, (tile, tile), 1)
    for off in range(3):
        dist = off * tile + i - j
        val = jnp.full((tile, tile), rb_ref[0, h], F32)
        for b in range(1, NUM_BUCKETS):
            val = jnp.where(dist >= thresholds[b], rb_ref[b, h], val)
        o_ref[0, off] = jnp.where(dist >= 0, val * LOG2E, NEG)


def _bias_call(rel_bias, tile):
    thresholds = _bucket_thresholds()
    assert thresholds[NUM_BUCKETS - 1] <= tile + 1
    heads = rel_bias.shape[1]
    return pl.pallas_call(
        functools.partial(_bias_kernel, thresholds=thresholds, tile=tile),
        grid=(heads,),
        in_specs=[pl.BlockSpec(memory_space=pltpu.SMEM)],
        out_specs=pl.BlockSpec((1, 3, tile, tile), lambda h: (h, 0, 0, 0)),
        out_shape=jax.ShapeDtypeStruct((heads, 3, tile, tile), F32),
        compiler_params=_cparams(1),
        name="attn_rel_bias_tiles",
    )(rel_bias)


def _attn_kernel(q_ref, k_ref, v_ref, bias_ref, lam_ref, sg_ref, o_ref, v1_ref,
                 s0_ref, s1_ref, p0_ref, p1_ref, m0_ref, m1_ref, mf0_ref, mf1_ref,
                 *, tile, lambda_init):
    lam = lam_ref[...]
    lam_full = (jnp.exp(jnp.sum(lam[0:1] * lam[1:2], axis=-1, keepdims=True))
                - jnp.exp(jnp.sum(lam[2:3] * lam[3:4], axis=-1, keepdims=True)) + lambda_init)
    scratch = (v1_ref, (s0_ref, s1_ref), (p0_ref, p1_ref), (m0_ref, m1_ref), (mf0_ref, mf1_ref))

    def one_head(hd, carry):
        _attn_head(q_ref.at[hd], k_ref.at[hd], v_ref.at[hd], bias_ref.at[hd], sg_ref, o_ref.at[hd],
                   scratch, lam_full, tile=tile, lambda_init=lambda_init)
        return carry

    lax.fori_loop(0, q_ref.shape[0], one_head, 0)


def _attn_head(q_ref, k_ref, v_ref, bias_ref, sg_ref, o_ref, scratch, lam_full, *, tile, lambda_init):
    v1_ref, s_refs, p_refs, m_refs, mfar_refs = scratch
    seq = q_ref.shape[0]
    dv = ATTN_V_DIM
    n_tiles = seq // tile
    v1_ref[:, :dv] = v_ref[...]
    v1_ref[:, dv:] = jnp.ones((seq, dv), BF16)
    lane = lax.broadcasted_iota(jnp.int32, (tile, dv), 1)

    def stacked_q(i):
        q = q_ref[i * tile:(i + 1) * tile, :]
        zero = jnp.zeros_like(q)
        return jnp.concatenate([jnp.where(lane < ATTN_HEAD_DIM, q, zero),
                                jnp.where(lane >= ATTN_HEAD_DIM, q, zero)], axis=0)

    far_bias = bias_ref[2, 0:1, 0:1]
    running_max = {}

    def is_far(i, j):
        return i - j >= 2

    def scores(i, j, qs):
        s = _dot_nt(qs, k_ref[j * tile:(j + 1) * tile, :])
        if not is_far(i, j):
            bt = bias_ref[i - j]
            s = s + jnp.concatenate([bt, bt], axis=0)
        s_refs[i % 2][:, j * tile:(j + 1) * tile] = s
        block_max = functools.reduce(
            jnp.maximum, [s[:, c:c + LANES] for c in range(0, tile, LANES)])
        key = (i, is_far(i, j))
        running_max[key] = (jnp.maximum(running_max[key], block_max)
                            if key in running_max else block_max)

    def row_max(i):
        m = jnp.max(running_max.pop((i, False)), axis=-1, keepdims=True)
        if (i, True) in running_max:
            m_far = jnp.max(running_max.pop((i, True)), axis=-1, keepdims=True) + far_bias
            m = jnp.maximum(m, m_far)
            mfar_refs[i % 2][...] = jnp.broadcast_to(m - far_bias, mfar_refs[i % 2].shape)
        m_refs[i % 2][...] = jnp.broadcast_to(m, m_refs[i % 2].shape)

    def probs(i, j):
        m = (mfar_refs if is_far(i, j) else m_refs)[i % 2][...]
        m_wide = jnp.concatenate([m] * (tile // LANES), axis=1)
        cols = slice(j * tile, (j + 1) * tile)
        p_refs[i % 2][:, cols] = jnp.exp2(s_refs[i % 2][:, cols] - m_wide).astype(BF16)

    def finish(i):
        n_keys = (i + 1) * tile
        acc = _dot(p_refs[i % 2][:, :n_keys], v1_ref[:n_keys, :])
        o_all = acc[:, :dv] / acc[:, dv:]
        o = o_all[:tile] - lam_full * o_all[tile:]
        ms = jnp.mean(o * o, axis=-1, keepdims=True)
        o = o * lax.rsqrt(ms + EPS) * sg_ref[...] * (1.0 - lambda_init)
        o_ref[i * tile:(i + 1) * tile, :] = o.astype(BF16)

    def score_and_prob_steps(i_scores, i_probs):
        qs = stacked_q(i_scores) if i_scores >= 0 else None
        for j in range(max(i_scores, i_probs) + 1):
            if j <= i_scores:
                scores(i_scores, j, qs)
            if j <= i_probs:
                probs(i_probs, j)
        if i_scores >= 0:
            row_max(i_scores)

    last = n_tiles - 1
    score_and_prob_steps(last, -1)
    score_and_prob_steps(last - 1, last)
    for i in range(last, -1, -1):
        finish(i)
        score_and_prob_steps(i - 2, i - 1)


def _attn_call(q2, k2, v2, bias_tiles, lam, subln_gain2, batch, seq, lambda_init):
    heads, t, _ = q2.shape
    tile = ATTN_TILE
    assert seq % tile == 0
    blk = pl.BlockSpec((heads, seq, ATTN_V_DIM), lambda b: (0, b, 0))
    return pl.pallas_call(
        functools.partial(_attn_kernel, tile=tile, lambda_init=lambda_init),
        grid=(batch,),
        in_specs=[
            blk, blk, blk,
            _resident(bias_tiles.shape),
            _resident(lam.shape),
            _resident((1, ATTN_V_DIM)),
        ],
        out_specs=blk,
        out_shape=jax.ShapeDtypeStruct((heads, t, ATTN_V_DIM), BF16),
        scratch_shapes=[
            pltpu.VMEM((seq, 2 * ATTN_V_DIM), BF16),
            pltpu.VMEM((2 * tile, seq), F32),
            pltpu.VMEM((2 * tile, seq), F32),
            pltpu.VMEM((2 * tile, seq), BF16),
            pltpu.VMEM((2 * tile, seq), BF16),
            pltpu.VMEM((2 * tile, LANES), F32),
            pltpu.VMEM((2 * tile, LANES), F32),
            pltpu.VMEM((2 * tile, LANES), F32),
            pltpu.VMEM((2 * tile, LANES), F32),
        ],
        compiler_params=_cparams(1),
        name="diff_attention",
    )(q2, k2, v2, bias_tiles, lam, subln_gain2)


def _hgrn_in_kernel(x_ref, mod_ref, g_ref, w_ref, lbl_ref, q_ref, k_ref, lf_ref, v_ref, og_ref,
                    *, d, layer_j):
    h = _norm_mod(x_ref[...], g_ref[...], mod_ref).astype(BF16)
    logits = lbl_ref[...]
    e = jnp.exp(logits - jnp.max(logits, axis=0, keepdims=True))
    p = e / jnp.sum(e, axis=0, keepdims=True)
    lb = jnp.zeros((1, d), F32)
    for i in range(1, layer_j + 1):
        lb = lb + p[i:i + 1, :]
    log_lb = jnp.log(lb)
    log_1m_lb = jnp.log1p(-lb)

    def dot_cols(w_lo, lo, hi):
        return lambda: _dot(h, w_ref[:, w_lo + lo:w_lo + hi])

    def forget_stage(lo, hi):
        def consume(f):
            ef = jnp.exp(-jnp.abs(f))
            one_plus = 1.0 + ef
            log_sig = jnp.minimum(f, 0.0) - jnp.log(one_plus)
            sig_neg = jnp.where(f >= 0.0, ef, 1.0) / one_plus
            if layer_j == 0:
                _store_heads(lf_ref, lo, hi, log_sig)
                _store_heads(k_ref, lo, hi, sig_neg)
            else:
                a = log_lb[:, lo:hi]
                b = log_1m_lb[:, lo:hi] + log_sig
                _store_heads(lf_ref, lo, hi,
                             jnp.maximum(a, b) + jnp.log(1.0 + jnp.exp(-jnp.abs(a - b))))
                _store_heads(k_ref, lo, hi, (1.0 - lb[:, lo:hi]) * sig_neg)

        return dot_cols(d, lo, hi), consume

    def query_stage(lo, hi):
        def consume(z):
            _store_heads(q_ref, lo, hi, z)

        return dot_cols(0, lo, hi), consume

    def value_stage(lo, hi):
        def consume(z):
            _store_heads(v_ref, lo, hi, z.astype(BF16))

        return dot_cols(2 * d, lo, hi), consume

    def gate_stage(lo, hi):
        def consume(g):
            _store_heads(og_ref, lo, hi, g * _sigmoid(g))

        return dot_cols(3 * d, lo, hi), consume

    stages = []
    for lo, hi in _col_pieces(d, HGRN_COLS):
        stages += [forget_stage(lo, hi), query_stage(lo, hi), gate_stage(lo, hi), value_stage(lo, hi)]
    _pipelined(stages)


def _hgrn_in_call(x2, mod3, gain, w_in, lb_logits, layer_j, seq):
    t, d = x2.shape
    tm = min(512, seq)
    assert t % tm == 0 and seq % tm == 0
    per_seq = seq // tm
    heads = d // LANES
    row = pl.BlockSpec((tm, d), lambda i: (i, 0))
    out_spec = _head_major(heads, tm)
    f32_out = jax.ShapeDtypeStruct((heads, t, LANES), F32)
    return pl.pallas_call(
        functools.partial(_hgrn_in_kernel, d=d, layer_j=layer_j),
        grid=(t // tm,),
        in_specs=[
            row,
            pl.BlockSpec((1, 3, d), lambda i: (i // per_seq, 0, 0)),
            _resident((1, d)),
            w_in.spec(),
            _resident(lb_logits.shape),
        ],
        out_specs=[out_spec] * 5,
        out_shape=[f32_out, f32_out, f32_out, jax.ShapeDtypeStruct((heads, t, LANES), BF16), f32_out],
        compiler_params=_cparams(1),
        name="hgrn_in_proj",
    )(x2, mod3, gain, w_in.array, lb_logits)


def _rows_at(g_cum, half):
    c = g_cum.shape[0]
    if half >= 4:
        blk = 2 * half
        return jnp.concatenate(
            [jnp.broadcast_to(g_cum[p * blk + half - 1:p * blk + half, :], (blk, g_cum.shape[1]))
             for p in range(c // blk)], axis=0)
    row = lax.broadcasted_iota(jnp.int32, g_cum.shape, 0)
    if half == 2:
        r4 = row % 4
        up1 = pltpu.roll(g_cum, c - 1, 0)
        dn1 = pltpu.roll(g_cum, 1, 0)
        dn2 = pltpu.roll(g_cum, 2, 0)
        return jnp.where(r4 == 0, up1, jnp.where(r4 == 1, g_cum, jnp.where(r4 == 2, dn1, dn2)))
    assert half == 1
    return jnp.where(row % 2 == 0, g_cum, pltpu.roll(g_cum, 1, 0))


def _round_robin(generators, width):
    pending = list(generators)
    active = []
    while pending or active:
        while pending and len(active) < width:
            active.append(pending.pop(0))
        for gen in list(active):
            try:
                next(gen)
            except StopIteration:
                active.remove(gen)


def _hgrn_rec_kernel(q_ref, k_ref, lf_ref, v_ref, og_ref, gain_ref, o_ref, *scratch, chunk):
    def one_head(hd, carry):
        _hgrn_rec_head(q_ref.at[hd], k_ref.at[hd], lf_ref.at[hd], v_ref.at[hd], og_ref.at[hd],
                       gain_ref.at[hd], o_ref.at[hd], *scratch, chunk=chunk)
        return carry

    lax.fori_loop(0, q_ref.shape[0], one_head, 0)


def _hgrn_rec_head(q_ref, k_ref, lf_ref, v_ref, og_ref, gain_ref, o_ref,
                   qe_ref, a_ref, u_ref, dl_ref, st_ref, *, chunk):
    seq = q_ref.shape[0]
    c = chunk
    n_chunks = seq // c
    ri = lax.broadcasted_iota(jnp.int32, (c, c), 0)
    ci = lax.broadcasted_iota(jnp.int32, (c, c), 1)
    tri = jnp.where(ri >= ci, 1.0, 0.0).astype(BF16)
    pair_key = jnp.where(ri >= ci, ri ^ ci, -1)
    row = lax.broadcasted_iota(jnp.int32, (c, HGRN_DK), 0)

    def neg_abs(z):
        return lax.bitcast_convert_type(
            lax.bitcast_convert_type(z, jnp.int32) | jnp.int32(-2 ** 31), F32)

    def intra(n):
        sl = slice(n * c, (n + 1) * c)
        g = lf_ref[sl, :]
        q = q_ref[sl, :]
        k = k_ref[sl, :]
        g_hi = g.astype(BF16)
        r1 = g - g_hi.astype(F32)
        g_mid = r1.astype(BF16)
        g_lo = (r1 - g_mid.astype(F32)).astype(BF16)
        g2 = (_dot(tri, g_hi) + (_dot(tri, g_mid) + _dot(tri, g_lo))) * LOG2E
        g2_last = g2[c - 1:c, :]
        yield
        qe_ref[sl, :] = (q * jnp.exp2(g2)).astype(BF16)
        k_dec = (k * jnp.exp2(g2_last - g2)).astype(BF16)
        u_ref[n] = _dot_tn(v_ref[sl, :], k_dec)
        dl_ref[n] = jnp.exp2(g2_last)
        scores = jnp.where(pair_key == 0, _dot_nt(q.astype(BF16), k.astype(BF16)), 0.0)
        yield
        half = 1
        while half < c:
            e = jnp.exp2(neg_abs(g2 - _rows_at(g2, half)))
            if half < 8:
                level = _dot_nt((q * e).astype(BF16), (k * e).astype(BF16))
                scores = jnp.where(pair_key >= half, level, scores)
            else:
                qk = jnp.concatenate(
                    [(q if (b % 2) else k)[b * half:(b + 1) * half] for b in range(c // half)], axis=0)
                zf = qk * e
                z = zf.astype(BF16)
                z_right = jnp.concatenate(
                    [zf[b * half:(b + 1) * half] for b in range(1, c // half, 2)], axis=0).astype(BF16)
                level = _dot_nt(z_right, z)
                pieces = []
                for b in range(c // half):
                    rows = slice(b * half, (b + 1) * half)
                    piece = scores[rows]
                    if b % 2:
                        lv = level[(b // 2) * half:(b // 2 + 1) * half]
                        piece = jnp.where(pair_key[rows] >= half, lv, piece)
                    pieces.append(piece)
                scores = jnp.concatenate(pieces, axis=0)
            half *= 2
            yield
        a_ref[n] = scores.astype(BF16)

    def outputs(n):
        sl = slice(n * c, (n + 1) * c)
        o = _dot_nt(qe_ref[sl, :], st_ref[n]) + _dot(a_ref[n], v_ref[sl, :])
        yield
        ms = jnp.mean(o * o, axis=-1, keepdims=True)
        o_ref[sl, :] = (o * lax.rsqrt(ms + EPS) * gain_ref[...] * og_ref[sl, :]).astype(BF16)

    _round_robin([intra(n) for n in range(n_chunks)], HGRN_INTERLEAVE)

    state = jnp.zeros((HGRN_DV, HGRN_DK), F32)
    for n in range(n_chunks):
        st_ref[n] = state.astype(BF16)
        state = dl_ref[n] * state + u_ref[n]

    _round_robin([outputs(n) for n in range(n_chunks)], HGRN_INTERLEAVE)


def _hgrn_rec_call(q2, k2, lf2, v2, og2, out_gain2, batch, seq):
    heads, t, _ = q2.shape
    c = HGRN_CHUNK
    assert seq % c == 0
    n_chunks = seq // c
    per_step = HGRN_HEADS_PER_STEP
    assert heads % per_step == 0
    blk = pl.BlockSpec((per_step, seq, HGRN_DV), lambda b, g: (g, b, 0))
    return pl.pallas_call(
        functools.partial(_hgrn_rec_kernel, chunk=c),
        grid=(batch, heads // per_step),
        in_specs=[blk, blk, blk, blk, blk,
                  pl.BlockSpec((per_step, 1, HGRN_DV), lambda b, g: (g, 0, 0))],
        out_specs=blk,
        out_shape=jax.ShapeDtypeStruct((heads, t, HGRN_DV), BF16),
        scratch_shapes=[
            pltpu.VMEM((seq, HGRN_DK), BF16),
            pltpu.VMEM((n_chunks, c, c), BF16),
            pltpu.VMEM((n_chunks, HGRN_DV, HGRN_DK), F32),
            pltpu.VMEM((n_chunks, 1, HGRN_DK), F32),
            pltpu.VMEM((n_chunks, HGRN_DV, HGRN_DK), BF16),
        ],
        compiler_params=_cparams(2),
        name="hgrn_recurrence",
    )(q2, k2, lf2, v2, og2, out_gain2)


def kernel(x, c, ada_w, ada_b, norm_g, ffn_w_in, ffn_w_down, attn_w_qkv, attn_w_o, attn_q_gain,
           attn_k_gain, attn_lambda, attn_subln_gain, rel_bias, hgrn_w_in, hgrn_w_o,
           hgrn_out_gain, hgrn_lb_logits):
    batch, seq, d = x.shape
    depth = ada_w.shape[0]
    mod = _ada_call(c, ada_w, ada_b).reshape(depth, batch, N_SUBLAYERS, 3, d)
    bias_tiles = _bias_call(rel_bias, ATTN_TILE)
    x2 = x.reshape(batch * seq, d)
    qkv_b, attn_o_b = attn_w_qkv.astype(BF16), attn_w_o.astype(BF16)
    hgrn_in_b, hgrn_o_b = hgrn_w_in.astype(BF16), hgrn_w_o.astype(BF16)
    ffn_order = [(layer, half) for layer in range(depth) for half in range(2)]

    def next_ffn_weights(layer, half):
        nxt = ffn_order.index((layer, half)) + 1
        if nxt == len(ffn_order):
            return ()
        return (_Stacked(ffn_w_in, ffn_order[nxt]), _Stacked(ffn_w_down, ffn_order[nxt]))

    ffn_w = [_Stacked(ffn_w_in[0, 0].astype(BF16)), _Stacked(ffn_w_down[0, 0].astype(BF16))]
    for layer in range(depth):
        gains = norm_g[layer].reshape(N_SUBLAYERS, 1, d)
        x2, ffn_w = _ffn_call(x2, mod[layer, :, 0], gains[0], ffn_w[0], ffn_w[1], seq,
                              cast_next=next_ffn_weights(layer, 0))
        j = layer // N_MIXERS
        if layer % N_MIXERS == 0:
            reps = QKV_COLS // ATTN_HEAD_DIM
            q2, k2, v2 = _qkv_call(
                x2, mod[layer, :, 1], gains[1], _Stacked(qkv_b, (j,)),
                jnp.tile(attn_q_gain[j], reps).reshape(1, QKV_COLS),
                jnp.tile(attn_k_gain[j], reps).reshape(1, QKV_COLS), seq)
            lambda_init = 0.8 - 0.6 * math.exp(-0.3 * layer)
            o2 = _attn_call(q2, k2, v2, bias_tiles, attn_lambda[j],
                            attn_subln_gain[j].reshape(1, ATTN_V_DIM), batch, seq, lambda_init)
            w_o = _Stacked(attn_o_b, (j,))
        else:
            q2, k2, lf2, v2, og2 = _hgrn_in_call(
                x2, mod[layer, :, 1], gains[1], _Stacked(hgrn_in_b, (j,)), hgrn_lb_logits, j, seq)
            o2 = _hgrn_rec_call(q2, k2, lf2, v2, og2,
                                hgrn_out_gain[j].reshape(HGRN_HEADS, 1, HGRN_DV), batch, seq)
            w_o = _Stacked(hgrn_o_b, (j,))
        x2, ffn_w = _ffn_call(x2, mod[layer, :, 2], gains[2], ffn_w[0], ffn_w[1], seq,
                              mixer=(o2, mod[layer, :, 1], w_o),
                              cast_next=next_ffn_weights(layer, 1))
    return x2.reshape(batch, seq, d)
```

```python
import functools
import math
from typing import NamedTuple

import numpy as np
import jax
import jax.numpy as jnp
from jax import lax
from jax.experimental import pallas as pl
from jax.experimental.pallas import tpu as pltpu

F32 = jnp.float32
BF16 = jnp.bfloat16

EPS = 1e-6
N_SUBLAYERS = 3
N_MIXERS = 2
ATTN_HEADS = 8
ATTN_HEAD_DIM = 64
ATTN_V_DIM = 128
NUM_BUCKETS = 32
MAX_DISTANCE = 128
HGRN_HEADS = 8
HGRN_DK = 128
HGRN_DV = 128

LANES = 128
VMEM_LIMIT = 60 * 1024 * 1024
NEG = -1e30
LOG2E = math.log2(math.e)

MXU_COLS = 256
ATTN_TILE = 256
ATTN_HEADS_PER_STEP = 1
QKV_COLS = MXU_COLS
FFN_COLS = 2 * MXU_COLS
HGRN_COLS = MXU_COLS
HGRN_CHUNK = 128
HGRN_HEADS_PER_STEP = 1
ADA_COLS = 4608
HGRN_INTERLEAVE = 16


def _cparams(n_axes):
    return pltpu.CompilerParams(
        dimension_semantics=("arbitrary",) * n_axes, vmem_limit_bytes=VMEM_LIMIT)


def _resident(shape):
    return pl.BlockSpec(shape, lambda *_: (0,) * len(shape), pipeline_mode=pl.Buffered(1))


class _Stacked(NamedTuple):
    array: jax.Array
    lead: tuple = ()

    @property
    def shape(self):
        return self.array.shape[len(self.lead):]

    def spec(self):
        lead, tail = self.lead, self.shape
        return pl.BlockSpec((None,) * len(lead) + tail, lambda *_: lead + (0,) * len(tail),
                            pipeline_mode=pl.Buffered(1))


def _sigmoid(z):
    return 1.0 / (1.0 + jnp.exp(-z))


def _dot(a, b):
    return jnp.dot(a, b, preferred_element_type=F32)


def _dot_nt(a, b):
    return lax.dot_general(a, b, (((1,), (1,)), ((), ())), preferred_element_type=F32)


def _dot_tn(a, b):
    return lax.dot_general(a, b, (((0,), (0,)), ((), ())), preferred_element_type=F32)


def _pipelined(stages):
    pending = None
    for produce, consume in stages:
        value = produce()
        if pending is not None:
            pending[1](pending[0])
        pending = (value, consume)
    if pending is not None:
        pending[1](pending[0])


def _cast_job_specs(cast_next, steps, step_of):
    operands, in_specs, out_specs, out_shapes = [], [], [], []
    for w in cast_next:
        n_rows, n_cols = w.shape
        assert n_rows % (steps * 16) == 0
        lead, blk_rows = w.lead, n_rows // steps
        operands.append(w.array)
        in_specs.append(pl.BlockSpec((None,) * len(lead) + (blk_rows, n_cols),
                                     lambda *g, lead=lead: lead + (step_of(*g), 0)))
        out_specs.append(pl.BlockSpec((blk_rows, n_cols), lambda *g: (step_of(*g), 0)))
        out_shapes.append(jax.ShapeDtypeStruct((n_rows, n_cols), BF16))
    return operands, in_specs, out_specs, out_shapes


def _run_cast_jobs(src_refs, dst_refs):
    for src_ref, dst_ref in zip(src_refs, dst_refs):
        dst_ref[...] = src_ref[...].astype(BF16)


def _col_pieces(total, width):
    return [(lo, min(lo + width, total)) for lo in range(0, total, width)]


def _store_heads(ref, lo, hi, value):
    for col in range(lo, hi, LANES):
        ref[col // LANES] = value[:, col - lo:col - lo + LANES]


def _head_major(heads, rows):
    return pl.BlockSpec((heads, rows, LANES), lambda i: (0, i, 0))


def _norm_mod(x, gain, mod_ref):
    ms = jnp.mean(x * x, axis=-1, keepdims=True)
    h = x * lax.rsqrt(ms + EPS) * gain
    return h * (1.0 + mod_ref[0, 1:2, :]) + mod_ref[0, 0:1, :]


def _ada_kernel(c_ref, w_ref, b_ref, o_ref):
    c = c_ref[...]
    ca = (c * _sigmoid(c)).astype(BF16)
    o_ref[0] = _dot(ca, w_ref[0].astype(BF16)) + b_ref[0]


def _ada_call(c, ada_w, ada_b):
    depth, d, n = ada_w.shape
    b = c.shape[0]
    tn = ADA_COLS
    assert n % tn == 0
    return pl.pallas_call(
        _ada_kernel,
        grid=(depth, n // tn),
        in_specs=[
            pl.BlockSpec((b, d), lambda l, j: (0, 0)),
            pl.BlockSpec((1, d, tn), lambda l, j: (l, 0, j)),
            pl.BlockSpec((1, 1, tn), lambda l, j: (l, 0, j)),
        ],
        out_specs=pl.BlockSpec((1, b, tn), lambda l, j: (l, 0, j)),
        out_shape=jax.ShapeDtypeStruct((depth, b, n), F32),
        compiler_params=_cparams(2),
        name="ada_mod",
    )(c, ada_w, ada_b.reshape(depth, 1, n))


def _ffn_kernel(*refs, d_ff, with_mixer, n_cast):
    n_in = (8 if with_mixer else 5) + n_cast
    ins, outs, act_ref = refs[:n_in], refs[n_in:-1], refs[-1]
    if with_mixer:
        x_ref, mix_ref, mixmod_ref, wo_ref, mod_ref, g_ref, win_ref, wdn_ref = ins[:8]
    else:
        x_ref, mod_ref, g_ref, win_ref, wdn_ref = ins[:5]
    o_ref = outs[0]
    _run_cast_jobs(ins[n_in - n_cast:], outs[1:])

    if with_mixer:
        mix = jnp.concatenate([mix_ref[hd] for hd in range(mix_ref.shape[0])], axis=1)
        x = x_ref[...] + mixmod_ref[0, 2:3, :] * _dot(mix, wo_ref[...])
    else:
        x = x_ref[...]
    h = _norm_mod(x, g_ref[...], mod_ref).astype(BF16)

    def stage(lo, hi):
        def produce():
            return _dot(h, win_ref[:, lo:hi]), _dot(h, win_ref[:, d_ff + lo:d_ff + hi])

        def consume(ab):
            a, b = ab
            act_ref[:, lo:hi] = (a * _sigmoid(a) * b).astype(BF16)

        return produce, consume

    _pipelined([stage(lo, hi) for lo, hi in _col_pieces(d_ff, FFN_COLS)])
    y = _dot(act_ref[...], wdn_ref[...])
    o_ref[...] = x + 0.5 * mod_ref[0, 2:3, :] * y


def _ffn_call(x2, mod3, gain, w_in, w_down, seq, mixer=None, cast_next=()):
    t, d = x2.shape
    d_ff = w_down.shape[0]
    tm = min(1024, seq)
    assert t % tm == 0 and seq % tm == 0 and d_ff % MXU_COLS == 0
    steps = t // tm
    per_seq = seq // tm
    row = pl.BlockSpec((tm, d), lambda i: (i, 0))
    mod_spec = pl.BlockSpec((1, 3, d), lambda i: (i // per_seq, 0, 0))
    operands, in_specs = [x2], [row]
    if mixer is not None:
        mix, mix_mod3, w_o = mixer
        operands += [mix, mix_mod3, w_o.array]
        in_specs += [_head_major(mix.shape[0], tm), mod_spec, w_o.spec()]
    operands += [mod3, gain, w_in.array, w_down.array]
    in_specs += [mod_spec, _resident((1, d)), w_in.spec(), w_down.spec()]
    job_operands, job_in, job_out, job_shapes = _cast_job_specs(cast_next, steps, lambda i: i)
    operands += job_operands
    in_specs += job_in
    out_specs = [row] + job_out
    out_shapes = [jax.ShapeDtypeStruct((t, d), F32)] + job_shapes
    outs = pl.pallas_call(
        functools.partial(_ffn_kernel, d_ff=d_ff, with_mixer=mixer is not None,
                          n_cast=len(cast_next)),
        grid=(steps,),
        in_specs=in_specs,
        out_specs=out_specs,
        out_shape=out_shapes,
        scratch_shapes=[pltpu.VMEM((tm, d_ff), BF16)],
        compiler_params=_cparams(1),
        name="ffn_half_step",
    )(*operands)
    return outs[0], [_Stacked(w) for w in outs[1:]]


def _group_mean_sq(z, ones_blk):
    return _dot((z * z).astype(BF16), ones_blk)


def _qkv_kernel(x_ref, mod_ref, g_ref, w_ref, qg_ref, kg_ref, q_ref, k_ref, v_ref, *, d):
    h = _norm_mod(x_ref[...], g_ref[...], mod_ref).astype(BF16)
    tn = QKV_COLS
    r = lax.broadcasted_iota(jnp.int32, (tn, tn), 0) // ATTN_HEAD_DIM
    c = lax.broadcasted_iota(jnp.int32, (tn, tn), 1) // ATTN_HEAD_DIM
    ones_blk = jnp.where(r == c, 1.0 / ATTN_HEAD_DIM, 0.0).astype(BF16)
    q_gain = qg_ref[...] * (ATTN_HEAD_DIM ** -0.5 * LOG2E)
    k_gain = kg_ref[...]

    def normed(w_lo, out_ref, lo, hi, gain):
        def consume(z):
            _store_heads(out_ref, lo, hi,
                         (z * lax.rsqrt(_group_mean_sq(z, ones_blk) + EPS) * gain).astype(BF16))

        return (lambda: _dot(h, w_ref[:, w_lo + lo:w_lo + hi])), consume

    def plain(w_lo, out_ref, lo, hi):
        def consume(z):
            _store_heads(out_ref, lo, hi, z.astype(BF16))

        return (lambda: _dot(h, w_ref[:, w_lo + lo:w_lo + hi])), consume

    stages = []
    for lo, hi in _col_pieces(d, tn):
        stages += [normed(0, q_ref, lo, hi, q_gain), normed(d, k_ref, lo, hi, k_gain),
                   plain(2 * d, v_ref, lo, hi)]
    _pipelined(stages)


def _qkv_call(x2, mod3, gain, w_qkv, q_gain2, k_gain2, seq):
    t, d = x2.shape
    tm = min(1024, seq)
    assert t % tm == 0 and seq % tm == 0
    per_seq = seq // tm
    heads = d // LANES
    out = jax.ShapeDtypeStruct((heads, t, LANES), BF16)
    row = pl.BlockSpec((tm, d), lambda i: (i, 0))
    out_spec = _head_major(heads, tm)
    return pl.pallas_call(
        functools.partial(_qkv_kernel, d=d),
        grid=(t // tm,),
        in_specs=[
            row,
            pl.BlockSpec((1, 3, d), lambda i: (i // per_seq, 0, 0)),
            _resident((1, d)),
            w_qkv.spec(),
            _resident((1, QKV_COLS)),
            _resident((1, QKV_COLS)),
        ],
        out_specs=[out_spec, out_spec, out_spec],
        out_shape=[out, out, out],
        compiler_params=_cparams(1),
        name="attn_qkv_proj",
    )(x2, mod3, gain, w_qkv.array, q_gain2, k_gain2)


def _bucket_thresholds():
    max_exact = NUM_BUCKETS // 2
    dist = np.arange(0, 4 * MAX_DISTANCE)
    d_f = np.maximum(dist, 1).astype(np.float32)
    large = max_exact + (np.log(d_f / np.float32(max_exact)) / np.float32(math.log(MAX_DISTANCE / max_exact))
                         * np.float32(NUM_BUCKETS - max_exact)).astype(np.int32)
    bucket = np.where(dist < max_exact, dist, np.minimum(large, NUM_BUCKETS - 1))
    assert np.all(np.diff(bucket) >= 0) and bucket[-1] == NUM_BUCKETS - 1
    return [int(np.argmax(bucket >= b)) for b in range(NUM_BUCKETS)]


def _bias_kernel(rb_ref, o_ref, *, thresholds, tile):
    h = pl.program_id(0)
    i = lax.broadcasted_iota(jnp.int32, (tile, tile), 0)
    j = lax.broadcasted_iota(jnp.int32, (tile, tile), 1)
    for off in range(3):
        dist = off * tile + i - j
        val = jnp.full((tile, tile), rb_ref[0, h], F32)
        for b in range(1, NUM_BUCKETS):
            val = jnp.where(dist >= thresholds[b], rb_ref[b, h], val)
        o_ref[0, off] = jnp.where(dist >= 0, val * LOG2E, NEG)


def _bias_call(rel_bias, tile):
    thresholds = _bucket_thresholds()
    assert thresholds[NUM_BUCKETS - 1] <= tile + 1
    heads = rel_bias.shape[1]
    return pl.pallas_call(
        functools.partial(_bias_kernel, thresholds=thresholds, tile=tile),
        grid=(heads,),
        in_specs=[pl.BlockSpec(memory_space=pltpu.SMEM)],
        out_specs=pl.BlockSpec((1, 3, tile, tile), lambda h: (h, 0, 0, 0)),
        out_shape=jax.ShapeDtypeStruct((heads, 3, tile, tile), F32),
        compiler_params=_cparams(1),
        name="attn_rel_bias_tiles",
    )(rel_bias)


def _attn_kernel(*refs, tile, lambda_init, n_cast):
    n_in = 6 + n_cast
    q_ref, k_ref, v_ref, bias_ref, lam_ref, sg_ref = refs[:6]
    o_ref = refs[n_in]
    v1_ref, s0_ref, s1_ref, p0_ref, p1_ref, m0_ref, m1_ref, mf0_ref, mf1_ref = refs[n_in + 1 + n_cast:]
    _run_cast_jobs(refs[6:n_in], refs[n_in + 1:n_in + 1 + n_cast])
    lam = lam_ref[...]
    lam_full = (jnp.exp(jnp.sum(lam[0:1] * lam[1:2], axis=-1, keepdims=True))
                - jnp.exp(jnp.sum(lam[2:3] * lam[3:4], axis=-1, keepdims=True)) + lambda_init)
    scratch = (v1_ref, (s0_ref, s1_ref), (p0_ref, p1_ref), (m0_ref, m1_ref), (mf0_ref, mf1_ref))

    def one_head(hd, carry):
        _attn_head(q_ref.at[hd], k_ref.at[hd], v_ref.at[hd], bias_ref.at[hd], sg_ref, o_ref.at[hd],
                   scratch, lam_full, tile=tile, lambda_init=lambda_init)
        return carry

    lax.fori_loop(0, q_ref.shape[0], one_head, 0)


def _attn_head(q_ref, k_ref, v_ref, bias_ref, sg_ref, o_ref, scratch, lam_full, *, tile, lambda_init):
    v1_ref, s_refs, p_refs, m_refs, mfar_refs = scratch
    seq = q_ref.shape[0]
    dv = ATTN_V_DIM
    n_tiles = seq // tile
    v1_ref[:, :dv] = v_ref[...]
    v1_ref[:, dv:] = jnp.ones((seq, dv), BF16)
    lane = lax.broadcasted_iota(jnp.int32, (tile, dv), 1)

    def stacked_q(i):
        q = q_ref[i * tile:(i + 1) * tile, :]
        zero = jnp.zeros_like(q)
        return jnp.concatenate([jnp.where(lane < ATTN_HEAD_DIM, q, zero),
                                jnp.where(lane >= ATTN_HEAD_DIM, q, zero)], axis=0)

    far_bias = bias_ref[2, 0:1, 0:1]
    running_max = {}

    def is_far(i, j):
        return i - j >= 2

    def scores(i, j, qs):
        s = _dot_nt(qs, k_ref[j * tile:(j + 1) * tile, :])
        if not is_far(i, j):
            bt = bias_ref[i - j]
            s = s + jnp.concatenate([bt, bt], axis=0)
        s_refs[i % 2][:, j * tile:(j + 1) * tile] = s
        block_max = functools.reduce(
            jnp.maximum, [s[:, c:c + LANES] for c in range(0, tile, LANES)])
        key = (i, is_far(i, j))
        running_max[key] = (jnp.maximum(running_max[key], block_max)
                            if key in running_max else block_max)

    def row_max(i):
        m = jnp.max(running_max.pop((i, False)), axis=-1, keepdims=True)
        if (i, True) in running_max:
            m_far = jnp.max(running_max.pop((i, True)), axis=-1, keepdims=True) + far_bias
            m = jnp.maximum(m, m_far)
            mfar_refs[i % 2][...] = jnp.broadcast_to(m - far_bias, mfar_refs[i % 2].shape)
        m_refs[i % 2][...] = jnp.broadcast_to(m, m_refs[i % 2].shape)

    def probs(i, j):
        m = (mfar_refs if is_far(i, j) else m_refs)[i % 2][...]
        m_wide = jnp.concatenate([m] * (tile // LANES), axis=1)
        cols = slice(j * tile, (j + 1) * tile)
        p_refs[i % 2][:, cols] = jnp.exp2(s_refs[i % 2][:, cols] - m_wide).astype(BF16)

    def finish(i):
        n_keys = (i + 1) * tile
        acc = _dot(p_refs[i % 2][:, :n_keys], v1_ref[:n_keys, :])
        o_all = acc[:, :dv] / acc[:, dv:]
        o = o_all[:tile] - lam_full * o_all[tile:]
        ms = jnp.mean(o * o, axis=-1, keepdims=True)
        o = o * lax.rsqrt(ms + EPS) * sg_ref[...] * (1.0 - lambda_init)
        o_ref[i * tile:(i + 1) * tile, :] = o.astype(BF16)

    def score_and_prob_steps(i_scores, i_probs):
        qs = stacked_q(i_scores) if i_scores >= 0 else None
        for j in range(max(i_scores, i_probs) + 1):
            if j <= i_scores:
                scores(i_scores, j, qs)
            if j <= i_probs:
                probs(i_probs, j)
        if i_scores >= 0:
            row_max(i_scores)

    last = n_tiles - 1
    score_and_prob_steps(last, -1)
    score_and_prob_steps(last - 1, last)
    for i in range(last, -1, -1):
        finish(i)
        score_and_prob_steps(i - 2, i - 1)


def _attn_call(q2, k2, v2, bias_tiles, lam, subln_gain2, batch, seq, lambda_init, cast_next=()):
    heads, t, _ = q2.shape
    tile = ATTN_TILE
    per_step = ATTN_HEADS_PER_STEP
    groups = heads // per_step
    assert seq % tile == 0 and heads % per_step == 0
    blk = pl.BlockSpec((per_step, seq, ATTN_V_DIM), lambda b, g: (g, b, 0))
    job_operands, job_in, job_out, job_shapes = _cast_job_specs(
        cast_next, batch * groups, lambda b, g: b * groups + g)
    outs = pl.pallas_call(
        functools.partial(_attn_kernel, tile=tile, lambda_init=lambda_init, n_cast=len(cast_next)),
        grid=(batch, groups),
        in_specs=[
            blk, blk, blk,
            pl.BlockSpec((per_step, 3, tile, tile), lambda b, g: (g, 0, 0, 0)),
            _resident(lam.shape),
            _resident((1, ATTN_V_DIM)),
        ] + job_in,
        out_specs=[blk] + job_out,
        out_shape=[jax.ShapeDtypeStruct((heads, t, ATTN_V_DIM), BF16)] + job_shapes,
        scratch_shapes=[
            pltpu.VMEM((seq, 2 * ATTN_V_DIM), BF16),
            pltpu.VMEM((2 * tile, seq), F32),
            pltpu.VMEM((2 * tile, seq), F32),
            pltpu.VMEM((2 * tile, seq), BF16),
            pltpu.VMEM((2 * tile, seq), BF16),
            pltpu.VMEM((2 * tile, LANES), F32),
            pltpu.VMEM((2 * tile, LANES), F32),
            pltpu.VMEM((2 * tile, LANES), F32),
            pltpu.VMEM((2 * tile, LANES), F32),
        ],
        compiler_params=_cparams(2),
        name="diff_attention",
    )(q2, k2, v2, bias_tiles, lam, subln_gain2, *job_operands)
    return outs[0], [_Stacked(w) for w in outs[1:]]


def _hgrn_in_kernel(x_ref, mod_ref, g_ref, w_ref, lbl_ref, q_ref, k_ref, lf_ref, v_ref, og_ref,
                    *, d, layer_j):
    h = _norm_mod(x_ref[...], g_ref[...], mod_ref).astype(BF16)
    logits = lbl_ref[...]
    e = jnp.exp(logits - jnp.max(logits, axis=0, keepdims=True))
    p = e / jnp.sum(e, axis=0, keepdims=True)
    lb = jnp.zeros((1, d), F32)
    for i in range(1, layer_j + 1):
        lb = lb + p[i:i + 1, :]
    log_lb = jnp.log(lb)
    log_1m_lb = jnp.log1p(-lb)

    def dot_cols(w_lo, lo, hi):
        return lambda: _dot(h, w_ref[:, w_lo + lo:w_lo + hi])

    def forget_stage(lo, hi):
        def consume(f):
            ef = jnp.exp(-jnp.abs(f))
            one_plus = 1.0 + ef
            log_sig = jnp.minimum(f, 0.0) - jnp.log(one_plus)
            sig_neg = jnp.where(f >= 0.0, ef, 1.0) / one_plus
            if layer_j == 0:
                _store_heads(lf_ref, lo, hi, log_sig)
                _store_heads(k_ref, lo, hi, sig_neg)
            else:
                a = log_lb[:, lo:hi]
                b = log_1m_lb[:, lo:hi] + log_sig
                _store_heads(lf_ref, lo, hi,
                             jnp.maximum(a, b) + jnp.log(1.0 + jnp.exp(-jnp.abs(a - b))))
                _store_heads(k_ref, lo, hi, (1.0 - lb[:, lo:hi]) * sig_neg)

        return dot_cols(d, lo, hi), consume

    def query_stage(lo, hi):
        def consume(z):
            _store_heads(q_ref, lo, hi, z)

        return dot_cols(0, lo, hi), consume

    def value_stage(lo, hi):
        def consume(z):
            _store_heads(v_ref, lo, hi, z.astype(BF16))

        return dot_cols(2 * d, lo, hi), consume

    def gate_stage(lo, hi):
        def consume(g):
            _store_heads(og_ref, lo, hi, g * _sigmoid(g))

        return dot_cols(3 * d, lo, hi), consume

    stages = []
    for lo, hi in _col_pieces(d, HGRN_COLS):
        stages += [forget_stage(lo, hi), query_stage(lo, hi), gate_stage(lo, hi), value_stage(lo, hi)]
    _pipelined(stages)


def _hgrn_in_call(x2, mod3, gain, w_in, lb_logits, layer_j, seq):
    t, d = x2.shape
    tm = min(512, seq)
    assert t % tm == 0 and seq % tm == 0
    per_seq = seq // tm
    heads = d // LANES
    row = pl.BlockSpec((tm, d), lambda i: (i, 0))
    out_spec = _head_major(heads, tm)
    f32_out = jax.ShapeDtypeStruct((heads, t, LANES), F32)
    return pl.pallas_call(
        functools.partial(_hgrn_in_kernel, d=d, layer_j=layer_j),
        grid=(t // tm,),
        in_specs=[
            row,
            pl.BlockSpec((1, 3, d), lambda i: (i // per_seq, 0, 0)),
            _resident((1, d)),
            w_in.spec(),
            _resident(lb_logits.shape),
        ],
        out_specs=[out_spec] * 5,
        out_shape=[f32_out, f32_out, f32_out, jax.ShapeDtypeStruct((heads, t, LANES), BF16), f32_out],
        compiler_params=_cparams(1),
        name="hgrn_in_proj",
    )(x2, mod3, gain, w_in.array, lb_logits)


def _rows_at(g_cum, half):
    c = g_cum.shape[0]
    if half >= 4:
        blk = 2 * half
        return jnp.concatenate(
            [jnp.broadcast_to(g_cum[p * blk + half - 1:p * blk + half, :], (blk, g_cum.shape[1]))
             for p in range(c // blk)], axis=0)
    row = lax.broadcasted_iota(jnp.int32, g_cum.shape, 0)
    if half == 2:
        r4 = row % 4
        up1 = pltpu.roll(g_cum, c - 1, 0)
        dn1 = pltpu.roll(g_cum, 1, 0)
        dn2 = pltpu.roll(g_cum, 2, 0)
        return jnp.where(r4 == 0, up1, jnp.where(r4 == 1, g_cum, jnp.where(r4 == 2, dn1, dn2)))
    assert half == 1
    return jnp.where(row % 2 == 0, g_cum, pltpu.roll(g_cum, 1, 0))


def _round_robin(generators, width):
    pending = list(generators)
    active = []
    while pending or active:
        while pending and len(active) < width:
            active.append(pending.pop(0))
        for gen in list(active):
            try:
                next(gen)
            except StopIteration:
                active.remove(gen)


def _hgrn_rec_kernel(*refs, chunk, n_cast):
    n_in = 6 + n_cast
    q_ref, k_ref, lf_ref, v_ref, og_ref, gain_ref = refs[:6]
    o_ref = refs[n_in]
    scratch = refs[n_in + 1 + n_cast:]
    _run_cast_jobs(refs[6:n_in], refs[n_in + 1:n_in + 1 + n_cast])

    def one_head(hd, carry):
        _hgrn_rec_head(q_ref.at[hd], k_ref.at[hd], lf_ref.at[hd], v_ref.at[hd], og_ref.at[hd],
                       gain_ref.at[hd], o_ref.at[hd], *scratch, chunk=chunk)
        return carry

    lax.fori_loop(0, q_ref.shape[0], one_head, 0)


def _hgrn_rec_head(q_ref, k_ref, lf_ref, v_ref, og_ref, gain_ref, o_ref,
                   qe_ref, a_ref, u_ref, dl_ref, st_ref, *, chunk):
    seq = q_ref.shape[0]
    c = chunk
    n_chunks = seq // c
    ri = lax.broadcasted_iota(jnp.int32, (c, c), 0)
    ci = lax.broadcasted_iota(jnp.int32, (c, c), 1)
    tri = jnp.where(ri >= ci, 1.0, 0.0).astype(BF16)
    pair_key = jnp.where(ri >= ci, ri ^ ci, -1)
    row = lax.broadcasted_iota(jnp.int32, (c, HGRN_DK), 0)

    def neg_abs(z):
        return lax.bitcast_convert_type(
            lax.bitcast_convert_type(z, jnp.int32) | jnp.int32(-2 ** 31), F32)

    def intra(n):
        sl = slice(n * c, (n + 1) * c)
        g = lf_ref[sl, :]
        q = q_ref[sl, :]
        k = k_ref[sl, :]
        g_hi = g.astype(BF16)
        r1 = g - g_hi.astype(F32)
        g_mid = r1.astype(BF16)
        g_lo = (r1 - g_mid.astype(F32)).astype(BF16)
        g2 = (_dot(tri, g_hi) + (_dot(tri, g_mid) + _dot(tri, g_lo))) * LOG2E
        g2_last = g2[c - 1:c, :]
        yield
        qe_ref[sl, :] = (q * jnp.exp2(g2)).astype(BF16)
        k_dec = (k * jnp.exp2(g2_last - g2)).astype(BF16)
        u_ref[n] = _dot_tn(v_ref[sl, :], k_dec)
        dl_ref[n] = jnp.exp2(g2_last)
        scores = jnp.where(pair_key == 0, _dot_nt(q.astype(BF16), k.astype(BF16)), 0.0)
        yield
        half = 1
        while half < c:
            e = jnp.exp2(neg_abs(g2 - _rows_at(g2, half)))
            if half < 8:
                level = _dot_nt((q * e).astype(BF16), (k * e).astype(BF16))
                scores = jnp.where(pair_key >= half, level, scores)
            else:
                qk = jnp.concatenate(
                    [(q if (b % 2) else k)[b * half:(b + 1) * half] for b in range(c // half)], axis=0)
                zf = qk * e
                z = zf.astype(BF16)
                z_right = jnp.concatenate(
                    [zf[b * half:(b + 1) * half] for b in range(1, c // half, 2)], axis=0).astype(BF16)
                level = _dot_nt(z_right, z)
                pieces = []
                for b in range(c // half):
                    rows = slice(b * half, (b + 1) * half)
                    piece = scores[rows]
                    if b % 2:
                        lv = level[(b // 2) * half:(b // 2 + 1) * half]
                        piece = jnp.where(pair_key[rows] >= half, lv, piece)
                    pieces.append(piece)
                scores = jnp.concatenate(pieces, axis=0)
            half *= 2
            yield
        a_ref[n] = scores.astype(BF16)

    def outputs(n):
        sl = slice(n * c, (n + 1) * c)
        o = _dot_nt(qe_ref[sl, :], st_ref[n]) + _dot(a_ref[n], v_ref[sl, :])
        yield
        ms = jnp.mean(o * o, axis=-1, keepdims=True)
        o_ref[sl, :] = (o * lax.rsqrt(ms + EPS) * gain_ref[...] * og_ref[sl, :]).astype(BF16)

    _round_robin([intra(n) for n in range(n_chunks)], HGRN_INTERLEAVE)

    state = jnp.zeros((HGRN_DV, HGRN_DK), F32)
    for n in range(n_chunks):
        st_ref[n] = state.astype(BF16)
        state = dl_ref[n] * state + u_ref[n]

    _round_robin([outputs(n) for n in range(n_chunks)], HGRN_INTERLEAVE)


def _hgrn_rec_call(q2, k2, lf2, v2, og2, out_gain2, batch, seq, cast_next=()):
    heads, t, _ = q2.shape
    c = HGRN_CHUNK
    assert seq % c == 0
    n_chunks = seq // c
    per_step = HGRN_HEADS_PER_STEP
    groups = heads // per_step
    assert heads % per_step == 0
    blk = pl.BlockSpec((per_step, seq, HGRN_DV), lambda b, g: (g, b, 0))
    job_operands, job_in, job_out, job_shapes = _cast_job_specs(
        cast_next, batch * groups, lambda b, g: b * groups + g)
    outs = pl.pallas_call(
        functools.partial(_hgrn_rec_kernel, chunk=c, n_cast=len(cast_next)),
        grid=(batch, groups),
        in_specs=[blk, blk, blk, blk, blk,
                  pl.BlockSpec((per_step, 1, HGRN_DV), lambda b, g: (g, 0, 0))] + job_in,
        out_specs=[blk] + job_out,
        out_shape=[jax.ShapeDtypeStruct((heads, t, HGRN_DV), BF16)] + job_shapes,
        scratch_shapes=[
            pltpu.VMEM((seq, HGRN_DK), BF16),
            pltpu.VMEM((n_chunks, c, c), BF16),
            pltpu.VMEM((n_chunks, HGRN_DV, HGRN_DK), F32),
            pltpu.VMEM((n_chunks, 1, HGRN_DK), F32),
            pltpu.VMEM((n_chunks, HGRN_DV, HGRN_DK), BF16),
        ],
        compiler_params=_cparams(2),
        name="hgrn_recurrence",
    )(q2, k2, lf2, v2, og2, out_gain2, *job_operands)
    return outs[0], [_Stacked(w) for w in outs[1:]]


def kernel(x, c, ada_w, ada_b, norm_g, ffn_w_in, ffn_w_down, attn_w_qkv, attn_w_o, attn_q_gain,
           attn_k_gain, attn_lambda, attn_subln_gain, rel_bias, hgrn_w_in, hgrn_w_o,
           hgrn_out_gain, hgrn_lb_logits):
    batch, seq, d = x.shape
    depth = ada_w.shape[0]
    mod = _ada_call(c, ada_w, ada_b).reshape(depth, batch, N_SUBLAYERS, 3, d)
    bias_tiles = _bias_call(rel_bias, ATTN_TILE)
    x2 = x.reshape(batch * seq, d)
    ffn_order = [(layer, half) for layer in range(depth) for half in range(2)]

    def next_ffn_weights(layer, half):
        nxt = ffn_order.index((layer, half)) + 1
        if nxt == len(ffn_order):
            return ()
        return (_Stacked(ffn_w_in, ffn_order[nxt]), _Stacked(ffn_w_down, ffn_order[nxt]))

    def mixer_in_weights(layer):
        j = layer // N_MIXERS
        return _Stacked(attn_w_qkv, (j,)) if layer % N_MIXERS == 0 else _Stacked(hgrn_w_in, (j,))

    def mixer_out_weights(layer):
        j = layer // N_MIXERS
        return _Stacked(attn_w_o, (j,)) if layer % N_MIXERS == 0 else _Stacked(hgrn_w_o, (j,))

    ffn_w = [_Stacked(ffn_w_in[0, 0].astype(BF16)), _Stacked(ffn_w_down[0, 0].astype(BF16))]
    first_mixer = mixer_in_weights(0)
    w_mix_in = _Stacked(first_mixer.array[first_mixer.lead].astype(BF16))
    for layer in range(depth):
        gains = norm_g[layer].reshape(N_SUBLAYERS, 1, d)
        x2, ffn_w = _ffn_call(x2, mod[layer, :, 0], gains[0], ffn_w[0], ffn_w[1], seq,
                              cast_next=next_ffn_weights(layer, 0))
        j = layer // N_MIXERS
        mixer_casts = (mixer_out_weights(layer),)
        if layer + 1 < depth:
            mixer_casts += (mixer_in_weights(layer + 1),)
        if layer % N_MIXERS == 0:
            reps = QKV_COLS // ATTN_HEAD_DIM
            q2, k2, v2 = _qkv_call(
                x2, mod[layer, :, 1], gains[1], w_mix_in,
                jnp.tile(attn_q_gain[j], reps).reshape(1, QKV_COLS),
                jnp.tile(attn_k_gain[j], reps).reshape(1, QKV_COLS), seq)
            lambda_init = 0.8 - 0.6 * math.exp(-0.3 * layer)
            o2, cast = _attn_call(q2, k2, v2, bias_tiles, attn_lambda[j],
                                  attn_subln_gain[j].reshape(1, ATTN_V_DIM), batch, seq, lambda_init,
                                  cast_next=mixer_casts)
        else:
            q2, k2, lf2, v2, og2 = _hgrn_in_call(
                x2, mod[layer, :, 1], gains[1], w_mix_in, hgrn_lb_logits, j, seq)
            o2, cast = _hgrn_rec_call(q2, k2, lf2, v2, og2,
                                      hgrn_out_gain[j].reshape(HGRN_HEADS, 1, HGRN_DV), batch, seq,
                                      cast_next=mixer_casts)
        w_o, w_mix_in = cast[0], (cast[1] if len(cast) > 1 else None)
        x2, ffn_w = _ffn_call(x2, mod[layer, :, 2], gains[2], ffn_w[0], ffn_w[1], seq,
                              mixer=(o2, mod[layer, :, 1], w_o),
                              cast_next=next_ffn_weights(layer, 1))
    return x2.reshape(batch, seq, d)
```

```python
import functools
import math
from typing import NamedTuple

import numpy as np
import jax
import jax.numpy as jnp
from jax import lax
from jax.experimental import pallas as pl
from jax.experimental.pallas import tpu as pltpu

F32 = jnp.float32
BF16 = jnp.bfloat16

EPS = 1e-6
N_SUBLAYERS = 3
N_MIXERS = 2
ATTN_HEADS = 8
ATTN_HEAD_DIM = 64
ATTN_V_DIM = 128
NUM_BUCKETS = 32
MAX_DISTANCE = 128
HGRN_HEADS = 8
HGRN_DK = 128
HGRN_DV = 128

LANES = 128
VMEM_LIMIT = 60 * 1024 * 1024
NEG = -1e30
LOG2E = math.log2(math.e)

MXU_COLS = 256
ATTN_TILE = 256
ATTN_HEADS_PER_STEP = 1
QKV_COLS = MXU_COLS
FFN_COLS = 2 * MXU_COLS
HGRN_COLS = MXU_COLS
HGRN_CHUNK = 128
HGRN_HEADS_PER_STEP = 1
ADA_COLS = 4608
HGRN_INTERLEAVE = 16


def _cparams(n_axes):
    return pltpu.CompilerParams(
        dimension_semantics=("arbitrary",) * n_axes, vmem_limit_bytes=VMEM_LIMIT)


def _resident(shape):
    return pl.BlockSpec(shape, lambda *_: (0,) * len(shape), pipeline_mode=pl.Buffered(1))


class _Stacked(NamedTuple):
    array: jax.Array
    lead: tuple = ()

    @property
    def shape(self):
        return self.array.shape[len(self.lead):]

    def spec(self):
        lead, tail = self.lead, self.shape
        return pl.BlockSpec((None,) * len(lead) + tail, lambda *_: lead + (0,) * len(tail),
                            pipeline_mode=pl.Buffered(1))


def _sigmoid(z):
    return 1.0 / (1.0 + jnp.exp(-z))


def _dot(a, b):
    return jnp.dot(a, b, preferred_element_type=F32)


def _dot_nt(a, b):
    return lax.dot_general(a, b, (((1,), (1,)), ((), ())), preferred_element_type=F32)


def _dot_tn(a, b):
    return lax.dot_general(a, b, (((0,), (0,)), ((), ())), preferred_element_type=F32)


def _pipelined(stages):
    pending = None
    for produce, consume in stages:
        value = produce()
        if pending is not None:
            pending[1](pending[0])
        pending = (value, consume)
    if pending is not None:
        pending[1](pending[0])


def _cast_job_specs(cast_next, steps, step_of):
    operands, in_specs, out_specs, out_shapes = [], [], [], []
    for w in cast_next:
        n_rows, n_cols = w.shape
        assert n_rows % (steps * 16) == 0
        lead, blk_rows = w.lead, n_rows // steps
        operands.append(w.array)
        in_specs.append(pl.BlockSpec((None,) * len(lead) + (blk_rows, n_cols),
                                     lambda *g, lead=lead: lead + (step_of(*g), 0)))
        out_specs.append(pl.BlockSpec((blk_rows, n_cols), lambda *g: (step_of(*g), 0)))
        out_shapes.append(jax.ShapeDtypeStruct((n_rows, n_cols), BF16))
    return operands, in_specs, out_specs, out_shapes


def _run_cast_jobs(src_refs, dst_refs):
    for src_ref, dst_ref in zip(src_refs, dst_refs):
        dst_ref[...] = src_ref[...].astype(BF16)


def _col_pieces(total, width):
    return [(lo, min(lo + width, total)) for lo in range(0, total, width)]


def _store_heads(ref, lo, hi, value):
    for col in range(lo, hi, LANES):
        ref[col // LANES] = value[:, col - lo:col - lo + LANES]


def _head_major(heads, rows):
    return pl.BlockSpec((heads, rows, LANES), lambda i: (0, i, 0))


def _norm_mod(x, gain, mod_ref):
    ms = jnp.mean(x * x, axis=-1, keepdims=True)
    h = x * lax.rsqrt(ms + EPS) * gain
    return h * (1.0 + mod_ref[0, 1:2, :]) + mod_ref[0, 0:1, :]


def _ada_kernel(c_ref, w_ref, b_ref, o_ref):
    c = c_ref[...]
    ca = (c * _sigmoid(c)).astype(BF16)
    o_ref[0] = _dot(ca, w_ref[0].astype(BF16)) + b_ref[0]


def _ada_call(c, ada_w, ada_b):
    depth, d, n = ada_w.shape
    b = c.shape[0]
    tn = ADA_COLS
    assert n % tn == 0
    return pl.pallas_call(
        _ada_kernel,
        grid=(depth, n // tn),
        in_specs=[
            pl.BlockSpec((b, d), lambda l, j: (0, 0)),
            pl.BlockSpec((1, d, tn), lambda l, j: (l, 0, j)),
            pl.BlockSpec((1, 1, tn), lambda l, j: (l, 0, j)),
        ],
        out_specs=pl.BlockSpec((1, b, tn), lambda l, j: (l, 0, j)),
        out_shape=jax.ShapeDtypeStruct((depth, b, n), F32),
        compiler_params=_cparams(2),
        name="ada_mod",
    )(c, ada_w, ada_b.reshape(depth, 1, n))


def _ffn_kernel(*refs, d_ff, with_mixer, n_cast):
    n_in = (8 if with_mixer else 5) + n_cast
    ins, outs, act_ref = refs[:n_in], refs[n_in:-1], refs[-1]
    if with_mixer:
        x_ref, mix_ref, mixmod_ref, wo_ref, mod_ref, g_ref, win_ref, wdn_ref = ins[:8]
    else:
        x_ref, mod_ref, g_ref, win_ref, wdn_ref = ins[:5]
    o_ref = outs[0]
    _run_cast_jobs(ins[n_in - n_cast:], outs[1:])

    if with_mixer:
        mix = jnp.concatenate([mix_ref[hd] for hd in range(mix_ref.shape[0])], axis=1)
        x = x_ref[...] + mixmod_ref[0, 2:3, :] * _dot(mix, wo_ref[...])
    else:
        x = x_ref[...]
    h = _norm_mod(x, g_ref[...], mod_ref).astype(BF16)

    def stage(lo, hi):
        def produce():
            return _dot(h, win_ref[:, lo:hi]), _dot(h, win_ref[:, d_ff + lo:d_ff + hi])

        def consume(ab):
            a, b = ab
            act_ref[:, lo:hi] = (a * _sigmoid(a) * b).astype(BF16)

        return produce, consume

    _pipelined([stage(lo, hi) for lo, hi in _col_pieces(d_ff, FFN_COLS)])
    y = _dot(act_ref[...], wdn_ref[...])
    o_ref[...] = x + 0.5 * mod_ref[0, 2:3, :] * y


def _ffn_call(x2, mod3, gain, w_in, w_down, seq, mixer=None, cast_next=()):
    t, d = x2.shape
    d_ff = w_down.shape[0]
    tm = min(1024, seq)
    assert t % tm == 0 and seq % tm == 0 and d_ff % MXU_COLS == 0
    steps = t // tm
    per_seq = seq // tm
    row = pl.BlockSpec((tm, d), lambda i: (i, 0))
    mod_spec = pl.BlockSpec((1, 3, d), lambda i: (i // per_seq, 0, 0))
    operands, in_specs = [x2], [row]
    if mixer is not None:
        mix, mix_mod3, w_o = mixer
        operands += [mix, mix_mod3, w_o.array]
        in_specs += [_head_major(mix.shape[0], tm), mod_spec, w_o.spec()]
    operands += [mod3, gain, w_in.array, w_down.array]
    in_specs += [mod_spec, _resident((1, d)), w_in.spec(), w_down.spec()]
    job_operands, job_in, job_out, job_shapes = _cast_job_specs(cast_next, steps, lambda i: i)
    operands += job_operands
    in_specs += job_in
    out_specs = [row] + job_out
    out_shapes = [jax.ShapeDtypeStruct((t, d), F32)] + job_shapes
    outs = pl.pallas_call(
        functools.partial(_ffn_kernel, d_ff=d_ff, with_mixer=mixer is not None,
                          n_cast=len(cast_next)),
        grid=(steps,),
        in_specs=in_specs,
        out_specs=out_specs,
        out_shape=out_shapes,
        scratch_shapes=[pltpu.VMEM((tm, d_ff), BF16)],
        compiler_params=_cparams(1),
        name="ffn_half_step",
    )(*operands)
    return outs[0], [_Stacked(w) for w in outs[1:]]


def _group_mean_sq(z, ones_blk):
    return _dot((z * z).astype(BF16), ones_blk)


def _qkv_kernel(x_ref, mod_ref, g_ref, w_ref, qg_ref, kg_ref, q_ref, k_ref, v_ref, *, d):
    h = _norm_mod(x_ref[...], g_ref[...], mod_ref).astype(BF16)
    tn = QKV_COLS
    r = lax.broadcasted_iota(jnp.int32, (tn, tn), 0) // ATTN_HEAD_DIM
    c = lax.broadcasted_iota(jnp.int32, (tn, tn), 1) // ATTN_HEAD_DIM
    ones_blk = jnp.where(r == c, 1.0 / ATTN_HEAD_DIM, 0.0).astype(BF16)
    q_gain = qg_ref[...] * (ATTN_HEAD_DIM ** -0.5 * LOG2E)
    k_gain = kg_ref[...]

    def normed(w_lo, out_ref, lo, hi, gain):
        def consume(z):
            _store_heads(out_ref, lo, hi,
                         (z * lax.rsqrt(_group_mean_sq(z, ones_blk) + EPS) * gain).astype(BF16))

        return (lambda: _dot(h, w_ref[:, w_lo + lo:w_lo + hi])), consume

    def plain(w_lo, out_ref, lo, hi):
        def consume(z):
            _store_heads(out_ref, lo, hi, z.astype(BF16))

        return (lambda: _dot(h, w_ref[:, w_lo + lo:w_lo + hi])), consume

    stages = []
    for lo, hi in _col_pieces(d, tn):
        stages += [normed(0, q_ref, lo, hi, q_gain), normed(d, k_ref, lo, hi, k_gain),
                   plain(2 * d, v_ref, lo, hi)]
    _pipelined(stages)


def _qkv_call(x2, mod3, gain, w_qkv, q_gain2, k_gain2, seq):
    t, d = x2.shape
    tm = min(1024, seq)
    assert t % tm == 0 and seq % tm == 0
    per_seq = seq // tm
    heads = d // LANES
    out = jax.ShapeDtypeStruct((heads, t, LANES), BF16)
    row = pl.BlockSpec((tm, d), lambda i: (i, 0))
    out_spec = _head_major(heads, tm)
    return pl.pallas_call(
        functools.partial(_qkv_kernel, d=d),
        grid=(t // tm,),
        in_specs=[
            row,
            pl.BlockSpec((1, 3, d), lambda i: (i // per_seq, 0, 0)),
            _resident((1, d)),
            w_qkv.spec(),
            _resident((1, QKV_COLS)),
            _resident((1, QKV_COLS)),
        ],
        out_specs=[out_spec, out_spec, out_spec],
        out_shape=[out, out, out],
        compiler_params=_cparams(1),
        name="attn_qkv_proj",
    )(x2, mod3, gain, w_qkv.array, q_gain2, k_gain2)


def _bucket_thresholds():
    max_exact = NUM_BUCKETS // 2
    dist = np.arange(0, 4 * MAX_DISTANCE)
    d_f = np.maximum(dist, 1).astype(np.float32)
    large = max_exact + (np.log(d_f / np.float32(max_exact)) / np.float32(math.log(MAX_DISTANCE / max_exact))
                         * np.float32(NUM_BUCKETS - max_exact)).astype(np.int32)
    bucket = np.where(dist < max_exact, dist, np.minimum(large, NUM_BUCKETS - 1))
    assert np.all(np.diff(bucket) >= 0) and bucket[-1] == NUM_BUCKETS - 1
    return [int(np.argmax(bucket >= b)) for b in range(NUM_BUCKETS)]


def _bias_kernel(rb_ref, o_ref, *, thresholds, tile):
    h = pl.program_id(0)
    i = lax.broadcasted_iota(jnp.int32, (tile, tile), 0)
    j = lax.broadcasted_iota(jnp.int32, (tile, tile), 1)
    for off in range(3):
        dist = off * tile + i - j
        val = jnp.full((tile, tile), rb_ref[0, h], F32)
        for b in range(1, NUM_BUCKETS):
            val = jnp.where(dist >= thresholds[b], rb_ref[b, h], val)
        o_ref[0, off] = jnp.where(dist >= 0, val * LOG2E, NEG)


def _bias_call(rel_bias, tile):
    thresholds = _bucket_thresholds()
    assert thresholds[NUM_BUCKETS - 1] <= tile + 1
    heads = rel_bias.shape[1]
    return pl.pallas_call(
        functools.partial(_bias_kernel, thresholds=thresholds, tile=tile),
        grid=(heads,),
        in_specs=[pl.BlockSpec(memory_space=pltpu.SMEM)],
        out_specs=pl.BlockSpec((1, 3, tile, tile), lambda h: (h, 0, 0, 0)),
        out_shape=jax.ShapeDtypeStruct((heads, 3, tile, tile), F32),
        compiler_params=_cparams(1),
        name="attn_rel_bias_tiles",
    )(rel_bias)


def _attn_kernel(*refs, tile, lambda_init, n_cast):
    n_in = 6 + n_cast
    q_ref, k_ref, v_ref, bias_ref, lam_ref, sg_ref = refs[:6]
    o_ref = refs[n_in]
    v1_ref, s0_ref, s1_ref, p0_ref, p1_ref, m0_ref, m1_ref, mf0_ref, mf1_ref = refs[n_in + 1 + n_cast:]
    _run_cast_jobs(refs[6:n_in], refs[n_in + 1:n_in + 1 + n_cast])
    lam = lam_ref[...]
    lam_full = (jnp.exp(jnp.sum(lam[0:1] * lam[1:2], axis=-1, keepdims=True))
                - jnp.exp(jnp.sum(lam[2:3] * lam[3:4], axis=-1, keepdims=True)) + lambda_init)
    scratch = (v1_ref, (s0_ref, s1_ref), (p0_ref, p1_ref), (m0_ref, m1_ref), (mf0_ref, mf1_ref))

    def one_head(hd, carry):
        _attn_head(q_ref.at[hd], k_ref.at[hd], v_ref.at[hd], bias_ref.at[hd], sg_ref, o_ref.at[hd],
                   scratch, lam_full, tile=tile, lambda_init=lambda_init)
        return carry

    lax.fori_loop(0, q_ref.shape[0], one_head, 0)


def _attn_head(q_ref, k_ref, v_ref, bias_ref, sg_ref, o_ref, scratch, lam_full, *, tile, lambda_init):
    v1_ref, s_refs, p_refs, m_refs, mfar_refs = scratch
    seq = q_ref.shape[0]
    dv = ATTN_V_DIM
    n_tiles = seq // tile
    v1_ref[:, :dv] = v_ref[...]
    v1_ref[:, dv:] = jnp.ones((seq, dv), BF16)
    lane = lax.broadcasted_iota(jnp.int32, (tile, dv), 1)

    def stacked_q(i):
        q = q_ref[i * tile:(i + 1) * tile, :]
        zero = jnp.zeros_like(q)
        return jnp.concatenate([jnp.where(lane < ATTN_HEAD_DIM, q, zero),
                                jnp.where(lane >= ATTN_HEAD_DIM, q, zero)], axis=0)

    far_bias = bias_ref[2, 0:1, 0:1]
    running_max = {}

    def is_far(i, j):
        return i - j >= 2

    def scores(i, j, qs):
        s = _dot_nt(qs, k_ref[j * tile:(j + 1) * tile, :])
        if not is_far(i, j):
            bt = bias_ref[i - j]
            s = s + jnp.concatenate([bt, bt], axis=0)
        s_refs[i % 2][:, j * tile:(j + 1) * tile] = s
        block_max = functools.reduce(
            jnp.maximum, [s[:, c:c + LANES] for c in range(0, tile, LANES)])
        key = (i, is_far(i, j))
        running_max[key] = (jnp.maximum(running_max[key], block_max)
                            if key in running_max else block_max)

    def row_max(i):
        m = jnp.max(running_max.pop((i, False)), axis=-1, keepdims=True)
        if (i, True) in running_max:
            m_far = jnp.max(running_max.pop((i, True)), axis=-1, keepdims=True) + far_bias
            m = jnp.maximum(m, m_far)
            mfar_refs[i % 2][...] = jnp.broadcast_to(m - far_bias, mfar_refs[i % 2].shape)
        m_refs[i % 2][...] = jnp.broadcast_to(m, m_refs[i % 2].shape)

    def probs(i, j):
        m = (mfar_refs if is_far(i, j) else m_refs)[i % 2][...]
        m_wide = jnp.concatenate([m] * (tile // LANES), axis=1)
        cols = slice(j * tile, (j + 1) * tile)
        p_refs[i % 2][:, cols] = jnp.exp2(s_refs[i % 2][:, cols] - m_wide).astype(BF16)

    def finish(i):
        n_keys = (i + 1) * tile
        acc = _dot(p_refs[i % 2][:, :n_keys], v1_ref[:n_keys, :])
        o_all = acc[:, :dv] / acc[:, dv:]
        o = o_all[:tile] - lam_full * o_all[tile:]
        ms = jnp.mean(o * o, axis=-1, keepdims=True)
        o = o * lax.rsqrt(ms + EPS) * sg_ref[...] * (1.0 - lambda_init)
        o_ref[i * tile:(i + 1) * tile, :] = o.astype(BF16)

    def score_and_prob_steps(i_scores, i_probs):
        qs = stacked_q(i_scores) if i_scores >= 0 else None
        for j in range(max(i_scores, i_probs) + 1):
            if j <= i_scores:
                scores(i_scores, j, qs)
            if j <= i_probs:
                probs(i_probs, j)
        if i_scores >= 0:
            row_max(i_scores)

    last = n_tiles - 1
    score_and_prob_steps(last, -1)
    score_and_prob_steps(last - 1, last)
    for i in range(last, -1, -1):
        finish(i)
        score_and_prob_steps(i - 2, i - 1)


def _attn_call(q2, k2, v2, bias_tiles, lam, subln_gain2, batch, seq, lambda_init, cast_next=()):
    heads, t, _ = q2.shape
    tile = ATTN_TILE
    per_step = ATTN_HEADS_PER_STEP
    groups = heads // per_step
    assert seq % tile == 0 and heads % per_step == 0
    blk = pl.BlockSpec((per_step, seq, ATTN_V_DIM), lambda b, g: (g, b, 0))
    job_operands, job_in, job_out, job_shapes = _cast_job_specs(
        cast_next, batch * groups, lambda b, g: b * groups + g)
    outs = pl.pallas_call(
        functools.partial(_attn_kernel, tile=tile, lambda_init=lambda_init, n_cast=len(cast_next)),
        grid=(batch, groups),
        in_specs=[
            blk, blk, blk,
            pl.BlockSpec((per_step, 3, tile, tile), lambda b, g: (g, 0, 0, 0)),
            _resident(lam.shape),
            _resident((1, ATTN_V_DIM)),
        ] + job_in,
        out_specs=[blk] + job_out,
        out_shape=[jax.ShapeDtypeStruct((heads, t, ATTN_V_DIM), BF16)] + job_shapes,
        scratch_shapes=[
            pltpu.VMEM((seq, 2 * ATTN_V_DIM), BF16),
            pltpu.VMEM((2 * tile, seq), F32),
            pltpu.VMEM((2 * tile, seq), F32),
            pltpu.VMEM((2 * tile, seq), BF16),
            pltpu.VMEM((2 * tile, seq), BF16),
            pltpu.VMEM((2 * tile, LANES), F32),
            pltpu.VMEM((2 * tile, LANES), F32),
            pltpu.VMEM((2 * tile, LANES), F32),
            pltpu.VMEM((2 * tile, LANES), F32),
        ],
        compiler_params=_cparams(2),
        name="diff_attention",
    )(q2, k2, v2, bias_tiles, lam, subln_gain2, *job_operands)
    return outs[0], [_Stacked(w) for w in outs[1:]]


def _hgrn_in_kernel(x_ref, mod_ref, g_ref, w_ref, lbl_ref, q_ref, k_ref, lf_ref, v_ref, og_ref,
                    *, d, layer_j):
    h = _norm_mod(x_ref[...], g_ref[...], mod_ref).astype(BF16)
    logits = lbl_ref[...]
    e = jnp.exp(logits - jnp.max(logits, axis=0, keepdims=True))
    p = e / jnp.sum(e, axis=0, keepdims=True)
    lb = jnp.zeros((1, d), F32)
    for i in range(1, layer_j + 1):
        lb = lb + p[i:i + 1, :]
    log_lb = jnp.log(lb)
    log_1m_lb = jnp.log1p(-lb)

    def dot_cols(w_lo, lo, hi):
        return lambda: _dot(h, w_ref[:, w_lo + lo:w_lo + hi])

    def forget_stage(lo, hi):
        def consume(f):
            ef = jnp.exp(-jnp.abs(f))
            one_plus = 1.0 + ef
            log_sig = jnp.minimum(f, 0.0) - jnp.log(one_plus)
            sig_neg = jnp.where(f >= 0.0, ef, 1.0) / one_plus
            if layer_j == 0:
                _store_heads(lf_ref, lo, hi, log_sig)
                _store_heads(k_ref, lo, hi, sig_neg)
            else:
                a = log_lb[:, lo:hi]
                b = log_1m_lb[:, lo:hi] + log_sig
                _store_heads(lf_ref, lo, hi,
                             jnp.maximum(a, b) + jnp.log(1.0 + jnp.exp(-jnp.abs(a - b))))
                _store_heads(k_ref, lo, hi, (1.0 - lb[:, lo:hi]) * sig_neg)

        return dot_cols(d, lo, hi), consume

    def query_stage(lo, hi):
        def consume(z):
            _store_heads(q_ref, lo, hi, z)

        return dot_cols(0, lo, hi), consume

    def value_stage(lo, hi):
        def consume(z):
            _store_heads(v_ref, lo, hi, z.astype(BF16))

        return dot_cols(2 * d, lo, hi), consume

    def gate_stage(lo, hi):
        def consume(g):
            _store_heads(og_ref, lo, hi, g * _sigmoid(g))

        return dot_cols(3 * d, lo, hi), consume

    stages = []
    for lo, hi in _col_pieces(d, HGRN_COLS):
        stages += [forget_stage(lo, hi), query_stage(lo, hi), gate_stage(lo, hi), value_stage(lo, hi)]
    _pipelined(stages)


def _hgrn_in_call(x2, mod3, gain, w_in, lb_logits, layer_j, seq):
    t, d = x2.shape
    tm = min(1024, seq)
    assert t % tm == 0 and seq % tm == 0
    per_seq = seq // tm
    heads = d // LANES
    row = pl.BlockSpec((tm, d), lambda i: (i, 0))
    out_spec = _head_major(heads, tm)
    f32_out = jax.ShapeDtypeStruct((heads, t, LANES), F32)
    return pl.pallas_call(
        functools.partial(_hgrn_in_kernel, d=d, layer_j=layer_j),
        grid=(t // tm,),
        in_specs=[
            row,
            pl.BlockSpec((1, 3, d), lambda i: (i // per_seq, 0, 0)),
            _resident((1, d)),
            w_in.spec(),
            _resident(lb_logits.shape),
        ],
        out_specs=[out_spec] * 5,
        out_shape=[f32_out, f32_out, f32_out, jax.ShapeDtypeStruct((heads, t, LANES), BF16), f32_out],
        compiler_params=_cparams(1),
        name="hgrn_in_proj",
    )(x2, mod3, gain, w_in.array, lb_logits)


def _rows_at(g_cum, half):
    c = g_cum.shape[0]
    if half >= 4:
        blk = 2 * half
        return jnp.concatenate(
            [jnp.broadcast_to(g_cum[p * blk + half - 1:p * blk + half, :], (blk, g_cum.shape[1]))
             for p in range(c // blk)], axis=0)
    row = lax.broadcasted_iota(jnp.int32, g_cum.shape, 0)
    if half == 2:
        r4 = row % 4
        up1 = pltpu.roll(g_cum, c - 1, 0)
        dn1 = pltpu.roll(g_cum, 1, 0)
        dn2 = pltpu.roll(g_cum, 2, 0)
        return jnp.where(r4 == 0, up1, jnp.where(r4 == 1, g_cum, jnp.where(r4 == 2, dn1, dn2)))
    assert half == 1
    return jnp.where(row % 2 == 0, g_cum, pltpu.roll(g_cum, 1, 0))


def _round_robin(generators, width):
    pending = list(generators)
    active = []
    while pending or active:
        while pending and len(active) < width:
            active.append(pending.pop(0))
        for gen in list(active):
            try:
                next(gen)
            except StopIteration:
                active.remove(gen)


def _hgrn_rec_kernel(*refs, chunk, n_cast):
    n_in = 6 + n_cast
    q_ref, k_ref, lf_ref, v_ref, og_ref, gain_ref = refs[:6]
    o_ref = refs[n_in]
    scratch = refs[n_in + 1 + n_cast:]
    _run_cast_jobs(refs[6:n_in], refs[n_in + 1:n_in + 1 + n_cast])

    def one_head(hd, carry):
        _hgrn_rec_head(q_ref.at[hd], k_ref.at[hd], lf_ref.at[hd], v_ref.at[hd], og_ref.at[hd],
                       gain_ref.at[hd], o_ref.at[hd], *scratch, chunk=chunk)
        return carry

    lax.fori_loop(0, q_ref.shape[0], one_head, 0)


def _hgrn_rec_head(q_ref, k_ref, lf_ref, v_ref, og_ref, gain_ref, o_ref,
                   qe_ref, a_ref, u_ref, dl_ref, st_ref, *, chunk):
    seq = q_ref.shape[0]
    c = chunk
    n_chunks = seq // c
    ri = lax.broadcasted_iota(jnp.int32, (c, c), 0)
    ci = lax.broadcasted_iota(jnp.int32, (c, c), 1)
    tri = jnp.where(ri >= ci, 1.0, 0.0).astype(BF16)
    pair_key = jnp.where(ri >= ci, ri ^ ci, -1)
    row = lax.broadcasted_iota(jnp.int32, (c, HGRN_DK), 0)

    def neg_abs(z):
        return lax.bitcast_convert_type(
            lax.bitcast_convert_type(z, jnp.int32) | jnp.int32(-2 ** 31), F32)

    def intra(n):
        sl = slice(n * c, (n + 1) * c)
        g = lf_ref[sl, :]
        q = q_ref[sl, :]
        k = k_ref[sl, :]
        g_hi = g.astype(BF16)
        r1 = g - g_hi.astype(F32)
        g_mid = r1.astype(BF16)
        g_lo = (r1 - g_mid.astype(F32)).astype(BF16)
        g2 = (_dot(tri, g_hi) + (_dot(tri, g_mid) + _dot(tri, g_lo))) * LOG2E
        g2_last = g2[c - 1:c, :]
        yield
        qe_ref[sl, :] = (q * jnp.exp2(g2)).astype(BF16)
        k_dec = (k * jnp.exp2(g2_last - g2)).astype(BF16)
        u_ref[n] = _dot_tn(v_ref[sl, :], k_dec)
        dl_ref[n] = jnp.exp2(g2_last)
        scores = jnp.where(pair_key == 0, _dot_nt(q.astype(BF16), k.astype(BF16)), 0.0)
        yield
        half = 1
        while half < c:
            e = jnp.exp2(neg_abs(g2 - _rows_at(g2, half)))
            if half < 8:
                level = _dot_nt((q * e).astype(BF16), (k * e).astype(BF16))
                scores = jnp.where(pair_key >= half, level, scores)
            else:
                qk = jnp.concatenate(
                    [(q if (b % 2) else k)[b * half:(b + 1) * half] for b in range(c // half)], axis=0)
                zf = qk * e
                z = zf.astype(BF16)
                z_right = jnp.concatenate(
                    [zf[b * half:(b + 1) * half] for b in range(1, c // half, 2)], axis=0).astype(BF16)
                level = _dot_nt(z_right, z)
                pieces = []
                for b in range(c // half):
                    rows = slice(b * half, (b + 1) * half)
                    piece = scores[rows]
                    if b % 2:
                        lv = level[(b // 2) * half:(b // 2 + 1) * half]
                        piece = jnp.where(pair_key[rows] >= half, lv, piece)
                    pieces.append(piece)
                scores = jnp.concatenate(pieces, axis=0)
            half *= 2
            yield
        a_ref[n] = scores.astype(BF16)

    def outputs(n):
        sl = slice(n * c, (n + 1) * c)
        o = _dot_nt(qe_ref[sl, :], st_ref[n]) + _dot(a_ref[n], v_ref[sl, :])
        yield
        ms = jnp.mean(o * o, axis=-1, keepdims=True)
        o_ref[sl, :] = (o * lax.rsqrt(ms + EPS) * gain_ref[...] * og_ref[sl, :]).astype(BF16)

    _round_robin([intra(n) for n in range(n_chunks)], HGRN_INTERLEAVE)

    state = jnp.zeros((HGRN_DV, HGRN_DK), F32)
    for n in range(n_chunks):
        st_ref[n] = state.astype(BF16)
        state = dl_ref[n] * state + u_ref[n]

    _round_robin([outputs(n) for n in range(n_chunks)], HGRN_INTERLEAVE)


def _hgrn_rec_call(q2, k2, lf2, v2, og2, out_gain2, batch, seq, cast_next=()):
    heads, t, _ = q2.shape
    c = HGRN_CHUNK
    assert seq % c == 0
    n_chunks = seq // c
    per_step = HGRN_HEADS_PER_STEP
    groups = heads // per_step
    assert heads % per_step == 0
    blk = pl.BlockSpec((per_step, seq, HGRN_DV), lambda b, g: (g, b, 0))
    job_operands, job_in, job_out, job_shapes = _cast_job_specs(
        cast_next, batch * groups, lambda b, g: b * groups + g)
    outs = pl.pallas_call(
        functools.partial(_hgrn_rec_kernel, chunk=c, n_cast=len(cast_next)),
        grid=(batch, groups),
        in_specs=[blk, blk, blk, blk, blk,
                  pl.BlockSpec((per_step, 1, HGRN_DV), lambda b, g: (g, 0, 0))] + job_in,
        out_specs=[blk] + job_out,
        out_shape=[jax.ShapeDtypeStruct((heads, t, HGRN_DV), BF16)] + job_shapes,
        scratch_shapes=[
            pltpu.VMEM((seq, HGRN_DK), BF16),
            pltpu.VMEM((n_chunks, c, c), BF16),
            pltpu.VMEM((n_chunks, HGRN_DV, HGRN_DK), F32),
            pltpu.VMEM((n_chunks, 1, HGRN_DK), F32),
            pltpu.VMEM((n_chunks, HGRN_DV, HGRN_DK), BF16),
        ],
        compiler_params=_cparams(2),
        name="hgrn_recurrence",
    )(q2, k2, lf2, v2, og2, out_gain2, *job_operands)
    return outs[0], [_Stacked(w) for w in outs[1:]]


def kernel(x, c, ada_w, ada_b, norm_g, ffn_w_in, ffn_w_down, attn_w_qkv, attn_w_o, attn_q_gain,
           attn_k_gain, attn_lambda, attn_subln_gain, rel_bias, hgrn_w_in, hgrn_w_o,
           hgrn_out_gain, hgrn_lb_logits):
    batch, seq, d = x.shape
    depth = ada_w.shape[0]
    mod = _ada_call(c, ada_w, ada_b).reshape(depth, batch, N_SUBLAYERS, 3, d)
    bias_tiles = _bias_call(rel_bias, ATTN_TILE)
    x2 = x.reshape(batch * seq, d)
    ffn_order = [(layer, half) for layer in range(depth) for half in range(2)]

    def next_ffn_weights(layer, half):
        nxt = ffn_order.index((layer, half)) + 1
        if nxt == len(ffn_order):
            return ()
        return (_Stacked(ffn_w_in, ffn_order[nxt]), _Stacked(ffn_w_down, ffn_order[nxt]))

    def mixer_in_weights(layer):
        j = layer // N_MIXERS
        return _Stacked(attn_w_qkv, (j,)) if layer % N_MIXERS == 0 else _Stacked(hgrn_w_in, (j,))

    def mixer_out_weights(layer):
        j = layer // N_MIXERS
        return _Stacked(attn_w_o, (j,)) if layer % N_MIXERS == 0 else _Stacked(hgrn_w_o, (j,))

    ffn_w = [_Stacked(ffn_w_in[0, 0].astype(BF16)), _Stacked(ffn_w_down[0, 0].astype(BF16))]
    first_mixer = mixer_in_weights(0)
    w_mix_in = _Stacked(first_mixer.array[first_mixer.lead].astype(BF16))
    for layer in range(depth):
        gains = norm_g[layer].reshape(N_SUBLAYERS, 1, d)
        x2, ffn_w = _ffn_call(x2, mod[layer, :, 0], gains[0], ffn_w[0], ffn_w[1], seq,
                              cast_next=next_ffn_weights(layer, 0))
        j = layer // N_MIXERS
        mixer_casts = (mixer_out_weights(layer),)
        if layer + 1 < depth:
            mixer_casts += (mixer_in_weights(layer + 1),)
        if layer % N_MIXERS == 0:
            reps = QKV_COLS // ATTN_HEAD_DIM
            q2, k2, v2 = _qkv_call(
                x2, mod[layer, :, 1], gains[1], w_mix_in,
                jnp.tile(attn_q_gain[j], reps).reshape(1, QKV_COLS),
                jnp.tile(attn_k_gain[j], reps).reshape(1, QKV_COLS), seq)
            lambda_init = 0.8 - 0.6 * math.exp(-0.3 * layer)
            o2, cast = _attn_call(q2, k2, v2, bias_tiles, attn_lambda[j],
                                  attn_subln_gain[j].reshape(1, ATTN_V_DIM), batch, seq, lambda_init,
                                  cast_next=mixer_casts)
        else:
            q2, k2, lf2, v2, og2 = _hgrn_in_call(
                x2, mod[layer, :, 1], gains[1], w_mix_in, hgrn_lb_logits, j, seq)
            o2, cast = _hgrn_rec_call(q2, k2, lf2, v2, og2,
                                      hgrn_out_gain[j].reshape(HGRN_HEADS, 1, HGRN_DV), batch, seq,
                                      cast_next=mixer_casts)
        w_o, w_mix_in = cast[0], (cast[1] if len(cast) > 1 else None)
        x2, ffn_w = _ffn_call(x2, mod[layer, :, 2], gains[2], ffn_w[0], ffn_w[1], seq,
                              mixer=(o2, mod[layer, :, 1], w_o),
                              cast_next=next_ffn_weights(layer, 1))
    return x2.reshape(batch, seq, d)
```

```python
import functools
import math
from typing import NamedTuple

import numpy as np
import jax
import jax.numpy as jnp
from jax import lax
from jax.experimental import pallas as pl
from jax.experimental.pallas import tpu as pltpu

F32 = jnp.float32
BF16 = jnp.bfloat16

EPS = 1e-6
N_SUBLAYERS = 3
N_MIXERS = 2
ATTN_HEAD_DIM = 64
ATTN_V_DIM = 128
NUM_BUCKETS = 32
MAX_DISTANCE = 128
HGRN_HEADS = 8
HGRN_DK = 128
HGRN_DV = 128

LANES = 128
BF16_SUBLANES = 16
VMEM_LIMIT = 60 * 1024 * 1024
NEG = -1e30
LOG2E = math.log2(math.e)

MXU_COLS = 256
ATTN_TILE = 256
ATTN_HEADS_PER_STEP = 1
QKV_COLS = MXU_COLS
FFN_COLS = 2 * MXU_COLS
HGRN_COLS = MXU_COLS
HGRN_CHUNK = 128
HGRN_HEADS_PER_STEP = 1
ADA_COLS = 4608
HGRN_INTERLEAVE = 16


def _cparams(n_axes):
    return pltpu.CompilerParams(
        dimension_semantics=("arbitrary",) * n_axes, vmem_limit_bytes=VMEM_LIMIT)


def _resident(shape):
    return pl.BlockSpec(shape, lambda *_: (0,) * len(shape), pipeline_mode=pl.Buffered(1))


class _Stacked(NamedTuple):
    array: jax.Array
    lead: tuple = ()

    @property
    def shape(self):
        return self.array.shape[len(self.lead):]

    def spec(self):
        lead, tail = self.lead, self.shape
        return pl.BlockSpec((None,) * len(lead) + tail, lambda *_: lead + (0,) * len(tail),
                            pipeline_mode=pl.Buffered(1))


def _sigmoid(z):
    return 1.0 / (1.0 + jnp.exp(-z))


def _dot(a, b):
    return jnp.dot(a, b, preferred_element_type=F32)


def _dot_nt(a, b):
    return lax.dot_general(a, b, (((1,), (1,)), ((), ())), preferred_element_type=F32)


def _dot_tn(a, b):
    return lax.dot_general(a, b, (((0,), (0,)), ((), ())), preferred_element_type=F32)


def _pipelined(stages):
    pending = None
    for produce, consume in stages:
        value = produce()
        if pending is not None:
            pending[1](pending[0])
        pending = (value, consume)
    if pending is not None:
        pending[1](pending[0])


def _cast_job_specs(cast_next, steps, step_of):
    operands, in_specs, out_specs, out_shapes = [], [], [], []
    for w in cast_next:
        n_rows, n_cols = w.shape
        assert n_rows % (steps * BF16_SUBLANES) == 0
        lead, blk_rows = w.lead, n_rows // steps
        operands.append(w.array)
        in_specs.append(pl.BlockSpec((None,) * len(lead) + (blk_rows, n_cols),
                                     lambda *g, lead=lead: lead + (step_of(*g), 0)))
        out_specs.append(pl.BlockSpec((blk_rows, n_cols), lambda *g: (step_of(*g), 0)))
        out_shapes.append(jax.ShapeDtypeStruct((n_rows, n_cols), BF16))
    return operands, in_specs, out_specs, out_shapes


def _run_cast_jobs(src_refs, dst_refs):
    for src_ref, dst_ref in zip(src_refs, dst_refs):
        dst_ref[...] = src_ref[...].astype(BF16)


def _col_pieces(total, width):
    return [(lo, min(lo + width, total)) for lo in range(0, total, width)]


def _store_heads(ref, lo, hi, value):
    for col in range(lo, hi, LANES):
        ref[col // LANES] = value[:, col - lo:col - lo + LANES]


def _head_major(heads, rows):
    return pl.BlockSpec((heads, rows, LANES), lambda i: (0, i, 0))


def _norm_mod(x, gain, mod_ref):
    ms = jnp.mean(x * x, axis=-1, keepdims=True)
    h = x * lax.rsqrt(ms + EPS) * gain
    return h * (1.0 + mod_ref[0, 1:2, :]) + mod_ref[0, 0:1, :]


def _ada_kernel(c_ref, w_ref, b_ref, o_ref):
    c = c_ref[...]
    ca = (c * _sigmoid(c)).astype(BF16)
    o_ref[0] = _dot(ca, w_ref[0].astype(BF16)) + b_ref[0]


def _ada_call(c, ada_w, ada_b):
    depth, d, n = ada_w.shape
    b = c.shape[0]
    tn = ADA_COLS
    assert n % tn == 0
    return pl.pallas_call(
        _ada_kernel,
        grid=(depth, n // tn),
        in_specs=[
            pl.BlockSpec((b, d), lambda l, j: (0, 0)),
            pl.BlockSpec((1, d, tn), lambda l, j: (l, 0, j)),
            pl.BlockSpec((1, 1, tn), lambda l, j: (l, 0, j)),
        ],
        out_specs=pl.BlockSpec((1, b, tn), lambda l, j: (l, 0, j)),
        out_shape=jax.ShapeDtypeStruct((depth, b, n), F32),
        compiler_params=_cparams(2),
        name="ada_mod",
    )(c, ada_w, ada_b.reshape(depth, 1, n))


def _ffn_kernel(*refs, d_ff, with_mixer, n_cast):
    n_in = (8 if with_mixer else 5) + n_cast
    ins, outs, act_ref = refs[:n_in], refs[n_in:-1], refs[-1]
    if with_mixer:
        x_ref, mix_ref, mixmod_ref, wo_ref, mod_ref, g_ref, win_ref, wdn_ref = ins[:8]
    else:
        x_ref, mod_ref, g_ref, win_ref, wdn_ref = ins[:5]
    o_ref = outs[0]
    _run_cast_jobs(ins[n_in - n_cast:], outs[1:])

    if with_mixer:
        mix = jnp.concatenate([mix_ref[hd] for hd in range(mix_ref.shape[0])], axis=1)
        x = x_ref[...] + mixmod_ref[0, 2:3, :] * _dot(mix, wo_ref[...])
    else:
        x = x_ref[...]
    h = _norm_mod(x, g_ref[...], mod_ref).astype(BF16)

    def stage(lo, hi):
        def produce():
            return _dot(h, win_ref[:, lo:hi]), _dot(h, win_ref[:, d_ff + lo:d_ff + hi])

        def consume(ab):
            a, b = ab
            act_ref[:, lo:hi] = (a * _sigmoid(a) * b).astype(BF16)

        return produce, consume

    _pipelined([stage(lo, hi) for lo, hi in _col_pieces(d_ff, FFN_COLS)])
    y = _dot(act_ref[...], wdn_ref[...])
    o_ref[...] = x + 0.5 * mod_ref[0, 2:3, :] * y


def _ffn_call(x2, mod3, gain, w_in, w_down, seq, mixer=None, cast_next=()):
    t, d = x2.shape
    d_ff = w_down.shape[0]
    tm = min(1024, seq)
    assert t % tm == 0 and seq % tm == 0 and d_ff % MXU_COLS == 0
    steps = t // tm
    per_seq = seq // tm
    row = pl.BlockSpec((tm, d), lambda i: (i, 0))
    mod_spec = pl.BlockSpec((1, 3, d), lambda i: (i // per_seq, 0, 0))
    operands, in_specs = [x2], [row]
    if mixer is not None:
        mix, mix_mod3, w_o = mixer
        operands += [mix, mix_mod3, w_o.array]
        in_specs += [_head_major(mix.shape[0], tm), mod_spec, w_o.spec()]
    operands += [mod3, gain, w_in.array, w_down.array]
    in_specs += [mod_spec, _resident((1, d)), w_in.spec(), w_down.spec()]
    job_operands, job_in, job_out, job_shapes = _cast_job_specs(cast_next, steps, lambda i: i)
    operands += job_operands
    in_specs += job_in
    out_specs = [row] + job_out
    out_shapes = [jax.ShapeDtypeStruct((t, d), F32)] + job_shapes
    outs = pl.pallas_call(
        functools.partial(_ffn_kernel, d_ff=d_ff, with_mixer=mixer is not None,
                          n_cast=len(cast_next)),
        grid=(steps,),
        in_specs=in_specs,
        out_specs=out_specs,
        out_shape=out_shapes,
        scratch_shapes=[pltpu.VMEM((tm, d_ff), BF16)],
        compiler_params=_cparams(1),
        name="ffn_half_step",
    )(*operands)
    return outs[0], [_Stacked(w) for w in outs[1:]]


def _group_mean_sq(z, ones_blk):
    return _dot((z * z).astype(BF16), ones_blk)


def _qkv_kernel(x_ref, mod_ref, g_ref, w_ref, qg_ref, kg_ref, q_ref, k_ref, v_ref, *, d):
    h = _norm_mod(x_ref[...], g_ref[...], mod_ref).astype(BF16)
    tn = QKV_COLS
    r = lax.broadcasted_iota(jnp.int32, (tn, tn), 0) // ATTN_HEAD_DIM
    c = lax.broadcasted_iota(jnp.int32, (tn, tn), 1) // ATTN_HEAD_DIM
    ones_blk = jnp.where(r == c, 1.0 / ATTN_HEAD_DIM, 0.0).astype(BF16)
    q_gain = qg_ref[...] * (ATTN_HEAD_DIM ** -0.5 * LOG2E)
    k_gain = kg_ref[...]

    def normed(w_lo, out_ref, lo, hi, gain):
        def consume(z):
            _store_heads(out_ref, lo, hi,
                         (z * lax.rsqrt(_group_mean_sq(z, ones_blk) + EPS) * gain).astype(BF16))

        return (lambda: _dot(h, w_ref[:, w_lo + lo:w_lo + hi])), consume

    def plain(w_lo, out_ref, lo, hi):
        def consume(z):
            _store_heads(out_ref, lo, hi, z.astype(BF16))

        return (lambda: _dot(h, w_ref[:, w_lo + lo:w_lo + hi])), consume

    stages = []
    for lo, hi in _col_pieces(d, tn):
        stages += [normed(0, q_ref, lo, hi, q_gain), normed(d, k_ref, lo, hi, k_gain),
                   plain(2 * d, v_ref, lo, hi)]
    _pipelined(stages)


def _qkv_call(x2, mod3, gain, w_qkv, q_gain2, k_gain2, seq):
    t, d = x2.shape
    tm = min(1024, seq)
    assert t % tm == 0 and seq % tm == 0
    per_seq = seq // tm
    heads = d // LANES
    out = jax.ShapeDtypeStruct((heads, t, LANES), BF16)
    row = pl.BlockSpec((tm, d), lambda i: (i, 0))
    out_spec = _head_major(heads, tm)
    return pl.pallas_call(
        functools.partial(_qkv_kernel, d=d),
        grid=(t // tm,),
        in_specs=[
            row,
            pl.BlockSpec((1, 3, d), lambda i: (i // per_seq, 0, 0)),
            _resident((1, d)),
            w_qkv.spec(),
            _resident((1, QKV_COLS)),
            _resident((1, QKV_COLS)),
        ],
        out_specs=[out_spec, out_spec, out_spec],
        out_shape=[out, out, out],
        compiler_params=_cparams(1),
        name="attn_qkv_proj",
    )(x2, mod3, gain, w_qkv.array, q_gain2, k_gain2)


def _bucket_thresholds():
    max_exact = NUM_BUCKETS // 2
    dist = np.arange(0, 4 * MAX_DISTANCE)
    d_f = np.maximum(dist, 1).astype(np.float32)
    large = max_exact + (np.log(d_f / np.float32(max_exact)) / np.float32(math.log(MAX_DISTANCE / max_exact))
                         * np.float32(NUM_BUCKETS - max_exact)).astype(np.int32)
    bucket = np.where(dist < max_exact, dist, np.minimum(large, NUM_BUCKETS - 1))
    assert np.all(np.diff(bucket) >= 0) and bucket[-1] == NUM_BUCKETS - 1
    return [int(np.argmax(bucket >= b)) for b in range(NUM_BUCKETS)]


def _bias_kernel(rb_ref, o_ref, *, thresholds, tile):
    h = pl.program_id(0)
    i = lax.broadcasted_iota(jnp.int32, (tile, tile), 0)
    j = lax.broadcasted_iota(jnp.int32, (tile, tile), 1)
    for off in range(3):
        dist = off * tile + i - j
        val = jnp.full((tile, tile), rb_ref[0, h], F32)
        for b in range(1, NUM_BUCKETS):
            val = jnp.where(dist >= thresholds[b], rb_ref[b, h], val)
        o_ref[0, off] = jnp.where(dist >= 0, val * LOG2E, NEG)


def _bias_call(rel_bias, tile):
    thresholds = _bucket_thresholds()
    assert thresholds[NUM_BUCKETS - 1] <= tile + 1
    heads = rel_bias.shape[1]
    return pl.pallas_call(
        functools.partial(_bias_kernel, thresholds=thresholds, tile=tile),
        grid=(heads,),
        in_specs=[pl.BlockSpec(memory_space=pltpu.SMEM)],
        out_specs=pl.BlockSpec((1, 3, tile, tile), lambda h: (h, 0, 0, 0)),
        out_shape=jax.ShapeDtypeStruct((heads, 3, tile, tile), F32),
        compiler_params=_cparams(1),
        name="attn_rel_bias_tiles",
    )(rel_bias)


def _attn_kernel(*refs, tile, lambda_init, n_cast):
    n_in = 6 + n_cast
    q_ref, k_ref, v_ref, bias_ref, lam_ref, sg_ref = refs[:6]
    o_ref = refs[n_in]
    v1_ref, s0_ref, s1_ref, p0_ref, p1_ref, m0_ref, m1_ref, mf0_ref, mf1_ref = refs[n_in + 1 + n_cast:]
    _run_cast_jobs(refs[6:n_in], refs[n_in + 1:n_in + 1 + n_cast])
    lam = lam_ref[...]
    lam_full = (jnp.exp(jnp.sum(lam[0:1] * lam[1:2], axis=-1, keepdims=True))
                - jnp.exp(jnp.sum(lam[2:3] * lam[3:4], axis=-1, keepdims=True)) + lambda_init)
    scratch = (v1_ref, (s0_ref, s1_ref), (p0_ref, p1_ref), (m0_ref, m1_ref), (mf0_ref, mf1_ref))

    def one_head(hd, carry):
        _attn_head(q_ref.at[hd], k_ref.at[hd], v_ref.at[hd], bias_ref.at[hd], sg_ref, o_ref.at[hd],
                   scratch, lam_full, tile=tile, lambda_init=lambda_init)
        return carry

    lax.fori_loop(0, q_ref.shape[0], one_head, 0)


def _attn_head(q_ref, k_ref, v_ref, bias_ref, sg_ref, o_ref, scratch, lam_full, *, tile, lambda_init):
    v1_ref, s_refs, p_refs, m_refs, mfar_refs = scratch
    seq = q_ref.shape[0]
    dv = ATTN_V_DIM
    n_tiles = seq // tile
    v1_ref[:, :dv] = v_ref[...]
    v1_ref[:, dv:] = jnp.ones((seq, dv), BF16)
    lane = lax.broadcasted_iota(jnp.int32, (tile, dv), 1)

    def stacked_q(i):
        q = q_ref[i * tile:(i + 1) * tile, :]
        zero = jnp.zeros_like(q)
        return jnp.concatenate([jnp.where(lane < ATTN_HEAD_DIM, q, zero),
                                jnp.where(lane >= ATTN_HEAD_DIM, q, zero)], axis=0)

    far_bias = bias_ref[2, 0:1, 0:1]
    running_max = {}

    def is_far(i, j):
        return i - j >= 2

    def scores(i, j, qs):
        s = _dot_nt(qs, k_ref[j * tile:(j + 1) * tile, :])
        if not is_far(i, j):
            bt = bias_ref[i - j]
            s = s + jnp.concatenate([bt, bt], axis=0)
        s_refs[i % 2][:, j * tile:(j + 1) * tile] = s
        block_max = functools.reduce(
            jnp.maximum, [s[:, c:c + LANES] for c in range(0, tile, LANES)])
        key = (i, is_far(i, j))
        running_max[key] = (jnp.maximum(running_max[key], block_max)
                            if key in running_max else block_max)

    def row_max(i):
        m = jnp.max(running_max.pop((i, False)), axis=-1, keepdims=True)
        if (i, True) in running_max:
            m_far = jnp.max(running_max.pop((i, True)), axis=-1, keepdims=True) + far_bias
            m = jnp.maximum(m, m_far)
            mfar_refs[i % 2][...] = jnp.broadcast_to(m - far_bias, mfar_refs[i % 2].shape)
        m_refs[i % 2][...] = jnp.broadcast_to(m, m_refs[i % 2].shape)

    def probs(i, j):
        m = (mfar_refs if is_far(i, j) else m_refs)[i % 2][...]
        m_wide = jnp.concatenate([m] * (tile // LANES), axis=1)
        cols = slice(j * tile, (j + 1) * tile)
        p_refs[i % 2][:, cols] = jnp.exp2(s_refs[i % 2][:, cols] - m_wide).astype(BF16)

    def finish(i):
        n_keys = (i + 1) * tile
        acc = _dot(p_refs[i % 2][:, :n_keys], v1_ref[:n_keys, :])
        o_all = acc[:, :dv] / acc[:, dv:]
        o = o_all[:tile] - lam_full * o_all[tile:]
        ms = jnp.mean(o * o, axis=-1, keepdims=True)
        o = o * lax.rsqrt(ms + EPS) * sg_ref[...] * (1.0 - lambda_init)
        o_ref[i * tile:(i + 1) * tile, :] = o.astype(BF16)

    def score_and_prob_steps(i_scores, i_probs):
        qs = stacked_q(i_scores) if i_scores >= 0 else None
        for j in range(max(i_scores, i_probs) + 1):
            if j <= i_scores:
                scores(i_scores, j, qs)
            if j <= i_probs:
                probs(i_probs, j)
        if i_scores >= 0:
            row_max(i_scores)

    last = n_tiles - 1
    score_and_prob_steps(last, -1)
    score_and_prob_steps(last - 1, last)
    for i in range(last, -1, -1):
        finish(i)
        score_and_prob_steps(i - 2, i - 1)


def _attn_call(q2, k2, v2, bias_tiles, lam, subln_gain2, batch, seq, lambda_init, cast_next=()):
    heads, t, _ = q2.shape
    tile = ATTN_TILE
    per_step = ATTN_HEADS_PER_STEP
    groups = heads // per_step
    assert seq % tile == 0 and heads % per_step == 0
    blk = pl.BlockSpec((per_step, seq, ATTN_V_DIM), lambda b, g: (g, b, 0))
    job_operands, job_in, job_out, job_shapes = _cast_job_specs(
        cast_next, batch * groups, lambda b, g: b * groups + g)
    outs = pl.pallas_call(
        functools.partial(_attn_kernel, tile=tile, lambda_init=lambda_init, n_cast=len(cast_next)),
        grid=(batch, groups),
        in_specs=[
            blk, blk, blk,
            pl.BlockSpec((per_step, 3, tile, tile), lambda b, g: (g, 0, 0, 0)),
            _resident(lam.shape),
            _resident((1, ATTN_V_DIM)),
        ] + job_in,
        out_specs=[blk] + job_out,
        out_shape=[jax.ShapeDtypeStruct((heads, t, ATTN_V_DIM), BF16)] + job_shapes,
        scratch_shapes=[
            pltpu.VMEM((seq, 2 * ATTN_V_DIM), BF16),
            pltpu.VMEM((2 * tile, seq), F32),
            pltpu.VMEM((2 * tile, seq), F32),
            pltpu.VMEM((2 * tile, seq), BF16),
            pltpu.VMEM((2 * tile, seq), BF16),
            pltpu.VMEM((2 * tile, LANES), F32),
            pltpu.VMEM((2 * tile, LANES), F32),
            pltpu.VMEM((2 * tile, LANES), F32),
            pltpu.VMEM((2 * tile, LANES), F32),
        ],
        compiler_params=_cparams(2),
        name="diff_attention",
    )(q2, k2, v2, bias_tiles, lam, subln_gain2, *job_operands)
    return outs[0], [_Stacked(w) for w in outs[1:]]


def _hgrn_in_kernel(x_ref, mod_ref, g_ref, w_ref, lbl_ref, q_ref, k_ref, lf_ref, v_ref, og_ref,
                    *, d, layer_j):
    h = _norm_mod(x_ref[...], g_ref[...], mod_ref).astype(BF16)
    logits = lbl_ref[...]
    e = jnp.exp(logits - jnp.max(logits, axis=0, keepdims=True))
    p = e / jnp.sum(e, axis=0, keepdims=True)
    lb = jnp.zeros((1, d), F32)
    for i in range(1, layer_j + 1):
        lb = lb + p[i:i + 1, :]
    log_lb = jnp.log(lb)
    log_1m_lb = jnp.log1p(-lb)

    def dot_cols(w_lo, lo, hi):
        return lambda: _dot(h, w_ref[:, w_lo + lo:w_lo + hi])

    def forget_stage(lo, hi):
        def consume(f):
            ef = jnp.exp(-jnp.abs(f))
            one_plus = 1.0 + ef
            log_sig = jnp.minimum(f, 0.0) - jnp.log(one_plus)
            sig_neg = jnp.where(f >= 0.0, ef, 1.0) / one_plus
            if layer_j == 0:
                _store_heads(lf_ref, lo, hi, log_sig)
                _store_heads(k_ref, lo, hi, sig_neg)
            else:
                a = log_lb[:, lo:hi]
                b = log_1m_lb[:, lo:hi] + log_sig
                _store_heads(lf_ref, lo, hi,
                             jnp.maximum(a, b) + jnp.log(1.0 + jnp.exp(-jnp.abs(a - b))))
                _store_heads(k_ref, lo, hi, (1.0 - lb[:, lo:hi]) * sig_neg)

        return dot_cols(d, lo, hi), consume

    def query_stage(lo, hi):
        def consume(z):
            _store_heads(q_ref, lo, hi, z)

        return dot_cols(0, lo, hi), consume

    def value_stage(lo, hi):
        def consume(z):
            _store_heads(v_ref, lo, hi, z.astype(BF16))

        return dot_cols(2 * d, lo, hi), consume

    def gate_stage(lo, hi):
        def consume(g):
            _store_heads(og_ref, lo, hi, g * _sigmoid(g))

        return dot_cols(3 * d, lo, hi), consume

    stages = []
    for lo, hi in _col_pieces(d, HGRN_COLS):
        stages += [forget_stage(lo, hi), query_stage(lo, hi), gate_stage(lo, hi), value_stage(lo, hi)]
    _pipelined(stages)


def _hgrn_in_call(x2, mod3, gain, w_in, lb_logits, layer_j, seq):
    t, d = x2.shape
    tm = min(512, seq)
    assert t % tm == 0 and seq % tm == 0
    per_seq = seq // tm
    heads = d // LANES
    row = pl.BlockSpec((tm, d), lambda i: (i, 0))
    out_spec = _head_major(heads, tm)
    f32_out = jax.ShapeDtypeStruct((heads, t, LANES), F32)
    return pl.pallas_call(
        functools.partial(_hgrn_in_kernel, d=d, layer_j=layer_j),
        grid=(t // tm,),
        in_specs=[
            row,
            pl.BlockSpec((1, 3, d), lambda i: (i // per_seq, 0, 0)),
            _resident((1, d)),
            w_in.spec(),
            _resident(lb_logits.shape),
        ],
        out_specs=[out_spec] * 5,
        out_shape=[f32_out, f32_out, f32_out, jax.ShapeDtypeStruct((heads, t, LANES), BF16), f32_out],
        compiler_params=_cparams(1),
        name="hgrn_in_proj",
    )(x2, mod3, gain, w_in.array, lb_logits)


def _rows_at(g_cum, half):
    c = g_cum.shape[0]
    if half >= 4:
        blk = 2 * half
        return jnp.concatenate(
            [jnp.broadcast_to(g_cum[p * blk + half - 1:p * blk + half, :], (blk, g_cum.shape[1]))
             for p in range(c // blk)], axis=0)
    row = lax.broadcasted_iota(jnp.int32, g_cum.shape, 0)
    if half == 2:
        r4 = row % 4
        up1 = pltpu.roll(g_cum, c - 1, 0)
        dn1 = pltpu.roll(g_cum, 1, 0)
        dn2 = pltpu.roll(g_cum, 2, 0)
        return jnp.where(r4 == 0, up1, jnp.where(r4 == 1, g_cum, jnp.where(r4 == 2, dn1, dn2)))
    assert half == 1
    return jnp.where(row % 2 == 0, g_cum, pltpu.roll(g_cum, 1, 0))


def _round_robin(generators, width):
    pending = list(generators)
    active = []
    while pending or active:
        while pending and len(active) < width:
            active.append(pending.pop(0))
        for gen in list(active):
            try:
                next(gen)
            except StopIteration:
                active.remove(gen)


def _hgrn_rec_kernel(*refs, chunk, n_cast):
    n_in = 6 + n_cast
    q_ref, k_ref, lf_ref, v_ref, og_ref, gain_ref = refs[:6]
    o_ref = refs[n_in]
    scratch = refs[n_in + 1 + n_cast:]
    _run_cast_jobs(refs[6:n_in], refs[n_in + 1:n_in + 1 + n_cast])

    def one_head(hd, carry):
        _hgrn_rec_head(q_ref.at[hd], k_ref.at[hd], lf_ref.at[hd], v_ref.at[hd], og_ref.at[hd],
                       gain_ref.at[hd], o_ref.at[hd], *scratch, chunk=chunk)
        return carry

    lax.fori_loop(0, q_ref.shape[0], one_head, 0)


def _hgrn_rec_head(q_ref, k_ref, lf_ref, v_ref, og_ref, gain_ref, o_ref,
                   qe_ref, a_ref, u_ref, dl_ref, st_ref, *, chunk):
    seq = q_ref.shape[0]
    c = chunk
    n_chunks = seq // c
    ri = lax.broadcasted_iota(jnp.int32, (c, c), 0)
    ci = lax.broadcasted_iota(jnp.int32, (c, c), 1)
    tri = jnp.where(ri >= ci, 1.0, 0.0).astype(BF16)
    pair_key = jnp.where(ri >= ci, ri ^ ci, -1)

    def neg_abs(z):
        return lax.bitcast_convert_type(
            lax.bitcast_convert_type(z, jnp.int32) | jnp.int32(-2 ** 31), F32)

    def intra(n):
        sl = slice(n * c, (n + 1) * c)
        g = lf_ref[sl, :]
        q = q_ref[sl, :]
        k = k_ref[sl, :]
        g_hi = g.astype(BF16)
        r1 = g - g_hi.astype(F32)
        g_mid = r1.astype(BF16)
        g_lo = (r1 - g_mid.astype(F32)).astype(BF16)
        g2 = (_dot(tri, g_hi) + (_dot(tri, g_mid) + _dot(tri, g_lo))) * LOG2E
        g2_last = g2[c - 1:c, :]
        yield
        qe_ref[sl, :] = (q * jnp.exp2(g2)).astype(BF16)
        k_dec = (k * jnp.exp2(g2_last - g2)).astype(BF16)
        u_ref[n] = _dot_tn(v_ref[sl, :], k_dec)
        dl_ref[n] = jnp.exp2(g2_last)
        scores = jnp.where(pair_key == 0, _dot_nt(q.astype(BF16), k.astype(BF16)), 0.0)
        yield
        half = 1
        while half < c:
            e = jnp.exp2(neg_abs(g2 - _rows_at(g2, half)))
            if half < 8:
                level = _dot_nt((q * e).astype(BF16), (k * e).astype(BF16))
                scores = jnp.where(pair_key >= half, level, scores)
            else:
                qk = jnp.concatenate(
                    [(q if (b % 2) else k)[b * half:(b + 1) * half] for b in range(c // half)], axis=0)
                zf = qk * e
                z = zf.astype(BF16)
                z_right = jnp.concatenate(
                    [zf[b * half:(b + 1) * half] for b in range(1, c // half, 2)], axis=0).astype(BF16)
                level = _dot_nt(z_right, z)
                pieces = []
                for b in range(c // half):
                    rows = slice(b * half, (b + 1) * half)
                    piece = scores[rows]
                    if b % 2:
                        lv = level[(b // 2) * half:(b // 2 + 1) * half]
                        piece = jnp.where(pair_key[rows] >= half, lv, piece)
                    pieces.append(piece)
                scores = jnp.concatenate(pieces, axis=0)
            half *= 2
            yield
        a_ref[n] = scores.astype(BF16)

    def outputs(n):
        sl = slice(n * c, (n + 1) * c)
        o = _dot_nt(qe_ref[sl, :], st_ref[n]) + _dot(a_ref[n], v_ref[sl, :])
        yield
        ms = jnp.mean(o * o, axis=-1, keepdims=True)
        o_ref[sl, :] = (o * lax.rsqrt(ms + EPS) * gain_ref[...] * og_ref[sl, :]).astype(BF16)

    _round_robin([intra(n) for n in range(n_chunks)], HGRN_INTERLEAVE)

    state = jnp.zeros((HGRN_DV, HGRN_DK), F32)
    for n in range(n_chunks):
        st_ref[n] = state.astype(BF16)
        state = dl_ref[n] * state + u_ref[n]

    _round_robin([outputs(n) for n in range(n_chunks)], HGRN_INTERLEAVE)


def _hgrn_rec_call(q2, k2, lf2, v2, og2, out_gain2, batch, seq, cast_next=()):
    heads, t, _ = q2.shape
    c = HGRN_CHUNK
    assert seq % c == 0
    n_chunks = seq // c
    per_step = HGRN_HEADS_PER_STEP
    groups = heads // per_step
    assert heads % per_step == 0
    blk = pl.BlockSpec((per_step, seq, HGRN_DV), lambda b, g: (g, b, 0))
    job_operands, job_in, job_out, job_shapes = _cast_job_specs(
        cast_next, batch * groups, lambda b, g: b * groups + g)
    outs = pl.pallas_call(
        functools.partial(_hgrn_rec_kernel, chunk=c, n_cast=len(cast_next)),
        grid=(batch, groups),
        in_specs=[blk, blk, blk, blk, blk,
                  pl.BlockSpec((per_step, 1, HGRN_DV), lambda b, g: (g, 0, 0))] + job_in,
        out_specs=[blk] + job_out,
        out_shape=[jax.ShapeDtypeStruct((heads, t, HGRN_DV), BF16)] + job_shapes,
        scratch_shapes=[
            pltpu.VMEM((seq, HGRN_DK), BF16),
            pltpu.VMEM((n_chunks, c, c), BF16),
            pltpu.VMEM((n_chunks, HGRN_DV, HGRN_DK), F32),
            pltpu.VMEM((n_chunks, 1, HGRN_DK), F32),
            pltpu.VMEM((n_chunks, HGRN_DV, HGRN_DK), BF16),
        ],
        compiler_params=_cparams(2),
        name="hgrn_recurrence",
    )(q2, k2, lf2, v2, og2, out_gain2, *job_operands)
    return outs[0], [_Stacked(w) for w in outs[1:]]


def kernel(x, c, ada_w, ada_b, norm_g, ffn_w_in, ffn_w_down, attn_w_qkv, attn_w_o, attn_q_gain,
           attn_k_gain, attn_lambda, attn_subln_gain, rel_bias, hgrn_w_in, hgrn_w_o,
           hgrn_out_gain, hgrn_lb_logits):
    batch, seq, d = x.shape
    depth = ada_w.shape[0]
    mod = _ada_call(c, ada_w, ada_b).reshape(depth, batch, N_SUBLAYERS, 3, d)
    bias_tiles = _bias_call(rel_bias, ATTN_TILE)
    x2 = x.reshape(batch * seq, d)
    ffn_order = [(layer, half) for layer in range(depth) for half in range(2)]

    def next_ffn_weights(layer, half):
        nxt = ffn_order.index((layer, half)) + 1
        if nxt == len(ffn_order):
            return ()
        return (_Stacked(ffn_w_in, ffn_order[nxt]), _Stacked(ffn_w_down, ffn_order[nxt]))

    def mixer_in_weights(layer):
        j = layer // N_MIXERS
        return _Stacked(attn_w_qkv, (j,)) if layer % N_MIXERS == 0 else _Stacked(hgrn_w_in, (j,))

    def mixer_out_weights(layer):
        j = layer // N_MIXERS
        return _Stacked(attn_w_o, (j,)) if layer % N_MIXERS == 0 else _Stacked(hgrn_w_o, (j,))

    ffn_w = [_Stacked(ffn_w_in[0, 0].astype(BF16)), _Stacked(ffn_w_down[0, 0].astype(BF16))]
    first_mixer = mixer_in_weights(0)
    w_mix_in = _Stacked(first_mixer.array[first_mixer.lead].astype(BF16))
    for layer in range(depth):
        gains = norm_g[layer].reshape(N_SUBLAYERS, 1, d)
        x2, ffn_w = _ffn_call(x2, mod[layer, :, 0], gains[0], ffn_w[0], ffn_w[1], seq,
                              cast_next=next_ffn_weights(layer, 0))
        j = layer // N_MIXERS
        mixer_casts = (mixer_out_weights(layer),)
        if layer + 1 < depth:
            mixer_casts += (mixer_in_weights(layer + 1),)
        if layer % N_MIXERS == 0:
            reps = QKV_COLS // ATTN_HEAD_DIM
            q2, k2, v2 = _qkv_call(
                x2, mod[layer, :, 1], gains[1], w_mix_in,
                jnp.tile(attn_q_gain[j], reps).reshape(1, QKV_COLS),
                jnp.tile(attn_k_gain[j], reps).reshape(1, QKV_COLS), seq)
            lambda_init = 0.8 - 0.6 * math.exp(-0.3 * layer)
            o2, cast = _attn_call(q2, k2, v2, bias_tiles, attn_lambda[j],
                                  attn_subln_gain[j].reshape(1, ATTN_V_DIM), batch, seq, lambda_init,
                                  cast_next=mixer_casts)
        else:
            q2, k2, lf2, v2, og2 = _hgrn_in_call(
                x2, mod[layer, :, 1], gains[1], w_mix_in, hgrn_lb_logits, j, seq)
            o2, cast = _hgrn_rec_call(q2, k2, lf2, v2, og2,
                                      hgrn_out_gain[j].reshape(HGRN_HEADS, 1, HGRN_DV), batch, seq,
                                      cast_next=mixer_casts)
        w_o, w_mix_in = cast[0], (cast[1] if len(cast) > 1 else None)
        x2, ffn_w = _ffn_call(x2, mod[layer, :, 2], gains[2], ffn_w[0], ffn_w[1], seq,
                              mixer=(o2, mod[layer, :, 1], w_o),
                              cast_next=next_ffn_weights(layer, 1))
    return x2.reshape(batch, seq, d)
```
